```python
import jax, jax.numpy as jnp
from jax import lax
import numpy as np

D_MODEL = 1024
BATCH = 16
SEQ = 2048
DEPTH = 1

CTX_LEN = 256
GRID_W = 64
ROWS_PER_CHUNK = 2
GMLP_CHUNK = ROWS_PER_CHUNK * GRID_W
MLSTM_CHUNK = 128
MLSTM_WIDTH = D_MODEL
N_HEADS = 4
HEAD_DIM = MLSTM_WIDTH // N_HEADS
N_GATES = 4 * N_HEADS
CONV_W = 3
GMLP_WIDTH = D_MODEL
GMLP_GROUP_DIM = 128
GMLP_GROUPS = GMLP_WIDTH // GMLP_GROUP_DIM
D_FF = 4 * D_MODEL
NEG = -1e30
EPS = 1e-6

OFF_Q = 0
OFF_K = OFF_Q + MLSTM_WIDTH
OFF_V = OFF_K + MLSTM_WIDTH
OFF_G = OFF_V + MLSTM_WIDTH
OFF_O = OFF_G + N_GATES
OFF_U = OFF_O + MLSTM_WIDTH
OFF_VG = OFF_U + GMLP_WIDTH
OFF_GA = OFF_VG + GMLP_WIDTH
OFF_GB = OFF_GA + D_MODEL
P_TOTAL = OFF_GB + D_MODEL

kernel_name = 'hybrid_mlstm_gmlp_dit_block'


def rmsnorm(x, g):
    xf = x.astype(jnp.float32)
    y = xf * lax.rsqrt(jnp.mean(xf * xf, axis=-1, keepdims=True) + EPS)
    return (y * g.astype(jnp.float32)).astype(x.dtype)


def modulate(xn, shift, scale):
    return xn * (1.0 + scale) + shift


def short_conv(x, w):
    pad = CONV_W // 2
    L = x.shape[1]
    xp = jnp.pad(x, ((0, 0), (pad, pad), (0, 0)))
    out = xp[:, 0:L] * w[0]
    for j in range(1, CONV_W):
        out = out + xp[:, j:j + L] * w[j]
    return out


def mlstm_inputs(p, conv_w, b_gate):
    B, L, _ = p.shape
    qk = jax.nn.silu(short_conv(p[..., OFF_Q:OFF_V], conv_w))
    q = qk[..., :MLSTM_WIDTH].reshape(B, L, N_HEADS, HEAD_DIM)
    k = (qk[..., MLSTM_WIDTH:] * (HEAD_DIM ** -0.5)).reshape(B, L, N_HEADS, HEAD_DIM)
    v = p[..., OFF_V:OFF_G].reshape(B, L, N_HEADS, HEAD_DIM)
    g = p[..., OFF_G:OFF_O].astype(jnp.float32) + b_gate.astype(jnp.float32)
    gi_f, gf_f, gi_b, gf_b = jnp.split(g, 4, axis=-1)
    fwd = (gi_f, jax.nn.log_sigmoid(gf_f))
    bwd = (gi_b, jax.nn.log_sigmoid(gf_b))
    return q, k, v, fwd, bwd


def init_state(B):
    C = jnp.zeros((B, N_HEADS, HEAD_DIM, HEAD_DIM), jnp.float32)
    n = jnp.zeros((B, N_HEADS, HEAD_DIM), jnp.float32)
    m = jnp.zeros((B, N_HEADS), jnp.float32)
    return (C, n, m)


def mlstm_chunked(q, k, v, log_i, log_f, state, return_h):
    B, L, H, d = q.shape
    nc = L // MLSTM_CHUNK

    def to_chunks(a):
        a = a.astype(jnp.float32).reshape((B, nc, MLSTM_CHUNK) + a.shape[2:])
        return jnp.moveaxis(a, (1, 3), (0, 2))

    xs = (to_chunks(q), to_chunks(k), to_chunks(v), to_chunks(log_i), to_chunks(log_f))
    tril = jnp.tril(jnp.ones((MLSTM_CHUNK, MLSTM_CHUNK), dtype=bool))

    def body(carry, inp):
        C, n, m = carry
        qc, kc, vc, li, lf = inp
        b = jnp.cumsum(lf, axis=-1)
        b_last = b[..., -1]
        w_end = b_last[..., None] - b + li
        m_new = jnp.maximum(b_last + m, jnp.max(w_end, axis=-1))
        e = jnp.exp(w_end - m_new[..., None])
        s_prev = jnp.exp(b_last + m - m_new)
        C_new = s_prev[..., None, None] * C + jnp.einsum('bhj,bhjd,bhjv->bhdv', e, kc, vc)
        n_new = s_prev[..., None] * n + jnp.einsum('bhj,bhjd->bhd', e, kc)
        if not return_h:
            return (C_new, n_new, m_new), None
        d_intra = jnp.where(tril, b[..., :, None] - b[..., None, :] + li[..., None, :], NEG)
        inter = b + m[..., None]
        m_i = jnp.maximum(inter, jnp.max(d_intra, axis=-1))
        w = jnp.exp(d_intra - m_i[..., None])
        s_inter = jnp.exp(inter - m_i)
        a = w * jnp.einsum('bhid,bhjd->bhij', qc, kc)
        num = jnp.einsum('bhij,bhjv->bhiv', a, vc) + s_inter[..., None] * jnp.einsum('bhid,bhdv->bhiv', qc, C)
        den = jnp.sum(a, axis=-1) + s_inter * jnp.einsum('bhid,bhd->bhi', qc, n)
        h = num / jnp.maximum(jnp.abs(den), jnp.exp(-m_i))[..., None]
        return (C_new, n_new, m_new), h

    state, hs = lax.scan(body, state, xs)
    if not return_h:
        return None, state
    h = jnp.moveaxis(hs, (0, 2), (1, 3)).reshape(B, L, H, d)
    return h, state


def bidir_mlstm(lat, ctx_in, need_ctx_h):
    qx, kx, vx, (lixf, lfxf), (lixb, lfxb) = lat
    qc, kc, vc, (licf, lfcf), (licb, lfcb) = ctx_in
    st0 = init_state(qx.shape[0])
    rev = lambda a: a[:, ::-1]
    hcf, st_f = mlstm_chunked(qc, kc, vc, licf, lfcf, st0, need_ctx_h)
    hxf, _ = mlstm_chunked(qx, kx, vx, lixf, lfxf, st_f, True)
    hcb, st_b = mlstm_chunked(rev(qc), rev(kc), rev(vc), rev(licb), rev(lfcb), st0, need_ctx_h)
    hxb, _ = mlstm_chunked(rev(qx), rev(kx), rev(vx), rev(lixb), rev(lfxb), st_b, True)
    hx = hxf + rev(hxb)
    hc = (hcf + rev(hcb)) if need_ctx_h else None
    return hx, hc


def spatial_gate(u, vg, g_sgu, w_s, b_s, n_chunks):
    B, L, _ = u.shape
    vn = rmsnorm(vg, g_sgu).reshape(B, n_chunks, GMLP_CHUNK, GMLP_GROUPS, GMLP_GROUP_DIM)
    s = jnp.einsum('gpq,bnqgc->bnpgc', w_s, vn) + jnp.transpose(b_s)[None, None, :, :, None]
    return u * s.reshape(B, L, GMLP_WIDTH)


def merge_branches(p, h_m, n_chunks, g_mh, w_a, w_s, b_s, g_sgu, w_b, w_out):
    B, L, _ = p.shape
    o = p[..., OFF_O:OFF_U]
    u = jax.nn.gelu(p[..., OFF_U:OFF_VG])
    vg = jax.nn.gelu(p[..., OFF_VG:OFF_GA])
    ga = p[..., OFF_GA:OFF_GB]
    gb = p[..., OFF_GB:P_TOTAL]
    hm = (jax.nn.sigmoid(o) * h_m.reshape(B, L, MLSTM_WIDTH).astype(p.dtype)).reshape(B, L, N_HEADS, HEAD_DIM)
    hm = rmsnorm(hm, g_mh.reshape(N_HEADS, HEAD_DIM)).reshape(B, L, MLSTM_WIDTH)
    ya = hm @ w_a
    yb = spatial_gate(u, vg, g_sgu, w_s, b_s, n_chunks) @ w_b
    y = jax.nn.sigmoid(ga) * ya + jax.nn.sigmoid(gb) * yb
    return y @ w_out


def sq_relu_mlp(xn, w1, w2):
    h = jax.nn.relu(xn @ w1)
    return (h * h) @ w2


def setup_inputs(seed: int = 0) -> dict:
    key = jax.random.key(seed)
    ks = jax.random.split(key, 24)
    f32 = jnp.float32
    nrm = lambda k, shape, scale: jax.random.normal(k, shape, f32) * scale
    i_bias = nrm(ks[10], (DEPTH, N_HEADS), 0.1)
    f_bias = 3.0 + 3.0 * jnp.linspace(0.0, 1.0, N_HEADS, dtype=f32)[None] + nrm(ks[11], (DEPTH, N_HEADS), 0.1)
    i_bias_b = nrm(ks[12], (DEPTH, N_HEADS), 0.1)
    f_bias_b = 3.0 + 3.0 * jnp.linspace(0.0, 1.0, N_HEADS, dtype=f32)[None] + nrm(ks[13], (DEPTH, N_HEADS), 0.1)
    return {
        'x': nrm(ks[0], (BATCH, SEQ, D_MODEL), 1.0),
        'c': nrm(ks[1], (BATCH, D_MODEL), 1.0),
        'ctx': nrm(ks[2], (BATCH, CTX_LEN, D_MODEL), 1.0),
        'c_ctx': nrm(ks[3], (D_MODEL,), 1.0),
        'norm1': 1.0 + nrm(ks[4], (DEPTH, D_MODEL), 0.05),
        'norm2': 1.0 + nrm(ks[5], (DEPTH, D_MODEL), 0.05),
        'w_mod': nrm(ks[6], (DEPTH, D_MODEL, 6 * D_MODEL), 0.5 * D_MODEL ** -0.5),
        'b_mod': nrm(ks[7], (DEPTH, 6 * D_MODEL), 0.02),
        'w_in': nrm(ks[8], (DEPTH, D_MODEL, P_TOTAL), D_MODEL ** -0.5),
        'conv_qk': nrm(ks[9], (DEPTH, CONV_W, 2 * MLSTM_WIDTH), CONV_W ** -0.5),
        'b_gate': jnp.concatenate([i_bias, f_bias, i_bias_b, f_bias_b], axis=-1),
        'g_mh': 1.0 + nrm(ks[14], (DEPTH, MLSTM_WIDTH), 0.05),
        'w_a': nrm(ks[15], (DEPTH, MLSTM_WIDTH, D_MODEL), MLSTM_WIDTH ** -0.5),
        'w_s': nrm(ks[16], (DEPTH, GMLP_GROUPS, GMLP_CHUNK, GMLP_CHUNK), GMLP_CHUNK ** -0.5),
        'b_s': 1.0 + nrm(ks[17], (DEPTH, GMLP_GROUPS, GMLP_CHUNK), 0.05),
        'g_sgu': 1.0 + nrm(ks[18], (DEPTH, GMLP_WIDTH), 0.05),
        'w_b': nrm(ks[19], (DEPTH, GMLP_WIDTH, D_MODEL), GMLP_WIDTH ** -0.5),
        'w_out': nrm(ks[20], (DEPTH, D_MODEL, D_MODEL), D_MODEL ** -0.5),
        'w1': nrm(ks[21], (DEPTH, D_MODEL, D_FF), D_MODEL ** -0.5),
        'w2': nrm(ks[22], (DEPTH, D_FF, D_MODEL), D_FF ** -0.5),
        'norm_f': 1.0 + nrm(ks[23], (D_MODEL,), 0.05),
    }


def reference(x, c, ctx, c_ctx, norm1, norm2, w_mod, b_mod, w_in, conv_qk, b_gate, g_mh,
              w_a, w_s, b_s, g_sgu, w_b, w_out, w1, w2, norm_f):
    B, S, D = x.shape
    rows = S // GRID_W
    n_lat_chunks = rows // ROWS_PER_CHUNK
    n_ctx_chunks = ctx.shape[1] // GMLP_CHUNK
    s_c = jax.nn.silu(c)
    s_cc = jax.nn.silu(c_ctx)
    for l in range(DEPTH):
        last = l == DEPTH - 1
        mod_x = s_c @ w_mod[l] + b_mod[l]
        sh1, sc1, g1, sh2, sc2, g2 = jnp.split(mod_x[:, None, :], 6, axis=-1)
        n_mod_c = 2 if last else 6
        mod_c = s_cc @ w_mod[l][:, :n_mod_c * D] + b_mod[l][:n_mod_c * D]
        mods_c = jnp.split(mod_c, n_mod_c, axis=-1)
        sh1c, sc1c = mods_c[0], mods_c[1]

        xn = modulate(rmsnorm(x, norm1[l]), sh1, sc1)
        cn = modulate(rmsnorm(ctx, norm1[l]), sh1c, sc1c)
        px = xn @ w_in[l]
        pc = cn @ (w_in[l][:, :OFF_O] if last else w_in[l])
        lat_in = mlstm_inputs(px, conv_qk[l], b_gate[l])
        ctx_in = mlstm_inputs(pc, conv_qk[l], b_gate[l])
        hx, hc = bidir_mlstm(lat_in, ctx_in, not last)
        mix_x = merge_branches(px, hx, n_lat_chunks, g_mh[l], w_a[l], w_s[l], b_s[l],
                               g_sgu[l], w_b[l], w_out[l])
        x = x + g1 * mix_x

        xn2 = modulate(rmsnorm(x, norm2[l]), sh2, sc2)
        x = x + g2 * sq_relu_mlp(xn2, w1[l], w2[l])

        if not last:
            g1c, sh2c, sc2c, g2c = mods_c[2], mods_c[3], mods_c[4], mods_c[5]
            mix_c = merge_branches(pc, hc, n_ctx_chunks, g_mh[l], w_a[l], w_s[l], b_s[l],
                                   g_sgu[l], w_b[l], w_out[l])
            ctx = ctx + g1c * mix_c
            cn2 = modulate(rmsnorm(ctx, norm2[l]), sh2c, sc2c)
            ctx = ctx + g2c * sq_relu_mlp(cn2, w1[l], w2[l])
    return rmsnorm(x, norm_f)
```

```python
import functools

import jax
import jax.numpy as jnp
from jax import lax
from jax.experimental import pallas as pl
from jax.experimental.pallas import tpu as pltpu

D_MODEL = 1024
N_HEADS = 4
HEAD_DIM = D_MODEL // N_HEADS
CHUNK = 128
GROUP_DIM = 128
N_GROUPS = D_MODEL // GROUP_DIM
D_FF = 4 * D_MODEL
CONV_W = 3
GATE_SLOTS = 8
NEG = -1e30
EPS = 1e-6

N_MERGE_SEG = 5
N_SEG = 8
SEG_Q, SEG_K, SEG_V = 5, 6, 7

VMEM_LIMIT = 60000 * 1024

F32 = jnp.float32
BF16 = jnp.bfloat16


def _dot(a, b):
    return jnp.dot(a, b, preferred_element_type=F32)


def _dot_nt(a, b):
    return lax.dot_general(a, b, (((1,), (1,)), ((), ())), preferred_element_type=F32)


def _dot_tn(a, b):
    return lax.dot_general(a, b, (((0,), (0,)), ((), ())), preferred_element_type=F32)


def _split_hi_lo(x):
    hi = x.astype(BF16)
    lo = (x - hi.astype(F32)).astype(BF16)
    return hi, lo


def _silu(x):
    return x * jax.nn.sigmoid(x)


def _const_spec(shape):
    nd = len(shape)
    return pl.BlockSpec(shape, lambda *_: (0,) * nd, pipeline_mode=pl.Buffered(1))


def _mod_kernel(c_ref, w_ref, b_ref, o_ref):
    s = _silu(c_ref[...])
    s_hi, s_lo = _split_hi_lo(s)
    w_hi, w_lo = _split_hi_lo(w_ref[...])
    acc = _dot(s_hi, w_hi) + _dot(s_hi, w_lo) + _dot(s_lo, w_hi)
    o_ref[...] = acc + b_ref[...]


def _modulation(cc, w_mod, b_mod):
    rows, d = cc.shape
    n = w_mod.shape[1]
    tn = 1536
    return pl.pallas_call(
        _mod_kernel,
        grid=(n // tn,),
        in_specs=[
            pl.BlockSpec((rows, d), lambda j: (0, 0)),
            pl.BlockSpec((d, tn), lambda j: (0, j)),
            pl.BlockSpec((1, tn), lambda j: (0, j)),
        ],
        out_specs=pl.BlockSpec((rows, tn), lambda j: (0, j)),
        out_shape=jax.ShapeDtypeStruct((rows, n), F32),
        compiler_params=pltpu.CompilerParams(
            dimension_semantics=("arbitrary",), vmem_limit_bytes=VMEM_LIMIT),
        name="modulation",
    )(cc, w_mod, b_mod)


def _log_sigmoid(x):
    return jnp.minimum(x, 0.0) - jnp.log1p(jnp.exp(-jnp.abs(x)))


def _inproj_kernel(x_ref, sh_ref, sc_ref, g_ref, w_ref, wg_ref, wgt_ref, bg_ref, bgt_ref,
                   p_ref, gc_ref, gr_ref, xn_ref):
    j = pl.program_id(2)
    tm = x_ref.shape[0]

    @pl.when(j == 0)
    def _():
        x = x_ref[...]
        ms = jnp.mean(x * x, axis=-1, keepdims=True)
        y = x * lax.rsqrt(ms + EPS) * g_ref[...]
        xn = (y * (1.0 + sc_ref[...]) + sh_ref[...]).astype(BF16)
        xn_ref[...] = xn
        gc = _dot(xn, wg_ref[...]) + bg_ref[...]
        slot_c = lax.broadcasted_iota(jnp.int32, gc.shape, 1) % GATE_SLOTS
        gc = jnp.where((slot_c == 1) | (slot_c == 3), _log_sigmoid(gc), gc)
        gr = _dot_nt(wgt_ref[...], xn) + bgt_ref[...]
        slot_r = lax.broadcasted_iota(jnp.int32, gr.shape, 0) % GATE_SLOTS
        gr = jnp.where((slot_r == 1) | (slot_r == 3), _log_sigmoid(gr), gr)
        for h in range(N_HEADS):
            gc_ref[h] = gc[:, h * GATE_SLOTS:(h + 1) * GATE_SLOTS]
            for cc in range(tm // CHUNK):
                gr_ref[h, cc] = gr[h * GATE_SLOTS:(h + 1) * GATE_SLOTS, cc * CHUNK:(cc + 1) * CHUNK]

    p_ref[...] = _dot(xn_ref[...], w_ref[...]).astype(BF16)


def _inproj(x, mod3, mod_row_of_batch, norm_g, w_main, wg, wgt, bg, bgt, seg0, nseg, tm):
    B, L, D = x.shape
    nc = L // CHUNK
    grid = (B, L // tm, nseg)
    row = mod_row_of_batch
    return pl.pallas_call(
        _inproj_kernel,
        grid=grid,
        in_specs=[
            pl.BlockSpec((None, tm, D), lambda b, i, j: (b, i, 0)),
            pl.BlockSpec((None, 1, D), lambda b, i, j: (row(b), 0, 0)),
            pl.BlockSpec((None, 1, D), lambda b, i, j: (row(b), 0, 1)),
            pl.BlockSpec((1, D), lambda b, i, j: (0, 0)),
            pl.BlockSpec((D, D), lambda b, i, j: (0, seg0 + j)),
            pl.BlockSpec(wg.shape, lambda b, i, j: (0, 0)),
            pl.BlockSpec(wgt.shape, lambda b, i, j: (0, 0)),
            pl.BlockSpec(bg.shape, lambda b, i, j: (0, 0)),
            pl.BlockSpec(bgt.shape, lambda b, i, j: (0, 0)),
        ],
        out_specs=[
            pl.BlockSpec((None, tm, D), lambda b, i, j: (b, i, j)),
            pl.BlockSpec((None, N_HEADS, tm, GATE_SLOTS), lambda b, i, j: (b, 0, i, 0)),
            pl.BlockSpec((None, N_HEADS, tm // CHUNK, GATE_SLOTS, CHUNK), lambda b, i, j: (b, 0, i, 0, 0)),
        ],
        out_shape=[
            jax.ShapeDtypeStruct((B, L, nseg * D), BF16),
            jax.ShapeDtypeStruct((B, N_HEADS, L, GATE_SLOTS), F32),
            jax.ShapeDtypeStruct((B, N_HEADS, nc, GATE_SLOTS, CHUNK), F32),
        ],
        scratch_shapes=[pltpu.VMEM((tm, D), BF16)],
        compiler_params=pltpu.CompilerParams(
            dimension_semantics=("parallel", "parallel", "arbitrary"), vmem_limit_bytes=VMEM_LIMIT),
        name="inproj",
    )(x, mod3, mod3, norm_g, w_main, wg, wgt, bg, bgt)


def _conv_silu_chunk(src_ref, w_ref, c, n_chunks, scale):
    L = n_chunks * CHUNK
    r0 = pl.multiple_of(c * CHUNK, CHUNK)
    x = src_ref[pl.ds(r0, CHUNK), :].astype(F32)
    halo = 16
    p0 = pl.multiple_of(jnp.maximum(r0 - halo, 0), halo)
    n0 = pl.multiple_of(jnp.minimum(r0 + CHUNK, L - halo), halo)
    prev_row = src_ref[pl.ds(p0, halo), :].astype(F32)[halo - 1:halo, :]
    next_row = src_ref[pl.ds(n0, halo), :].astype(F32)[0:1, :]
    prev_row = jnp.where(c > 0, prev_row, 0.0)
    next_row = jnp.where(c < n_chunks - 1, next_row, 0.0)
    rows = lax.broadcasted_iota(jnp.int32, x.shape, 0)
    xm1 = jnp.where(rows == 0, prev_row, pltpu.roll(x, 1, 0))
    xp1 = jnp.where(rows == CHUNK - 1, next_row, pltpu.roll(x, CHUNK - 1, 0))
    w = w_ref[...]
    y = xm1 * w[0:1, :] + x * w[1:2, :] + xp1 * w[2:3, :]
    y = _silu(y)
    if scale != 1.0:
        y = y * scale
    return y.astype(BF16)


def _mlstm_chunk(q, k, v, gc, gr, tri_c, tri_r, mask, slot_i, c_ref, n_ref, m_ref, want_h):
    slot_f = slot_i + 1
    gc_hi, gc_lo = _split_hi_lo(gc)
    gr_hi, gr_lo = _split_hi_lo(gr)
    bc = _dot(tri_c, gc_hi) + _dot(tri_c, gc_lo)
    br = _dot(gr_hi, tri_r) + _dot(gr_lo, tri_r)
    b_col = bc[:, slot_f:slot_f + 1]
    b_row = br[slot_f:slot_f + 1, :]
    li_col = gc[:, slot_i:slot_i + 1]
    li_row = gr[slot_i:slot_i + 1, :]
    lf_row = gr[slot_f:slot_f + 1, :]
    b_last = jnp.sum(lf_row, axis=1, keepdims=True)
    m_prev = m_ref[...]
    c_prev = c_ref[...]
    n_prev = n_ref[...]

    w_end_row = b_last - b_row + li_row
    w_end_col = b_last - b_col + li_col
    m_new = jnp.maximum(b_last + m_prev, jnp.max(w_end_row, axis=1, keepdims=True))
    e_col = jnp.exp(w_end_col - m_new)
    s_prev = jnp.exp(b_last + m_prev - m_new)
    ke = k.astype(F32) * e_col
    c_ref[...] = s_prev * c_prev + _dot_tn(ke.astype(BF16), v)
    n_ref[...] = s_prev * n_prev + jnp.sum(ke, axis=0, keepdims=True)
    m_ref[...] = m_new
    if not want_h:
        return None

    d_intra = jnp.where(mask, b_col - b_row + li_row, NEG)
    inter = b_col + m_prev
    m_i = jnp.maximum(inter, jnp.max(d_intra, axis=1, keepdims=True))
    w = jnp.exp(d_intra - m_i)
    s_inter = jnp.exp(inter - m_i)
    a = w * _dot_nt(q, k)
    num = _dot(a.astype(BF16), v) + s_inter * _dot(q, c_prev.astype(BF16))
    den = jnp.sum(a, axis=1, keepdims=True) + s_inter * jnp.sum(q.astype(F32) * n_prev, axis=1, keepdims=True)
    return num / jnp.maximum(jnp.abs(den), jnp.exp(-m_i))


def _mlstm_kernel(q_ref, k_ref, v_ref, gc_ref, gr_ref, kx_ref, vx_ref, gcx_ref, grx_ref,
                  wq_ref, wk_ref, o_ref, qs_ref, ks_ref, kxs_ref, hacc_ref, c_ref, n_ref, m_ref):
    nc = q_ref.shape[0] // CHUNK
    ncx = kx_ref.shape[0] // CHUNK

    def conv_body(c, carry):
        r0 = pl.multiple_of(c * CHUNK, CHUNK)
        qs_ref[pl.ds(r0, CHUNK), :] = _conv_silu_chunk(q_ref, wq_ref, c, nc, 1.0)
        ks_ref[pl.ds(r0, CHUNK), :] = _conv_silu_chunk(k_ref, wk_ref, c, nc, HEAD_DIM ** -0.5)
        return carry

    lax.fori_loop(0, nc, conv_body, 0)
    for c in range(ncx):
        kxs_ref[c * CHUNK:(c + 1) * CHUNK, :] = _conv_silu_chunk(kx_ref, wk_ref, c, ncx, HEAD_DIM ** -0.5)

    ii = lax.broadcasted_iota(jnp.int32, (CHUNK, CHUNK), 0)
    jj = lax.broadcasted_iota(jnp.int32, (CHUNK, CHUNK), 1)
    lower = ii >= jj
    upper = ii <= jj
    tril = jnp.where(lower, 1.0, 0.0).astype(BF16)
    triu = jnp.where(upper, 1.0, 0.0).astype(BF16)

    def run_direction(backward):
        mask = upper if backward else lower
        tri_c = triu if backward else tril
        tri_r = tril if backward else triu
        slot_i = 2 if backward else 0
        c_ref[...] = jnp.zeros_like(c_ref)
        n_ref[...] = jnp.zeros_like(n_ref)
        m_ref[...] = jnp.zeros_like(m_ref)
        order = range(ncx - 1, -1, -1) if backward else range(ncx)
        for c in order:
            rows = slice(c * CHUNK, (c + 1) * CHUNK)
            _mlstm_chunk(None, kxs_ref[rows, :], vx_ref[rows, :], gcx_ref[rows, :], grx_ref[c],
                         tri_c, tri_r, mask, slot_i, c_ref, n_ref, m_ref, False)

        def body(t, carry):
            c = (nc - 1 - t) if backward else t
            r0 = pl.multiple_of(c * CHUNK, CHUNK)
            rows = pl.ds(r0, CHUNK)
            h = _mlstm_chunk(qs_ref[rows, :], ks_ref[rows, :], v_ref[rows, :], gc_ref[rows, :], gr_ref[c],
                             tri_c, tri_r, mask, slot_i, c_ref, n_ref, m_ref, True)
            if backward:
                hacc_ref[rows, :] = h
            else:
                o_ref[rows, :] = (hacc_ref[rows, :] + h).astype(o_ref.dtype)
            return carry

        lax.fori_loop(0, nc, body, 0)

    run_direction(True)
    run_direction(False)


def _mlstm(p, gc, gr, pc, gcx, grx, conv_q, conv_k):
    B, L, _ = p.shape
    Lx = pc.shape[1]
    nc, ncx = L // CHUNK, Lx // CHUNK
    per_seg = D_MODEL // HEAD_DIM
    seq_spec = lambda seg: pl.BlockSpec((None, L, HEAD_DIM), lambda b, h: (b, 0, seg * per_seg + h))
    ctx_spec = lambda seg: pl.BlockSpec((None, Lx, HEAD_DIM), lambda b, h: (b, 0, seg * per_seg + h))
    return pl.pallas_call(
        _mlstm_kernel,
        grid=(B, N_HEADS),
        in_specs=[
            seq_spec(SEG_Q), seq_spec(SEG_K), seq_spec(SEG_V),
            pl.BlockSpec((None, None, L, GATE_SLOTS), lambda b, h: (b, h, 0, 0)),
            pl.BlockSpec((None, None, nc, GATE_SLOTS, CHUNK), lambda b, h: (b, h, 0, 0, 0)),
            ctx_spec(0), ctx_spec(1),
            pl.BlockSpec((None, None, Lx, GATE_SLOTS), lambda b, h: (b, h, 0, 0)),
            pl.BlockSpec((None, None, ncx, GATE_SLOTS, CHUNK), lambda b, h: (b, h, 0, 0, 0)),
            pl.BlockSpec((CONV_W, HEAD_DIM), lambda b, h: (0, h)),
            pl.BlockSpec((CONV_W, HEAD_DIM), lambda b, h: (0, h)),
        ],
        out_specs=pl.BlockSpec((None, L, HEAD_DIM), lambda b, h: (b, 0, h)),
        out_shape=jax.ShapeDtypeStruct((B, L, D_MODEL), BF16),
        scratch_shapes=[
            pltpu.VMEM((L, HEAD_DIM), BF16),
            pltpu.VMEM((L, HEAD_DIM), BF16),
            pltpu.VMEM((Lx, HEAD_DIM), BF16),
            pltpu.VMEM((L, HEAD_DIM), F32),
            pltpu.VMEM((HEAD_DIM, HEAD_DIM), F32),
            pltpu.VMEM((1, HEAD_DIM), F32),
            pltpu.VMEM((1, 1), F32),
        ],
        compiler_params=pltpu.CompilerParams(
            dimension_semantics=("parallel", "parallel"), vmem_limit_bytes=VMEM_LIMIT),
        name="mlstm",
    )(p, p, p, gc, gr, pc, pc, gcx, grx, conv_q, conv_k)


def _rms(x, g):
    ms = jnp.mean(x * x, axis=-1, keepdims=True)
    return x * lax.rsqrt(ms + EPS) * g


def _merge_mlp_kernel(x_ref, p_ref, h_ref, g1_ref, sh2_ref, sc2_ref, g2_ref,
                      n2_ref, nf_ref, gmh_ref, gsgu_ref, wa_ref, wb_ref, wo_ref, ws_ref, bst_ref,
                      w1_ref, w2_ref, o_ref, t_ref):
    tm = x_ref.shape[0]
    D = D_MODEL
    seg = lambda s: p_ref[:, s * D:(s + 1) * D].astype(F32)

    hm = jax.nn.sigmoid(seg(0)) * h_ref[...].astype(F32)
    gmh = gmh_ref[...]
    parts = []
    for h in range(N_HEADS):
        cols = slice(h * HEAD_DIM, (h + 1) * HEAD_DIM)
        parts.append(_rms(hm[:, cols], gmh[:, cols]).astype(BF16))
    ya = _dot(jnp.concatenate(parts, axis=1), wa_ref[...])

    u = jax.nn.gelu(seg(1))
    vg = jax.nn.gelu(seg(2))
    vn = _rms(vg, gsgu_ref[...]).astype(BF16)
    bst = bst_ref[...]
    for cc in range(tm // CHUNK):
        rows = slice(cc * CHUNK, (cc + 1) * CHUNK)
        for g in range(N_GROUPS):
            cols = slice(g * GROUP_DIM, (g + 1) * GROUP_DIM)
            s = _dot(ws_ref[g], vn[rows, cols]) + bst[:, g:g + 1]
            t_ref[rows, cols] = (u[rows, cols] * s).astype(BF16)
    yb = _dot(t_ref[...], wb_ref[...])

    y = jax.nn.sigmoid(seg(3)) * ya + jax.nn.sigmoid(seg(4)) * yb
    mix = _dot(y.astype(BF16), wo_ref[...])
    x1 = x_ref[...] + g1_ref[...] * mix

    xn2 = (_rms(x1, n2_ref[...]) * (1.0 + sc2_ref[...]) + sh2_ref[...]).astype(BF16)
    ff = D_FF // 4
    acc = jnp.zeros((tm, D), F32)
    for kk in range(4):
        hmid = jnp.maximum(_dot(xn2, w1_ref[:, kk * ff:(kk + 1) * ff]), 0.0)
        acc = acc + _dot((hmid * hmid).astype(BF16), w2_ref[kk * ff:(kk + 1) * ff, :])
    x2 = x1 + g2_ref[...] * acc
    o_ref[...] = _rms(x2, nf_ref[...])


def _merge_mlp(x, p, hm, mod3, norm2, norm_f, g_mh, g_sgu, w_a, w_b, w_out, w_s, b_st, w1, w2, tm):
    B, L, D = x.shape
    mod_spec = lambda k: pl.BlockSpec((None, 1, D), lambda b, i: (b, 0, k))
    return pl.pallas_call(
        _merge_mlp_kernel,
        grid=(B, L // tm),
        in_specs=[
            pl.BlockSpec((None, tm, D), lambda b, i: (b, i, 0)),
            pl.BlockSpec((None, tm, N_MERGE_SEG * D), lambda b, i: (b, i, 0)),
            pl.BlockSpec((None, tm, D), lambda b, i: (b, i, 0)),
            mod_spec(2), mod_spec(3), mod_spec(4), mod_spec(5),
            _const_spec((1, D)), _const_spec((1, D)), _const_spec((1, D)), _const_spec((1, D)),
            _const_spec((D, D)), _const_spec((D, D)), _const_spec((D, D)),
            _const_spec(w_s.shape), _const_spec(b_st.shape),
            _const_spec((D, D_FF)), _const_spec((D_FF, D)),
        ],
        out_specs=pl.BlockSpec((None, tm, D), lambda b, i: (b, i, 0)),
        out_shape=jax.ShapeDtypeStruct((B, L, D), F32),
        scratch_shapes=[pltpu.VMEM((tm, D), BF16)],
        compiler_params=pltpu.CompilerParams(
            dimension_semantics=("parallel", "parallel"), vmem_limit_bytes=VMEM_LIMIT),
        name="merge_mlp",
    )(x, p, hm, mod3, mod3, mod3, mod3, norm2, norm_f, g_mh, g_sgu, w_a, w_b, w_out, w_s, b_st, w1, w2)


def kernel(x, c, ctx, c_ctx, norm1, norm2, w_mod, b_mod, w_in, conv_qk, b_gate, g_mh, w_a, w_s, b_s,
           g_sgu, w_b, w_out, w1, w2, norm_f):
    B, S, D = x.shape
    assert D == D_MODEL and S % CHUNK == 0 and ctx.shape[1] % CHUNK == 0
    assert w_mod.shape[0] == 1, "single-layer block"
    W = D_MODEL
    off_g = 3 * W
    off_o = off_g + 4 * N_HEADS

    mod_rows = ((B + 1 + 7) // 8) * 8
    cc = jnp.zeros((mod_rows, D), F32).at[:B].set(c).at[B].set(c_ctx)
    mod = _modulation(cc, w_mod[0], b_mod[0][None, :])
    mod3 = mod.reshape(mod_rows, 1, 6 * D)

    wi = w_in[0]
    w_main = jnp.concatenate([wi[:, off_o:], wi[:, :off_g]], axis=1).astype(BF16)
    wg4 = wi[:, off_g:off_o].reshape(D, 4, N_HEADS).transpose(0, 2, 1)
    wg = jnp.pad(wg4, ((0, 0), (0, 0), (0, GATE_SLOTS - 4))).reshape(D, N_HEADS * GATE_SLOTS)
    bg4 = b_gate[0].reshape(4, N_HEADS).T
    bg = jnp.pad(bg4, ((0, 0), (0, GATE_SLOTS - 4))).reshape(1, N_HEADS * GATE_SLOTS)
    wg_b = wg.astype(BF16)
    wgt_b = wg.T.astype(BF16)
    bgt = bg.T

    n1 = norm1[0][None, :]
    p, gc, gr = _inproj(x, mod3, lambda b: b, n1, w_main, wg_b, wgt_b, bg, bgt, 0, N_SEG, 512)
    pc, gcx, grx = _inproj(ctx, mod3, lambda b: B, n1, w_main, wg_b, wgt_b, bg, bgt, SEG_K, 2, ctx.shape[1])

    conv = conv_qk[0]
    hm = _mlstm(p, gc, gr, pc, gcx, grx, conv[:, :W], conv[:, W:])

    return _merge_mlp(
        x, p, hm, mod3, norm2[0][None, :], norm_f[None, :], g_mh[0][None, :], g_sgu[0][None, :],
        w_a[0].astype(BF16), w_b[0].astype(BF16), w_out[0].astype(BF16),
        w_s[0].astype(BF16), b_s[0].T, w1[0].astype(BF16), w2[0].astype(BF16), 256)
```

```python
import jax
import jax.numpy as jnp
from jax import lax
from jax.experimental import pallas as pl
from jax.experimental.pallas import tpu as pltpu

D_MODEL = 1024
N_HEADS = 4
HEAD_DIM = D_MODEL // N_HEADS
CHUNK = 128
GROUP_DIM = 128
N_GROUPS = D_MODEL // GROUP_DIM
D_FF = 4 * D_MODEL
CONV_W = 3
NEG = -1e30
EPS = 1e-6
LANES = 128
HALO = 16

N_QUANT = 4
SERIES = N_HEADS * N_QUANT * 2
CHUNKS_PER_TILE = LANES // SERIES
Q_G, Q_E, Q_S, Q_FL = 0, 1, 2, 3

N_MERGE_SEG = 5
N_SEG = 8
SEG_Q, SEG_K, SEG_V = 5, 6, 7

VMEM_LIMIT = 60000 * 1024

F32 = jnp.float32
BF16 = jnp.bfloat16


def _dot(a, b):
    return jnp.dot(a, b, preferred_element_type=F32)


def _dot_nt(a, b):
    return lax.dot_general(a, b, (((1,), (1,)), ((), ())), preferred_element_type=F32)


def _dot_tn(a, b):
    return lax.dot_general(a, b, (((0,), (0,)), ((), ())), preferred_element_type=F32)


def _split_hi_lo(x):
    hi = x.astype(BF16)
    lo = (x - hi.astype(F32)).astype(BF16)
    return hi, lo


def _silu(x):
    return x * jax.nn.sigmoid(x)


def _const_spec(shape):
    nd = len(shape)
    return pl.BlockSpec(shape, lambda *_: (0,) * nd, pipeline_mode=pl.Buffered(1))


def _mod_kernel(c_ref, w_ref, b_ref, o_ref):
    s = _silu(c_ref[...])
    s_hi, s_lo = _split_hi_lo(s)
    w_hi, w_lo = _split_hi_lo(w_ref[...])
    acc = _dot(s_hi, w_hi) + _dot(s_hi, w_lo) + _dot(s_lo, w_hi)
    o_ref[...] = acc + b_ref[...]


def _modulation(cc, w_mod, b_mod):
    rows, d = cc.shape
    n = w_mod.shape[1]
    tn = 1536
    return pl.pallas_call(
        _mod_kernel,
        grid=(n // tn,),
        in_specs=[
            pl.BlockSpec((rows, d), lambda j: (0, 0)),
            pl.BlockSpec((d, tn), lambda j: (0, j)),
            pl.BlockSpec((1, tn), lambda j: (0, j)),
        ],
        out_specs=pl.BlockSpec((rows, tn), lambda j: (0, j)),
        out_shape=jax.ShapeDtypeStruct((rows, n), F32),
        compiler_params=pltpu.CompilerParams(
            dimension_semantics=("arbitrary",), vmem_limit_bytes=VMEM_LIMIT),
        name="modulation",
    )(cc, w_mod, b_mod)


def _log_sigmoid(x):
    return jnp.minimum(x, 0.0) - jnp.log1p(jnp.exp(-jnp.abs(x)))


def _make_inproj_kernel(tm, n_tiles, seg_q, seg_k):
    n_cc = tm // CHUNK

    def kernel(x_ref, xp_ref, xn_ref_in, sh_ref, sc_ref, g_ref, w_ref, wg_ref, bg_ref, cw_ref,
               p_ref, gf_ref, gi_ref, xe_ref, pe_ref):
        i = pl.program_id(1)
        j = pl.program_id(2)

        @pl.when(j == 0)
        def _():
            def normed(x):
                ms = jnp.mean(x * x, axis=-1, keepdims=True)
                y = x * lax.rsqrt(ms + EPS) * g_ref[...]
                return y * (1.0 + sc_ref[...]) + sh_ref[...]

            xn = normed(x_ref[...]).astype(BF16)
            xe_ref[HALO:HALO + tm, :] = xn
            xe_ref[0:HALO, :] = jnp.where(i > 0, normed(xp_ref[...]), 0.0).astype(BF16)
            xe_ref[HALO + tm:, :] = jnp.where(i < n_tiles - 1, normed(xn_ref_in[...]), 0.0).astype(BF16)
            acc = _dot(xn[0:CHUNK], wg_ref[0])
            for cc in range(1, n_cc):
                acc = acc + _dot(xn[cc * CHUNK:(cc + 1) * CHUNK], wg_ref[cc])
            acc = acc + bg_ref[...]
            gf_ref[...] = _log_sigmoid(acc[:, :LANES])
            gi_ref[...] = acc[:, LANES:]

        conv_js = [s for s in (seg_q, seg_k) if s is not None]
        is_conv = functools_reduce_or([j == s for s in conv_js])

        @pl.when(is_conv)
        def _():
            pe_ref[...] = _dot(xe_ref[...], w_ref[...])
            w = cw_ref[...]
            scale = jnp.where(j == seg_k, HEAD_DIM ** -0.5, 1.0) if seg_k is not None else 1.0
            for cc in range(n_cc):
                r = HALO + cc * CHUNK
                y = (pe_ref[pl.ds(r - 1, CHUNK), :] * w[0:1, :] + pe_ref[pl.ds(r, CHUNK), :] * w[1:2, :]
                     + pe_ref[pl.ds(r + 1, CHUNK), :] * w[2:3, :])
                p_ref[cc * CHUNK:(cc + 1) * CHUNK, :] = (_silu(y) * scale).astype(BF16)

        @pl.when(jnp.logical_not(is_conv))
        def _():
            p_ref[...] = _dot(xe_ref[HALO:HALO + tm, :], w_ref[...]).astype(BF16)

    return kernel


def functools_reduce_or(preds):
    out = preds[0]
    for p in preds[1:]:
        out = out | p
    return out


def _inproj(x, mod3, mod_row_of_batch, norm_g, w_main, wgs, bgs, conv_qk, seg0, nseg, seg_q, seg_k, tm):
    B, L, D = x.shape
    n_tiles = L // tm
    hb = tm // HALO
    n_hblk = L // HALO
    row = mod_row_of_batch
    conv_blk = (lambda b, i, j: (0, jnp.where(j == seg_k, 1, 0))) if seg_k is not None else (lambda b, i, j: (0, 0))
    gate_lanes = (tm // CHUNK) * SERIES
    return pl.pallas_call(
        _make_inproj_kernel(tm, n_tiles, seg_q, seg_k),
        grid=(B, n_tiles, nseg),
        in_specs=[
            pl.BlockSpec((None, tm, D), lambda b, i, j: (b, i, 0)),
            pl.BlockSpec((None, HALO, D), lambda b, i, j: (b, jnp.maximum(i * hb - 1, 0), 0)),
            pl.BlockSpec((None, HALO, D), lambda b, i, j: (b, jnp.minimum((i + 1) * hb, n_hblk - 1), 0)),
            pl.BlockSpec((None, 1, D), lambda b, i, j: (row(b), 0, 0)),
            pl.BlockSpec((None, 1, D), lambda b, i, j: (row(b), 0, 1)),
            pl.BlockSpec((1, D), lambda b, i, j: (0, 0)),
            pl.BlockSpec((D, D), lambda b, i, j: (0, seg0 + j)),
            pl.BlockSpec(wgs.shape, lambda b, i, j: (0, 0, 0)),
            pl.BlockSpec(bgs.shape, lambda b, i, j: (0, 0)),
            pl.BlockSpec((CONV_W, D), conv_blk),
        ],
        out_specs=[
            pl.BlockSpec((None, tm, D), lambda b, i, j: (b, i, j)),
            pl.BlockSpec((None, CHUNK, LANES), lambda b, i, j: (b, 0, i)),
            pl.BlockSpec((None, CHUNK, LANES), lambda b, i, j: (b, 0, i)),
        ],
        out_shape=[
            jax.ShapeDtypeStruct((B, L, nseg * D), BF16),
            jax.ShapeDtypeStruct((B, CHUNK, n_tiles * LANES), F32),
            jax.ShapeDtypeStruct((B, CHUNK, n_tiles * LANES), F32),
        ],
        scratch_shapes=[pltpu.VMEM((tm + 2 * HALO, D), BF16), pltpu.VMEM((tm + 2 * HALO, D), F32)],
        compiler_params=pltpu.CompilerParams(
            dimension_semantics=("parallel", "parallel", "arbitrary"), vmem_limit_bytes=VMEM_LIMIT),
        name="inproj",
    )(x, x, x, mod3, mod3, norm_g, w_main, wgs, bgs, conv_qk)


def _cummax_rows(x, reverse):
    n = x.shape[0]
    rows = lax.broadcasted_iota(jnp.int32, x.shape, 0)
    k = 1
    while k < n:
        if reverse:
            shifted = jnp.where(rows < n - k, pltpu.roll(x, n - k, 0), NEG)
        else:
            shifted = jnp.where(rows >= k, pltpu.roll(x, k, 0), NEG)
        x = jnp.maximum(x, shifted)
        k *= 2
    return x


def _gate_prep(gf, gi, n_lat, n_ctx, cols_ref, rt_ref):
    n_ext = n_lat + n_ctx
    width = gf.shape[1]
    n_tile = width // LANES
    ii = lax.broadcasted_iota(jnp.int32, (CHUNK, CHUNK), 0)
    jj = lax.broadcasted_iota(jnp.int32, (CHUNK, CHUNK), 1)
    tril = jnp.where(ii >= jj, 1.0, 0.0).astype(BF16)
    triu = jnp.where(ii <= jj, 1.0, 0.0).astype(BF16)
    lane = lax.broadcasted_iota(jnp.int32, (1, width), 1)
    lane_bwd = (lane % 2) == 1
    lane_q = (lane // 2) % N_QUANT

    hi, lo = _split_hi_lo(gf)
    b = jnp.where(lane_bwd, _dot(triu, hi) + _dot(triu, lo), _dot(tril, hi) + _dot(tril, lo))
    tot = jnp.where(lane_bwd, b[0:1, :], b[CHUNK - 1:CHUNK, :])
    r = gi - b
    cm = jnp.where(lane_bwd, _cummax_rows(r, True), _cummax_rows(r, False))
    rmax = jnp.where(lane_bwd, cm[0:1, :], cm[CHUNK - 1:CHUNK, :])

    def to_rows(v):
        return [jnp.broadcast_to(v[:, t * LANES:(t + 1) * LANES], (CHUNK, LANES)).T for t in range(n_tile)]

    tot_t, rmax_t = to_rows(tot), to_rows(rmax)

    def slab(tiles, e):
        t, k = divmod(e, CHUNKS_PER_TILE)
        return tiles[t][k * SERIES:(k + 1) * SERIES, :]

    ctx_ids = list(range(n_lat, n_ext))
    orders = (ctx_ids + list(range(n_lat)), ctx_ids[::-1] + list(range(n_lat - 1, -1, -1)))
    m_prev, m_new = [{}, {}], [{}, {}]
    for d, order in enumerate(orders):
        m = jnp.zeros((SERIES, LANES), F32)
        for e in order:
            m_prev[d][e] = m
            tot_e = slab(tot_t, e)
            m = jnp.maximum(tot_e + m, tot_e + slab(rmax_t, e))
            m_new[d][e] = m
    row_bwd = (lax.broadcasted_iota(jnp.int32, (SERIES, LANES), 0) % 2) == 1
    mp_slabs = [jnp.where(row_bwd, m_prev[1][e], m_prev[0][e]) for e in range(n_ext)]
    mn_slabs = [jnp.where(row_bwd, m_new[1][e], m_new[0][e]) for e in range(n_ext)]
    sp_slabs = [jnp.exp(slab(tot_t, e) + mp_slabs[e] - mn_slabs[e]) for e in range(n_ext)]

    def tiles_of(slabs):
        pad = [jnp.zeros((SERIES, LANES), F32)] * (n_tile * CHUNKS_PER_TILE - n_ext)
        full = slabs + pad
        return [jnp.concatenate(full[t * CHUNKS_PER_TILE:(t + 1) * CHUNKS_PER_TILE], axis=0) for t in range(n_tile)]

    def to_lanes(slabs):
        return jnp.concatenate([t.T for t in tiles_of(slabs)], axis=1)

    mp = to_lanes(mp_slabs)
    mn = to_lanes(mn_slabs)

    g = jnp.maximum(mp, cm)
    e_w = jnp.exp(tot + r - mn)
    s_inter = jnp.exp(mp - g)
    floor = jnp.exp(-(b + g))
    packed = jnp.where(lane_q == Q_G, g, jnp.where(lane_q == Q_E, e_w, jnp.where(lane_q == Q_S, s_inter, floor)))

    row_q = (lax.broadcasted_iota(jnp.int32, (CHUNK, LANES), 0) // 2) % N_QUANT
    sp_tiles = tiles_of(sp_slabs)
    for t in range(n_tile):
        r_t = r[:, t * LANES:(t + 1) * LANES].T
        rt_ref[t * CHUNK:(t + 1) * CHUNK, :] = jnp.where(row_q == 0, r_t, sp_tiles[t])
    for e in range(n_ext):
        t, k = divmod(e, CHUNKS_PER_TILE)
        tile = packed[:, t * LANES:(t + 1) * LANES]
        for h in range(N_HEADS):
            shift = k * SERIES + h * 2 * N_QUANT
            cols_ref[h, e] = tile if shift == 0 else pltpu.roll(tile, LANES - shift, 1)


def _mlstm_kernel(q_ref, k_ref, v_ref, kx_ref, vx_ref, gf_ref, gi_ref, gfx_ref, gix_ref,
                  o_ref, cols_ref, rt_ref, c_ref, n_ref, cprev_ref, nprev_ref):
    n_lat = q_ref.shape[0] // CHUNK
    n_ctx = kx_ref.shape[0] // CHUNK
    h = pl.program_id(1)

    @pl.when(h == 0)
    def _():
        gf = jnp.concatenate([gf_ref[...], gfx_ref[...]], axis=1)
        gi = jnp.concatenate([gi_ref[...], gix_ref[...]], axis=1)
        _gate_prep(gf, gi, n_lat, n_ctx, cols_ref, rt_ref)
        nprev_ref[...] = jnp.zeros_like(nprev_ref)

    def chunk_scalars(e):
        cols = cols_ref[h, e]
        r0 = pl.multiple_of(e * SERIES + h * (2 * N_QUANT), 2 * N_QUANT)
        return cols, rt_ref[pl.ds(r0, 2 * N_QUANT), :]

    def col(cols, quantity, d):
        lane = 2 * quantity + d
        return cols[:, lane:lane + 1]

    def state_step(d, e, k, v, c_lat):
        cols, rt = chunk_scalars(e)
        s_row = rt[2 + d:3 + d, :]
        s2 = jnp.concatenate([s_row, s_row], axis=1)
        ke = k.astype(F32) * col(cols, Q_E, d)
        c_old = c_ref[...]
        n_old = n_ref[...]
        if c_lat is not None:
            cprev_ref[d, c_lat] = c_old.astype(BF16)
            nprev_ref[c_lat, d:d + 1, :] = n_old
        c_ref[...] = s2 * c_old + _dot_tn(ke.astype(BF16), v)
        n_ref[...] = s2 * n_old + jnp.sum(ke, axis=0, keepdims=True)

    for d in range(2):
        c_ref[...] = jnp.zeros_like(c_ref)
        n_ref[...] = jnp.zeros_like(n_ref)
        for cx in (range(n_ctx - 1, -1, -1) if d else range(n_ctx)):
            rows = slice(cx * CHUNK, (cx + 1) * CHUNK)
            state_step(d, n_lat + cx, kx_ref[rows, :], vx_ref[rows, :], None)

        def body(t, carry, d=d):
            c = (n_lat - 1 - t) if d else t
            rows = pl.ds(pl.multiple_of(c * CHUNK, CHUNK), CHUNK)
            state_step(d, c, k_ref[rows, :], v_ref[rows, :], c)
            return carry

        lax.fori_loop(0, n_lat, body, 0, unroll=4)

    ii = lax.broadcasted_iota(jnp.int32, (CHUNK, CHUNK), 0)
    jj = lax.broadcasted_iota(jnp.int32, (CHUNK, CHUNK), 1)
    visible = (ii >= jj, ii <= jj)

    def out_body(c, carry):
        rows = pl.ds(pl.multiple_of(c * CHUNK, CHUNK), CHUNK)
        q, k, v = q_ref[rows, :], k_ref[rows, :], v_ref[rows, :]
        cols, rt = chunk_scalars(c)
        s_qk = _dot_nt(q, k)
        a = []
        for d in range(2):
            w = jnp.exp(jnp.where(visible[d], rt[d:d + 1, :] - col(cols, Q_G, d), NEG))
            a.append(w * s_qk)
        av = _dot(jnp.concatenate(a, axis=0).astype(BF16), v)
        n_hi, n_lo = _split_hi_lo(nprev_ref[c])
        qn = _dot_nt(q, n_hi) + _dot_nt(q, n_lo)
        out = None
        for d in range(2):
            s_inter = col(cols, Q_S, d)
            num = av[d * CHUNK:(d + 1) * CHUNK, :] + s_inter * _dot(q, cprev_ref[d, c])
            den = jnp.sum(a[d], axis=1, keepdims=True) + s_inter * qn[:, d:d + 1]
            hd = num * (1.0 / jnp.maximum(jnp.abs(den), col(cols, Q_FL, d)))
            out = hd if out is None else out + hd
        o_ref[rows, :] = out.astype(o_ref.dtype)
        return carry

    lax.fori_loop(0, n_lat, out_body, 0, unroll=2)


def _mlstm(p, gf, gi, pc, gfx, gix):
    B, L, _ = p.shape
    Lx = pc.shape[1]
    n_lat, n_ctx = L // CHUNK, Lx // CHUNK
    n_ext = n_lat + n_ctx
    n_tile = gf.shape[2] // LANES + gfx.shape[2] // LANES
    per_seg = D_MODEL // HEAD_DIM
    seq_spec = lambda seg: pl.BlockSpec((None, L, HEAD_DIM), lambda b, h: (b, 0, seg * per_seg + h))
    ctx_spec = lambda seg: pl.BlockSpec((None, Lx, HEAD_DIM), lambda b, h: (b, 0, seg * per_seg + h))
    gate_spec = lambda a: pl.BlockSpec((None,) + a.shape[1:], lambda b, h: (b, 0, 0))
    return pl.pallas_call(
        _mlstm_kernel,
        grid=(B, N_HEADS),
        in_specs=[
            seq_spec(SEG_Q), seq_spec(SEG_K), seq_spec(SEG_V), ctx_spec(0), ctx_spec(1),
            gate_spec(gf), gate_spec(gi), gate_spec(gfx), gate_spec(gix),
        ],
        out_specs=pl.BlockSpec((None, L, HEAD_DIM), lambda b, h: (b, 0, h)),
        out_shape=jax.ShapeDtypeStruct((B, L, D_MODEL), BF16),
        scratch_shapes=[
            pltpu.VMEM((N_HEADS, n_ext, CHUNK, LANES), F32),
            pltpu.VMEM((n_tile * CHUNK, CHUNK), F32),
            pltpu.VMEM((HEAD_DIM, HEAD_DIM), F32),
            pltpu.VMEM((1, HEAD_DIM), F32),
            pltpu.VMEM((2, n_lat, HEAD_DIM, HEAD_DIM), BF16),
            pltpu.VMEM((n_lat, 2 * N_QUANT, HEAD_DIM), F32),
        ],
        compiler_params=pltpu.CompilerParams(
            dimension_semantics=("parallel", "arbitrary"), vmem_limit_bytes=VMEM_LIMIT),
        name="mlstm",
    )(p, p, p, pc, pc, gf, gi, gfx, gix)


def _rms(x, g):
    ms = jnp.mean(x * x, axis=-1, keepdims=True)
    return x * lax.rsqrt(ms + EPS) * g


def _merge_mlp_kernel(x_ref, p_ref, h_ref, g1_ref, sh2_ref, sc2_ref, g2_ref,
                      n2_ref, nf_ref, gmh_ref, gsgu_ref, wa_ref, wb_ref, wo_ref, ws_ref, bst_ref,
                      w1_ref, w2_ref, o_ref, t_ref):
    tm = x_ref.shape[0]
    D = D_MODEL
    seg = lambda s: p_ref[:, s * D:(s + 1) * D].astype(F32)

    hm = jax.nn.sigmoid(seg(0)) * h_ref[...].astype(F32)
    gmh = gmh_ref[...]
    parts = []
    for h in range(N_HEADS):
        cols = slice(h * HEAD_DIM, (h + 1) * HEAD_DIM)
        parts.append(_rms(hm[:, cols], gmh[:, cols]).astype(BF16))
    ya = _dot(jnp.concatenate(parts, axis=1), wa_ref[...])

    u = jax.nn.gelu(seg(1))
    vg = jax.nn.gelu(seg(2))
    vn = _rms(vg, gsgu_ref[...]).astype(BF16)
    bst = bst_ref[...]
    for cc in range(tm // CHUNK):
        rows = slice(cc * CHUNK, (cc + 1) * CHUNK)
        for g in range(N_GROUPS):
            cols = slice(g * GROUP_DIM, (g + 1) * GROUP_DIM)
            s = _dot(ws_ref[g], vn[rows, cols]) + bst[:, g:g + 1]
            t_ref[rows, cols] = (u[rows, cols] * s).astype(BF16)
    yb = _dot(t_ref[...], wb_ref[...])

    y = jax.nn.sigmoid(seg(3)) * ya + jax.nn.sigmoid(seg(4)) * yb
    mix = _dot(y.astype(BF16), wo_ref[...])
    x1 = x_ref[...] + g1_ref[...] * mix

    xn2 = (_rms(x1, n2_ref[...]) * (1.0 + sc2_ref[...]) + sh2_ref[...]).astype(BF16)
    ff = D_FF // 4
    acc = jnp.zeros((tm, D), F32)
    for kk in range(4):
        hmid = jnp.maximum(_dot(xn2, w1_ref[:, kk * ff:(kk + 1) * ff]), 0.0)
        acc = acc + _dot((hmid * hmid).astype(BF16), w2_ref[kk * ff:(kk + 1) * ff, :])
    x2 = x1 + g2_ref[...] * acc
    o_ref[...] = _rms(x2, nf_ref[...])


def _merge_mlp(x, p, hm, mod3, norm2, norm_f, g_mh, g_sgu, w_a, w_b, w_out, w_s, b_st, w1, w2, tm):
    B, L, D = x.shape
    mod_spec = lambda k: pl.BlockSpec((None, 1, D), lambda b, i: (b, 0, k))
    return pl.pallas_call(
        _merge_mlp_kernel,
        grid=(B, L // tm),
        in_specs=[
            pl.BlockSpec((None, tm, D), lambda b, i: (b, i, 0)),
            pl.BlockSpec((None, tm, N_MERGE_SEG * D), lambda b, i: (b, i, 0)),
            pl.BlockSpec((None, tm, D), lambda b, i: (b, i, 0)),
            mod_spec(2), mod_spec(3), mod_spec(4), mod_spec(5),
            _const_spec((1, D)), _const_spec((1, D)), _const_spec((1, D)), _const_spec((1, D)),
            _const_spec((D, D)), _const_spec((D, D)), _const_spec((D, D)),
            _const_spec(w_s.shape), _const_spec(b_st.shape),
            _const_spec((D, D_FF)), _const_spec((D_FF, D)),
        ],
        out_specs=pl.BlockSpec((None, tm, D), lambda b, i: (b, i, 0)),
        out_shape=jax.ShapeDtypeStruct((B, L, D), F32),
        scratch_shapes=[pltpu.VMEM((tm, D), BF16)],
        compiler_params=pltpu.CompilerParams(
            dimension_semantics=("parallel", "parallel"), vmem_limit_bytes=VMEM_LIMIT),
        name="merge_mlp",
    )(x, p, hm, mod3, mod3, mod3, mod3, norm2, norm_f, g_mh, g_sgu, w_a, w_b, w_out, w_s, b_st, w1, w2)


def _gate_weights(w_gate, b_gate):
    D = w_gate.shape[0]

    def series(a, kind_of_dir):
        a4 = a.reshape(a.shape[:-1] + (4, N_HEADS))
        per_dir = jnp.stack([a4[..., kind_of_dir[0], :], a4[..., kind_of_dir[1], :]], axis=-1)
        dup = jnp.broadcast_to(per_dir[..., :, None, :], per_dir.shape[:-1] + (N_QUANT, 2))
        return dup.reshape(a.shape[:-1] + (SERIES,))

    wf, wi = series(w_gate, (1, 3)), series(w_gate, (0, 2))
    bf, bi = series(b_gate, (1, 3)), series(b_gate, (0, 2))
    eye = jnp.eye(CHUNKS_PER_TILE, dtype=w_gate.dtype)
    place = lambda w: jnp.einsum("ab,dk->adbk", eye, w).reshape(CHUNKS_PER_TILE, D, LANES)
    wgs = jnp.concatenate([place(wf), place(wi)], axis=2).astype(BF16)
    bgs = jnp.concatenate([jnp.tile(bf, CHUNKS_PER_TILE), jnp.tile(bi, CHUNKS_PER_TILE)])[None, :]
    return wgs, bgs


def kernel(x, c, ctx, c_ctx, norm1, norm2, w_mod, b_mod, w_in, conv_qk, b_gate, g_mh, w_a, w_s, b_s,
           g_sgu, w_b, w_out, w1, w2, norm_f):
    B, S, D = x.shape
    Lx = ctx.shape[1]
    assert D == D_MODEL and S % (CHUNK * CHUNKS_PER_TILE) == 0 and Lx % CHUNK == 0
    assert Lx // CHUNK <= CHUNKS_PER_TILE
    assert w_mod.shape[0] == 1, "single-layer block"
    W = D_MODEL
    off_g = 3 * W
    off_o = off_g + 4 * N_HEADS

    mod_rows = ((B + 1 + 7) // 8) * 8
    cc = jnp.zeros((mod_rows, D), F32).at[:B].set(c).at[B].set(c_ctx)
    mod = _modulation(cc, w_mod[0], b_mod[0][None, :])
    mod3 = mod.reshape(mod_rows, 1, 6 * D)

    wi = w_in[0]
    w_main = jnp.concatenate([wi[:, off_o:], wi[:, :off_g]], axis=1).astype(BF16)
    wgs, bgs = _gate_weights(wi[:, off_g:off_o], b_gate[0])

    n1 = norm1[0][None, :]
    conv = conv_qk[0]
    tm = CHUNK * CHUNKS_PER_TILE
    p, gf, gi = _inproj(x, mod3, lambda b: b, n1, w_main, wgs, bgs, conv, 0, N_SEG, SEG_Q, SEG_K, tm)
    pc, gfx, gix = _inproj(ctx, mod3, lambda b: B, n1, w_main, wgs, bgs, conv, SEG_K, 2, None, 0, Lx)

    hm = _mlstm(p, gf, gi, pc, gfx, gix)

    return _merge_mlp(
        x, p, hm, mod3, norm2[0][None, :], norm_f[None, :], g_mh[0][None, :], g_sgu[0][None, :],
        w_a[0].astype(BF16), w_b[0].astype(BF16), w_out[0].astype(BF16),
        w_s[0].astype(BF16), b_s[0].T, w1[0].astype(BF16), w2[0].astype(BF16), 256)
```

```python
import jax
import jax.numpy as jnp
from jax import lax
from jax.experimental import pallas as pl
from jax.experimental.pallas import tpu as pltpu

D_MODEL = 1024
N_HEADS = 4
HEAD_DIM = D_MODEL // N_HEADS
CHUNK = 128
MCHUNK = 256
GROUP_DIM = 128
N_GROUPS = D_MODEL // GROUP_DIM
D_FF = 4 * D_MODEL
CONV_W = 3
NEG = -1e30
EPS = 1e-6
LANES = 128
HALO = 16

N_QUANT = 4
ROWS_PER_HEAD = 2 * N_QUANT
SERIES = N_HEADS * ROWS_PER_HEAD
CHUNKS_PER_TILE = LANES // SERIES
Q_G, Q_E, Q_S, Q_FL = 0, 1, 2, 3

N_MERGE_SEG = 5

VMEM_LIMIT = 60000 * 1024

F32 = jnp.float32
BF16 = jnp.bfloat16


def _dot(a, b):
    return jnp.dot(a, b, preferred_element_type=F32)


def _dot_nt(a, b):
    return lax.dot_general(a, b, (((1,), (1,)), ((), ())), preferred_element_type=F32)


def _split_hi_lo(x):
    hi = x.astype(BF16)
    lo = (x - hi.astype(F32)).astype(BF16)
    return hi, lo


def _silu(x):
    return x * jax.nn.sigmoid(x)


def _const_spec(shape):
    nd = len(shape)
    return pl.BlockSpec(shape, lambda *_: (0,) * nd, pipeline_mode=pl.Buffered(1))


def _mod_kernel(c_ref, w_ref, b_ref, o_ref):
    s = _silu(c_ref[...])
    s_hi, s_lo = _split_hi_lo(s)
    w_hi, w_lo = _split_hi_lo(w_ref[...])
    acc = _dot(s_hi, w_hi) + _dot(s_hi, w_lo) + _dot(s_lo, w_hi)
    o_ref[...] = acc + b_ref[...]


def _modulation(cc, w_mod, b_mod):
    rows, d = cc.shape
    n = w_mod.shape[1]
    tn = 1536
    return pl.pallas_call(
        _mod_kernel,
        grid=(n // tn,),
        in_specs=[
            pl.BlockSpec((rows, d), lambda j: (0, 0)),
            pl.BlockSpec((d, tn), lambda j: (0, j)),
            pl.BlockSpec((1, tn), lambda j: (0, j)),
        ],
        out_specs=pl.BlockSpec((rows, tn), lambda j: (0, j)),
        out_shape=jax.ShapeDtypeStruct((rows, n), F32),
        compiler_params=pltpu.CompilerParams(
            dimension_semantics=("arbitrary",), vmem_limit_bytes=VMEM_LIMIT),
        name="modulation",
    )(cc, w_mod, b_mod)


def _log_sigmoid(x):
    return jnp.minimum(x, 0.0) - jnp.log1p(jnp.exp(-jnp.abs(x)))


def _make_inproj_kernel(tm, n_tiles, latent):
    n_cc = tm // CHUNK
    n_mc = tm // MCHUNK
    tiles_per_gate_tile = CHUNKS_PER_TILE // n_mc

    def kernel(*refs):
        if latent:
            (x_ref, xp_ref, xnx_ref, sh_ref, sc_ref, g_ref, wg_ref, bg_ref, cw_ref, wm_ref, wq_ref, wk_ref, wvt_ref,
             pm_ref, q_ref, k_ref, vt_ref, gf_ref, gi_ref, xe_ref, peq_ref, pek_ref) = refs
        else:
            (x_ref, xp_ref, xnx_ref, sh_ref, sc_ref, g_ref, wg_ref, bg_ref, cw_ref, wk_ref, wvt_ref,
             k_ref, vt_ref, gf_ref, gi_ref, xe_ref, pek_ref) = refs
        i = pl.program_id(1)

        def normed(x):
            ms = jnp.mean(x * x, axis=-1, keepdims=True)
            y = x * lax.rsqrt(ms + EPS) * g_ref[...]
            return y * (1.0 + sc_ref[...]) + sh_ref[...]

        xn = normed(x_ref[...]).astype(BF16)
        xe_ref[HALO:HALO + tm, :] = xn
        xe_ref[0:HALO, :] = jnp.where(i > 0, normed(xp_ref[...]), 0.0).astype(BF16)
        xe_ref[HALO + tm:, :] = jnp.where(i < n_tiles - 1, normed(xnx_ref[...]), 0.0).astype(BF16)

        def conv_silu(pe_ref, w_ref, cw, scale, out_ref):
            pe_ref[...] = _dot(xe_ref[...], w_ref[...])
            for cc in range(n_cc):
                r = HALO + cc * CHUNK
                y = (pe_ref[pl.ds(r - 1, CHUNK), :] * cw[0:1, :] + pe_ref[pl.ds(r, CHUNK), :] * cw[1:2, :]
                     + pe_ref[pl.ds(r + 1, CHUNK), :] * cw[2:3, :])
                y = _silu(y)
                if scale != 1.0:
                    y = y * scale
                out_ref[cc * CHUNK:(cc + 1) * CHUNK, :] = y.astype(BF16)

        cw = cw_ref[...]
        if latent:
            conv_silu(peq_ref, wq_ref, cw[:, :D_MODEL], 1.0, q_ref)
        conv_silu(pek_ref, wk_ref, cw[:, D_MODEL:], HEAD_DIM ** -0.5, k_ref)

        sub = i % tiles_per_gate_tile
        acc = _dot(xn[0:MCHUNK], wg_ref[sub * n_mc])
        for mc in range(1, n_mc):
            acc = acc + _dot(xn[mc * MCHUNK:(mc + 1) * MCHUNK], wg_ref[sub * n_mc + mc])
        acc = acc + bg_ref[...]
        group = lax.broadcasted_iota(jnp.int32, (1, LANES), 1) // SERIES
        own = (group >= sub * n_mc) & (group < (sub + 1) * n_mc)
        new_f = jnp.where(own, _log_sigmoid(acc[:, :LANES]), 0.0)
        new_i = jnp.where(own, acc[:, LANES:], 0.0)

        @pl.when(sub == 0)
        def _():
            gf_ref[...] = new_f
            gi_ref[...] = new_i

        @pl.when(sub != 0)
        def _():
            gf_ref[...] += new_f
            gi_ref[...] += new_i

        if latent:
            for s in range(N_MERGE_SEG):
                cols = slice(s * D_MODEL, (s + 1) * D_MODEL)
                pm_ref[:, cols] = _dot(xn, wm_ref[:, cols]).astype(BF16)
        vt = _dot_nt(wvt_ref[...], xn)
        for mc in range(n_mc):
            vt_ref[mc] = vt[:, mc * MCHUNK:(mc + 1) * MCHUNK].astype(BF16)

    return kernel


def _inproj(x, mod3, mod_row_of_batch, norm_g, wgs, bgs, conv_qk, weights, tm, latent):
    B, L, D = x.shape
    n_tiles = L // tm
    n_mc = tm // MCHUNK
    tiles_per_gate_tile = CHUNKS_PER_TILE // n_mc
    n_gate_tiles = -(-n_tiles // tiles_per_gate_tile)
    hb = tm // HALO
    n_hblk = L // HALO
    row = mod_row_of_batch
    tok = lambda width: pl.BlockSpec((None, tm, width), lambda b, i: (b, i, 0))
    tok_shape = lambda width: jax.ShapeDtypeStruct((B, L, width), BF16)
    widths = [w.shape[1] for w in weights[:-1]]
    gate_spec = pl.BlockSpec((None, MCHUNK, LANES), lambda b, i: (b, 0, i // tiles_per_gate_tile))
    gate_shape = jax.ShapeDtypeStruct((B, MCHUNK, n_gate_tiles * LANES), F32)
    vt_spec = pl.BlockSpec((None, n_mc, D, MCHUNK), lambda b, i: (b, i, 0, 0))
    vt_shape = jax.ShapeDtypeStruct((B, L // MCHUNK, D, MCHUNK), BF16)
    return pl.pallas_call(
        _make_inproj_kernel(tm, n_tiles, latent),
        grid=(B, n_tiles),
        in_specs=[
            pl.BlockSpec((None, tm, D), lambda b, i: (b, i, 0)),
            pl.BlockSpec((None, HALO, D), lambda b, i: (b, jnp.maximum(i * hb - 1, 0), 0)),
            pl.BlockSpec((None, HALO, D), lambda b, i: (b, jnp.minimum((i + 1) * hb, n_hblk - 1), 0)),
            pl.BlockSpec((None, 1, D), lambda b, i: (row(b), 0, 0)),
            pl.BlockSpec((None, 1, D), lambda b, i: (row(b), 0, 1)),
            _const_spec((1, D)), _const_spec(wgs.shape), _const_spec(bgs.shape), _const_spec(conv_qk.shape),
        ] + [_const_spec(w.shape) for w in weights],
        out_specs=[tok(w) for w in widths] + [vt_spec, gate_spec, gate_spec],
        out_shape=[tok_shape(w) for w in widths] + [vt_shape, gate_shape, gate_shape],
        scratch_shapes=[pltpu.VMEM((tm + 2 * HALO, D), BF16)]
        + [pltpu.VMEM((tm + 2 * HALO, D), F32)] * (2 if latent else 1),
        compiler_params=pltpu.CompilerParams(
            dimension_semantics=("parallel", "arbitrary"), vmem_limit_bytes=VMEM_LIMIT),
        name="inproj_latent" if latent else "inproj_context",
    )(x, x, x, mod3, mod3, norm_g, wgs, bgs, conv_qk, *weights)


def _cummax_rows(x, reverse):
    n = x.shape[0]
    rows = lax.broadcasted_iota(jnp.int32, x.shape, 0)
    k = 1
    while k < n:
        if reverse:
            shifted = jnp.where(rows < n - k, pltpu.roll(x, n - k, 0), NEG)
        else:
            shifted = jnp.where(rows >= k, pltpu.roll(x, k, 0), NEG)
        x = jnp.maximum(x, shifted)
        k *= 2
    return x


def _gate_prep(gf, gi, n_lat, n_ctx, cols_ref, rt_ref, sp_ref):
    n_ext = n_lat + n_ctx
    width = gf.shape[1]
    n_tile = width // LANES
    ii = lax.broadcasted_iota(jnp.int32, (MCHUNK, MCHUNK), 0)
    jj = lax.broadcasted_iota(jnp.int32, (MCHUNK, MCHUNK), 1)
    tril = jnp.where(ii >= jj, 1.0, 0.0).astype(BF16)
    triu = jnp.where(ii <= jj, 1.0, 0.0).astype(BF16)
    lane = lax.broadcasted_iota(jnp.int32, (1, width), 1)
    lane_bwd = (lane % 2) == 1
    lane_q = (lane // 2) % N_QUANT

    hi, lo = _split_hi_lo(gf)
    b = jnp.where(lane_bwd, _dot(triu, hi) + _dot(triu, lo), _dot(tril, hi) + _dot(tril, lo))
    tot = jnp.where(lane_bwd, b[0:1, :], b[MCHUNK - 1:MCHUNK, :])
    r = gi - b
    cm = jnp.where(lane_bwd, _cummax_rows(r, True), _cummax_rows(r, False))
    rmax = jnp.where(lane_bwd, cm[0:1, :], cm[MCHUNK - 1:MCHUNK, :])

    def to_rows(v):
        return [jnp.broadcast_to(v[:, t * LANES:(t + 1) * LANES], (LANES, LANES)).T for t in range(n_tile)]

    tot_t, rmax_t = to_rows(tot), to_rows(rmax)

    def slab(tiles, e):
        t, k = divmod(e, CHUNKS_PER_TILE)
        return tiles[t][k * SERIES:(k + 1) * SERIES, :]

    ctx_ids = list(range(n_lat, n_ext))
    orders = (ctx_ids + list(range(n_lat)), ctx_ids[::-1] + list(range(n_lat - 1, -1, -1)))
    m_prev, m_new = [{}, {}], [{}, {}]
    for d, order in enumerate(orders):
        m = jnp.zeros((SERIES, LANES), F32)
        for e in order:
            m_prev[d][e] = m
            tot_e = slab(tot_t, e)
            m = jnp.maximum(tot_e + m, tot_e + slab(rmax_t, e))
            m_new[d][e] = m
    row_bwd = (lax.broadcasted_iota(jnp.int32, (SERIES, LANES), 0) % 2) == 1
    mp_slabs = [jnp.where(row_bwd, m_prev[1][e], m_prev[0][e]) for e in range(n_ext)]
    mn_slabs = [jnp.where(row_bwd, m_new[1][e], m_new[0][e]) for e in range(n_ext)]
    sp_slabs = [jnp.exp(slab(tot_t, e) + mp_slabs[e] - mn_slabs[e]) for e in range(n_ext)]

    def tiles_of(slabs):
        pad = [jnp.zeros((SERIES, LANES), F32)] * (n_tile * CHUNKS_PER_TILE - n_ext)
        full = slabs + pad
        return [jnp.concatenate(full[t * CHUNKS_PER_TILE:(t + 1) * CHUNKS_PER_TILE], axis=0) for t in range(n_tile)]

    def to_lanes(slabs):
        return jnp.concatenate([t.T[0:1, :] for t in tiles_of(slabs)], axis=1)

    mp = to_lanes(mp_slabs)
    mn = to_lanes(mn_slabs)

    g = jnp.maximum(mp, cm)
    e_w = jnp.exp(tot + r - mn)
    s_inter = jnp.exp(mp - g)
    floor = jnp.exp(-(b + g))
    packed = jnp.where(lane_q == Q_G, g, jnp.where(lane_q == Q_E, e_w, jnp.where(lane_q == Q_S, s_inter, floor)))

    sp_tiles = tiles_of(sp_slabs)
    for t in range(n_tile):
        rt_ref[t * LANES:(t + 1) * LANES, :] = packed[:, t * LANES:(t + 1) * LANES].T
        sp_ref[t * LANES:(t + 1) * LANES, :] = sp_tiles[t]
    for e in range(n_ext):
        t, k = divmod(e, CHUNKS_PER_TILE)
        tile = r[:, t * LANES:(t + 1) * LANES]
        for h in range(N_HEADS):
            shift = k * SERIES + h * ROWS_PER_HEAD
            cols_ref[h, e] = tile if shift == 0 else pltpu.roll(tile, LANES - shift, 1)


def _mlstm_kernel(q_ref, k_ref, vt_ref, kx_ref, vtx_ref, gf_ref, gi_ref, gfx_ref, gix_ref,
                  o_ref, cols_ref, rt_ref, sp_ref, ct_ref, n_ref, cprev_ref, nprev_ref):
    n_lat = q_ref.shape[0] // MCHUNK
    n_ctx = kx_ref.shape[0] // MCHUNK
    h = pl.program_id(1)

    @pl.when(h == 0)
    def _():
        gf = jnp.concatenate([gf_ref[...], gfx_ref[...]], axis=1)
        gi = jnp.concatenate([gi_ref[...], gix_ref[...]], axis=1)
        _gate_prep(gf, gi, n_lat, n_ctx, cols_ref, rt_ref, sp_ref)
        nprev_ref[...] = jnp.zeros_like(nprev_ref)

    def series_rows(ref, e):
        r0 = pl.multiple_of(e * SERIES + h * ROWS_PER_HEAD, ROWS_PER_HEAD)
        return ref[pl.ds(r0, ROWS_PER_HEAD), :]

    def row(rt, quantity, d):
        i = 2 * quantity + d
        return rt[i:i + 1, :]

    def state_step(d, e, k, vt, c_lat):
        rt = series_rows(rt_ref, e)
        s_row = series_rows(sp_ref, e)[d:d + 1, :]
        s2 = jnp.concatenate([s_row, s_row], axis=1)
        vet = (vt.astype(F32) * row(rt, Q_E, d)).astype(BF16)
        rt_hi, rt_lo = _split_hi_lo(rt)
        ek = (_dot(rt_hi, k) + _dot(rt_lo, k))[2 * Q_E + d:2 * Q_E + d + 1, :]
        ct_old = ct_ref[...]
        n_old = n_ref[...]
        if c_lat is not None:
            cprev_ref[d, c_lat] = ct_old.astype(BF16)
            nprev_ref[c_lat, d:d + 1, :] = n_old
        ct_ref[...] = s2 * ct_old + _dot(vet, k)
        n_ref[...] = s2 * n_old + ek

    for d in range(2):
        ct_ref[...] = jnp.zeros_like(ct_ref)
        n_ref[...] = jnp.zeros_like(n_ref)
        for cx in (range(n_ctx - 1, -1, -1) if d else range(n_ctx)):
            rows = slice(cx * MCHUNK, (cx + 1) * MCHUNK)
            state_step(d, n_lat + cx, kx_ref[rows, :], vtx_ref[cx], None)

        def body(t, carry, d=d):
            c = (n_lat - 1 - t) if d else t
            rows = pl.ds(pl.multiple_of(c * MCHUNK, MCHUNK), MCHUNK)
            state_step(d, c, k_ref[rows, :], vt_ref[c], c)
            return carry

        lax.fori_loop(0, n_lat, body, 0, unroll=2)

    jj = lax.broadcasted_iota(jnp.int32, (MCHUNK, MCHUNK), 0)
    ii = lax.broadcasted_iota(jnp.int32, (MCHUNK, MCHUNK), 1)
    visible = (jj <= ii, jj >= ii)

    def out_body(c, carry):
        rows = pl.ds(pl.multiple_of(c * MCHUNK, MCHUNK), MCHUNK)
        q, k, vt = q_ref[rows, :], k_ref[rows, :], vt_ref[c]
        cols = cols_ref[h, c]
        rt = series_rows(rt_ref, c)
        st = _dot_nt(k, q)
        at = []
        for d in range(2):
            w = jnp.exp(jnp.where(visible[d], cols[:, d:d + 1] - row(rt, Q_G, d), NEG))
            at.append(w * st)
        num = _dot(vt, jnp.concatenate(at, axis=1).astype(BF16))
        n_hi, n_lo = _split_hi_lo(nprev_ref[c])
        qn = _dot_nt(n_hi, q) + _dot_nt(n_lo, q)
        out = None
        for d in range(2):
            s_inter = row(rt, Q_S, d)
            den = jnp.sum(at[d], axis=0, keepdims=True) + s_inter * qn[d:d + 1, :]
            inv = 1.0 / jnp.maximum(jnp.abs(den), row(rt, Q_FL, d))
            hd = (num[:, d * MCHUNK:(d + 1) * MCHUNK] + s_inter * _dot_nt(cprev_ref[d, c], q)) * inv
            out = hd if out is None else out + hd
        o_ref[c] = out.astype(o_ref.dtype)
        return carry

    lax.fori_loop(0, n_lat, out_body, 0, unroll=2)


def _mlstm(q, k, vt, kx, vtx, gf, gi, gfx, gix):
    B, L, _ = q.shape
    Lx = kx.shape[1]
    n_lat, n_ctx = L // MCHUNK, Lx // MCHUNK
    n_ext = n_lat + n_ctx
    n_tile = gf.shape[2] // LANES + gfx.shape[2] // LANES
    head_cols = lambda n: pl.BlockSpec((None, n, HEAD_DIM), lambda b, h: (b, 0, h))
    head_rows = lambda n: pl.BlockSpec((None, n, HEAD_DIM, MCHUNK), lambda b, h: (b, 0, h, 0))
    gate_spec = lambda a: pl.BlockSpec((None,) + a.shape[1:], lambda b, h: (b, 0, 0))
    return pl.pallas_call(
        _mlstm_kernel,
        grid=(B, N_HEADS),
        in_specs=[
            head_cols(L), head_cols(L), head_rows(n_lat), head_cols(Lx), head_rows(n_ctx),
            gate_spec(gf), gate_spec(gi), gate_spec(gfx), gate_spec(gix),
        ],
        out_specs=head_rows(n_lat),
        out_shape=jax.ShapeDtypeStruct((B, n_lat, D_MODEL, MCHUNK), BF16),
        scratch_shapes=[
            pltpu.VMEM((N_HEADS, n_ext, MCHUNK, LANES), F32),
            pltpu.VMEM((n_tile * LANES, MCHUNK), F32),
            pltpu.VMEM((n_tile * LANES, LANES), F32),
            pltpu.VMEM((HEAD_DIM, HEAD_DIM), F32),
            pltpu.VMEM((1, HEAD_DIM), F32),
            pltpu.VMEM((2, n_lat, HEAD_DIM, HEAD_DIM), BF16),
            pltpu.VMEM((n_lat, ROWS_PER_HEAD, HEAD_DIM), F32),
        ],
        compiler_params=pltpu.CompilerParams(
            dimension_semantics=("parallel", "arbitrary"), vmem_limit_bytes=VMEM_LIMIT),
        name="mlstm",
    )(q, k, vt, kx, vtx, gf, gi, gfx, gix)


def _rms(x, g):
    ms = jnp.mean(x * x, axis=-1, keepdims=True)
    return x * lax.rsqrt(ms + EPS) * g


def _merge_mlp_kernel(x_ref, p_ref, ht_ref, g1_ref, sh2_ref, sc2_ref, g2_ref,
                      n2_ref, nf_ref, gmh_ref, gsgu_ref, wa_ref, wb_ref, wo_ref, ws_ref, bst_ref,
                      w1_ref, w2_ref, o_ref, t_ref):
    tm = x_ref.shape[0]
    D = D_MODEL
    seg = lambda s: p_ref[:, s * D:(s + 1) * D].astype(F32)

    h_m = jnp.concatenate([ht_ref[mc].astype(F32).T for mc in range(tm // MCHUNK)], axis=0)
    hm = jax.nn.sigmoid(seg(0)) * h_m
    gmh = gmh_ref[...]
    parts = []
    for h in range(N_HEADS):
        cols = slice(h * HEAD_DIM, (h + 1) * HEAD_DIM)
        parts.append(_rms(hm[:, cols], gmh[:, cols]).astype(BF16))
    ya = _dot(jnp.concatenate(parts, axis=1), wa_ref[...])

    u = jax.nn.gelu(seg(1))
    vg = jax.nn.gelu(seg(2))
    vn = _rms(vg, gsgu_ref[...]).astype(BF16)
    bst = bst_ref[...]
    for cc in range(tm // CHUNK):
        rows = slice(cc * CHUNK, (cc + 1) * CHUNK)
        for g in range(N_GROUPS):
            cols = slice(g * GROUP_DIM, (g + 1) * GROUP_DIM)
            s = _dot(ws_ref[g], vn[rows, cols]) + bst[:, g:g + 1]
            t_ref[rows, cols] = (u[rows, cols] * s).astype(BF16)
    yb = _dot(t_ref[...], wb_ref[...])

    y = jax.nn.sigmoid(seg(3)) * ya + jax.nn.sigmoid(seg(4)) * yb
    mix = _dot(y.astype(BF16), wo_ref[...])
    x1 = x_ref[...] + g1_ref[...] * mix

    xn2 = (_rms(x1, n2_ref[...]) * (1.0 + sc2_ref[...]) + sh2_ref[...]).astype(BF16)
    ff = D_FF // 4
    acc = jnp.zeros((tm, D), F32)
    for kk in range(4):
        hmid = jnp.maximum(_dot(xn2, w1_ref[:, kk * ff:(kk + 1) * ff]), 0.0)
        acc = acc + _dot((hmid * hmid).astype(BF16), w2_ref[kk * ff:(kk + 1) * ff, :])
    x2 = x1 + g2_ref[...] * acc
    o_ref[...] = _rms(x2, nf_ref[...])


def _merge_mlp(x, pm, ht, mod3, norm2, norm_f, g_mh, g_sgu, w_a, w_b, w_out, w_s, b_st, w1, w2, tm):
    B, L, D = x.shape
    mod_spec = lambda k: pl.BlockSpec((None, 1, D), lambda b, i: (b, 0, k))
    return pl.pallas_call(
        _merge_mlp_kernel,
        grid=(B, L // tm),
        in_specs=[
            pl.BlockSpec((None, tm, D), lambda b, i: (b, i, 0)),
            pl.BlockSpec((None, tm, N_MERGE_SEG * D), lambda b, i: (b, i, 0)),
            pl.BlockSpec((None, tm // MCHUNK, D, MCHUNK), lambda b, i: (b, i, 0, 0)),
            mod_spec(2), mod_spec(3), mod_spec(4), mod_spec(5),
            _const_spec((1, D)), _const_spec((1, D)), _const_spec((1, D)), _const_spec((1, D)),
            _const_spec((D, D)), _const_spec((D, D)), _const_spec((D, D)),
            _const_spec(w_s.shape), _const_spec(b_st.shape),
            _const_spec((D, D_FF)), _const_spec((D_FF, D)),
        ],
        out_specs=pl.BlockSpec((None, tm, D), lambda b, i: (b, i, 0)),
        out_shape=jax.ShapeDtypeStruct((B, L, D), F32),
        scratch_shapes=[pltpu.VMEM((tm, D), BF16)],
        compiler_params=pltpu.CompilerParams(
            dimension_semantics=("parallel", "parallel"), vmem_limit_bytes=VMEM_LIMIT),
        name="merge_mlp",
    )(x, pm, ht, mod3, mod3, mod3, mod3, norm2, norm_f, g_mh, g_sgu, w_a, w_b, w_out, w_s, b_st, w1, w2)


def _gate_weights(w_gate, b_gate):
    D = w_gate.shape[0]

    def series(a, kind_of_dir):
        a4 = a.reshape(a.shape[:-1] + (4, N_HEADS))
        per_dir = jnp.stack([a4[..., kind_of_dir[0], :], a4[..., kind_of_dir[1], :]], axis=-1)
        dup = jnp.broadcast_to(per_dir[..., :, None, :], per_dir.shape[:-1] + (N_QUANT, 2))
        return dup.reshape(a.shape[:-1] + (SERIES,))

    wf, wi = series(w_gate, (1, 3)), series(w_gate, (0, 2))
    bf, bi = series(b_gate, (1, 3)), series(b_gate, (0, 2))
    eye = jnp.eye(CHUNKS_PER_TILE, dtype=w_gate.dtype)
    place = lambda w: jnp.einsum("ab,dk->adbk", eye, w).reshape(CHUNKS_PER_TILE, D, LANES)
    wgs = jnp.concatenate([place(wf), place(wi)], axis=2).astype(BF16)
    bgs = jnp.concatenate([jnp.tile(bf, CHUNKS_PER_TILE), jnp.tile(bi, CHUNKS_PER_TILE)])[None, :]
    return wgs, bgs


def kernel(x, c, ctx, c_ctx, norm1, norm2, w_mod, b_mod, w_in, conv_qk, b_gate, g_mh, w_a, w_s, b_s,
           g_sgu, w_b, w_out, w1, w2, norm_f):
    B, S, D = x.shape
    Lx = ctx.shape[1]
    tm = 2 * MCHUNK
    assert D == D_MODEL and S % (MCHUNK * CHUNKS_PER_TILE) == 0 and Lx % MCHUNK == 0
    assert Lx // MCHUNK <= CHUNKS_PER_TILE
    assert w_mod.shape[0] == 1, "single-layer block"
    W = D_MODEL
    off_g = 3 * W
    off_o = off_g + 4 * N_HEADS

    mod_rows = ((B + 1 + 7) // 8) * 8
    cc = jnp.zeros((mod_rows, D), F32).at[:B].set(c).at[B].set(c_ctx)
    mod = _modulation(cc, w_mod[0], b_mod[0][None, :])
    mod3 = mod.reshape(mod_rows, 1, 6 * D)

    wi = w_in[0]
    w_merge = wi[:, off_o:].astype(BF16)
    wq = wi[:, :W].astype(BF16)
    wk = wi[:, W:2 * W].astype(BF16)
    wvt = wi[:, 2 * W:off_g].T.astype(BF16)
    wgs, bgs = _gate_weights(wi[:, off_g:off_o], b_gate[0])

    n1 = norm1[0][None, :]
    conv = conv_qk[0]
    pm, q, k, vt, gf, gi = _inproj(x, mod3, lambda b: b, n1, wgs, bgs, conv, (w_merge, wq, wk, wvt), tm, True)
    kx, vtx, gfx, gix = _inproj(ctx, mod3, lambda b: B, n1, wgs, bgs, conv, (wk, wvt), Lx, False)

    ht = _mlstm(q, k, vt, kx, vtx, gf, gi, gfx, gix)

    return _merge_mlp(
        x, pm, ht, mod3, norm2[0][None, :], norm_f[None, :], g_mh[0][None, :], g_sgu[0][None, :],
        w_a[0].astype(BF16), w_b[0].astype(BF16), w_out[0].astype(BF16),
        w_s[0].astype(BF16), b_s[0].T, w1[0].astype(BF16), w2[0].astype(BF16), 256)
```

```python
import jax
import jax.numpy as jnp
from jax import lax
from jax.experimental import pallas as pl
from jax.experimental.pallas import tpu as pltpu

D_MODEL = 1024
N_HEADS = 4
HEAD_DIM = D_MODEL // N_HEADS
CHUNK = 128
MCHUNK = 256
GROUP_DIM = 128
N_GROUPS = D_MODEL // GROUP_DIM
D_FF = 4 * D_MODEL
CONV_W = 3
NEG = -1e30
EPS = 1e-6
LANES = 128
HALO = 16
EXTRA = 16

N_QUANT = 4
ROWS_PER_HEAD = 2 * N_QUANT
SERIES = N_HEADS * ROWS_PER_HEAD
CHUNKS_PER_TILE = LANES // SERIES
Q_G, Q_E, Q_S, Q_FL = 0, 1, 2, 3

N_MERGE_SEG = 5

VMEM_LIMIT = 60000 * 1024

F32 = jnp.float32
BF16 = jnp.bfloat16


def _dot(a, b):
    return jnp.dot(a, b, preferred_element_type=F32)


def _dot_nt(a, b):
    return lax.dot_general(a, b, (((1,), (1,)), ((), ())), preferred_element_type=F32)


def _split_hi_lo(x):
    hi = x.astype(BF16)
    lo = (x - hi.astype(F32)).astype(BF16)
    return hi, lo


def _silu(x):
    return x * jax.nn.sigmoid(x)


def _const_spec(shape):
    nd = len(shape)
    return pl.BlockSpec(shape, lambda *_: (0,) * nd, pipeline_mode=pl.Buffered(1))


def _mod_kernel(c_ref, w_ref, b_ref, o_ref):
    s = _silu(c_ref[...])
    s_hi, s_lo = _split_hi_lo(s)
    w_hi, w_lo = _split_hi_lo(w_ref[...])
    acc = _dot(s_hi, w_hi) + _dot(s_hi, w_lo) + _dot(s_lo, w_hi)
    o_ref[...] = acc + b_ref[...]


def _modulation(cc, w_mod, b_mod):
    rows, d = cc.shape
    n = w_mod.shape[1]
    tn = 1536
    return pl.pallas_call(
        _mod_kernel,
        grid=(n // tn,),
        in_specs=[
            pl.BlockSpec((rows, d), lambda j: (0, 0)),
            pl.BlockSpec((d, tn), lambda j: (0, j)),
            pl.BlockSpec((1, tn), lambda j: (0, j)),
        ],
        out_specs=pl.BlockSpec((rows, tn), lambda j: (0, j)),
        out_shape=jax.ShapeDtypeStruct((rows, n), F32),
        compiler_params=pltpu.CompilerParams(
            dimension_semantics=("arbitrary",), vmem_limit_bytes=VMEM_LIMIT),
        name="modulation",
    )(cc, w_mod, b_mod)


def _log_sigmoid(x):
    return jnp.minimum(x, 0.0) - jnp.log1p(jnp.exp(-jnp.abs(x)))


def _make_inproj_kernel(tm, n_tiles, latent):
    n_cc = tm // CHUNK
    n_mc = tm // MCHUNK
    tiles_per_gate_tile = CHUNKS_PER_TILE // n_mc

    def kernel(*refs):
        if latent:
            (x_ref, xp_ref, xnx_ref, sh_ref, sc_ref, g_ref, wg_ref, bg_ref, cw_ref, wm_ref, wq_ref, wk_ref, wvt_ref,
             pm_ref, q_ref, k_ref, vt_ref, gf_ref, gi_ref, xe_ref, peq_ref, pek_ref) = refs
        else:
            (x_ref, xp_ref, xnx_ref, sh_ref, sc_ref, g_ref, wg_ref, bg_ref, cw_ref, wk_ref, wvt_ref,
             k_ref, vt_ref, gf_ref, gi_ref, xe_ref, pek_ref) = refs
        i = pl.program_id(1)

        def normed(x):
            ms = jnp.mean(x * x, axis=-1, keepdims=True)
            y = x * lax.rsqrt(ms + EPS) * g_ref[...]
            return y * (1.0 + sc_ref[...]) + sh_ref[...]

        xn = normed(x_ref[...]).astype(BF16)
        xe_ref[HALO:HALO + tm, :] = xn
        xe_ref[0:HALO, :] = jnp.where(i > 0, normed(xp_ref[...]), 0.0).astype(BF16)
        xe_ref[HALO + tm:, :] = jnp.where(i < n_tiles - 1, normed(xnx_ref[...]), 0.0).astype(BF16)

        def conv_silu(pe_ref, w_ref, cw, scale, out_ref):
            pe_ref[...] = _dot(xe_ref[...], w_ref[...])
            for cc in range(n_cc):
                r = HALO + cc * CHUNK
                y = (pe_ref[pl.ds(r - 1, CHUNK), :] * cw[0:1, :] + pe_ref[pl.ds(r, CHUNK), :] * cw[1:2, :]
                     + pe_ref[pl.ds(r + 1, CHUNK), :] * cw[2:3, :])
                y = _silu(y)
                if scale != 1.0:
                    y = y * scale
                out_ref[cc * CHUNK:(cc + 1) * CHUNK, :] = y.astype(BF16)

        cw = cw_ref[...]
        if latent:
            conv_silu(peq_ref, wq_ref, cw[:, :D_MODEL], 1.0, q_ref)
        conv_silu(pek_ref, wk_ref, cw[:, D_MODEL:], HEAD_DIM ** -0.5, k_ref)

        sub = i % tiles_per_gate_tile
        acc = _dot(xn[0:MCHUNK], wg_ref[sub * n_mc])
        for mc in range(1, n_mc):
            acc = acc + _dot(xn[mc * MCHUNK:(mc + 1) * MCHUNK], wg_ref[sub * n_mc + mc])
        acc = acc + bg_ref[...]
        group = lax.broadcasted_iota(jnp.int32, (1, LANES), 1) // SERIES
        own = (group >= sub * n_mc) & (group < (sub + 1) * n_mc)
        new_f = jnp.where(own, _log_sigmoid(acc[:, :LANES]), 0.0)
        new_i = jnp.where(own, acc[:, LANES:], 0.0)

        @pl.when(sub == 0)
        def _():
            gf_ref[...] = new_f
            gi_ref[...] = new_i

        @pl.when(sub != 0)
        def _():
            gf_ref[...] += new_f
            gi_ref[...] += new_i

        if latent:
            for s in range(N_MERGE_SEG):
                cols = slice(s * D_MODEL, (s + 1) * D_MODEL)
                pm_ref[:, cols] = _dot(xn, wm_ref[:, cols]).astype(BF16)
        vt = _dot_nt(wvt_ref[...], xn)
        for mc in range(n_mc):
            vt_ref[mc] = vt[:, mc * MCHUNK:(mc + 1) * MCHUNK].astype(BF16)

    return kernel


def _inproj(x, mod3, mod_row_of_batch, norm_g, wgs, bgs, conv_qk, weights, tm, latent):
    B, L, D = x.shape
    n_tiles = L // tm
    n_mc = tm // MCHUNK
    tiles_per_gate_tile = CHUNKS_PER_TILE // n_mc
    n_gate_tiles = -(-n_tiles // tiles_per_gate_tile)
    hb = tm // HALO
    n_hblk = L // HALO
    row = mod_row_of_batch
    tok = lambda width: pl.BlockSpec((None, tm, width), lambda b, i: (b, i, 0))
    tok_shape = lambda width: jax.ShapeDtypeStruct((B, L, width), BF16)
    widths = [w.shape[1] for w in weights[:-1]]
    gate_spec = pl.BlockSpec((None, MCHUNK, LANES), lambda b, i: (b, 0, i // tiles_per_gate_tile))
    gate_shape = jax.ShapeDtypeStruct((B, MCHUNK, n_gate_tiles * LANES), F32)
    vt_spec = pl.BlockSpec((None, n_mc, D, MCHUNK), lambda b, i: (b, i, 0, 0))
    vt_shape = jax.ShapeDtypeStruct((B, L // MCHUNK, D, MCHUNK), BF16)
    return pl.pallas_call(
        _make_inproj_kernel(tm, n_tiles, latent),
        grid=(B, n_tiles),
        in_specs=[
            pl.BlockSpec((None, tm, D), lambda b, i: (b, i, 0)),
            pl.BlockSpec((None, HALO, D), lambda b, i: (b, jnp.maximum(i * hb - 1, 0), 0)),
            pl.BlockSpec((None, HALO, D), lambda b, i: (b, jnp.minimum((i + 1) * hb, n_hblk - 1), 0)),
            pl.BlockSpec((None, 1, D), lambda b, i: (row(b), 0, 0)),
            pl.BlockSpec((None, 1, D), lambda b, i: (row(b), 0, 1)),
            _const_spec((1, D)), _const_spec(wgs.shape), _const_spec(bgs.shape), _const_spec(conv_qk.shape),
        ] + [_const_spec(w.shape) for w in weights],
        out_specs=[tok(w) for w in widths] + [vt_spec, gate_spec, gate_spec],
        out_shape=[tok_shape(w) for w in widths] + [vt_shape, gate_shape, gate_shape],
        scratch_shapes=[pltpu.VMEM((tm + 2 * HALO, D), BF16)]
        + [pltpu.VMEM((tm + 2 * HALO, D), F32)] * (2 if latent else 1),
        compiler_params=pltpu.CompilerParams(
            dimension_semantics=("parallel", "arbitrary"), vmem_limit_bytes=VMEM_LIMIT),
        name="inproj_latent" if latent else "inproj_context",
    )(x, x, x, mod3, mod3, norm_g, wgs, bgs, conv_qk, *weights)


def _cummax_rows(x, reverse):
    n = x.shape[0]
    rows = lax.broadcasted_iota(jnp.int32, x.shape, 0)
    k = 1
    while k < n:
        if reverse:
            shifted = jnp.where(rows < n - k, pltpu.roll(x, n - k, 0), NEG)
        else:
            shifted = jnp.where(rows >= k, pltpu.roll(x, k, 0), NEG)
        x = jnp.maximum(x, shifted)
        k *= 2
    return x


def _gate_prep(gf, gi, n_lat, n_ctx, cols_ref, rt_ref, sp_ref):
    n_ext = n_lat + n_ctx
    width = gf.shape[1]
    n_tile = width // LANES
    ii = lax.broadcasted_iota(jnp.int32, (MCHUNK, MCHUNK), 0)
    jj = lax.broadcasted_iota(jnp.int32, (MCHUNK, MCHUNK), 1)
    tril = jnp.where(ii >= jj, 1.0, 0.0).astype(BF16)
    triu = jnp.where(ii <= jj, 1.0, 0.0).astype(BF16)
    lane = lax.broadcasted_iota(jnp.int32, (1, width), 1)
    lane_bwd = (lane % 2) == 1
    lane_q = (lane // 2) % N_QUANT

    hi, lo = _split_hi_lo(gf)
    b = jnp.where(lane_bwd, _dot(triu, hi) + _dot(triu, lo), _dot(tril, hi) + _dot(tril, lo))
    tot = jnp.where(lane_bwd, b[0:1, :], b[MCHUNK - 1:MCHUNK, :])
    r = gi - b
    cm = jnp.where(lane_bwd, _cummax_rows(r, True), _cummax_rows(r, False))
    rmax = jnp.where(lane_bwd, cm[0:1, :], cm[MCHUNK - 1:MCHUNK, :])

    def to_rows(v):
        return [jnp.broadcast_to(v[:, t * LANES:(t + 1) * LANES], (LANES, LANES)).T for t in range(n_tile)]

    tot_t, rmax_t = to_rows(tot), to_rows(rmax)

    def slab(tiles, e):
        t, k = divmod(e, CHUNKS_PER_TILE)
        return tiles[t][k * SERIES:(k + 1) * SERIES, :]

    ctx_ids = list(range(n_lat, n_ext))
    orders = (ctx_ids + list(range(n_lat)), ctx_ids[::-1] + list(range(n_lat - 1, -1, -1)))
    m_prev, m_new = [{}, {}], [{}, {}]
    for d, order in enumerate(orders):
        m = jnp.zeros((SERIES, LANES), F32)
        for e in order:
            m_prev[d][e] = m
            tot_e = slab(tot_t, e)
            m = jnp.maximum(tot_e + m, tot_e + slab(rmax_t, e))
            m_new[d][e] = m
    row_bwd = (lax.broadcasted_iota(jnp.int32, (SERIES, LANES), 0) % 2) == 1
    mp_slabs = [jnp.where(row_bwd, m_prev[1][e], m_prev[0][e]) for e in range(n_ext)]
    mn_slabs = [jnp.where(row_bwd, m_new[1][e], m_new[0][e]) for e in range(n_ext)]
    sp_slabs = [jnp.exp(slab(tot_t, e) + mp_slabs[e] - mn_slabs[e]) for e in range(n_ext)]

    def tiles_of(slabs):
        pad = [jnp.zeros((SERIES, LANES), F32)] * (n_tile * CHUNKS_PER_TILE - n_ext)
        full = slabs + pad
        return [jnp.concatenate(full[t * CHUNKS_PER_TILE:(t + 1) * CHUNKS_PER_TILE], axis=0) for t in range(n_tile)]

    def to_lanes(slabs):
        return jnp.concatenate([t.T[0:1, :] for t in tiles_of(slabs)], axis=1)

    mp = to_lanes(mp_slabs)
    mn = to_lanes(mn_slabs)

    g = jnp.maximum(mp, cm)
    e_w = jnp.exp(tot + r - mn)
    s_inter = jnp.exp(mp - g)
    floor = jnp.exp(-(b + g))
    packed = jnp.where(lane_q == Q_G, g, jnp.where(lane_q == Q_E, e_w, jnp.where(lane_q == Q_S, s_inter, floor)))

    sp_tiles = tiles_of(sp_slabs)
    for t in range(n_tile):
        rt_ref[t * LANES:(t + 1) * LANES, :] = packed[:, t * LANES:(t + 1) * LANES].T
        sp_ref[t * LANES:(t + 1) * LANES, :] = sp_tiles[t]
    for e in range(n_ext):
        t, k = divmod(e, CHUNKS_PER_TILE)
        tile = r[:, t * LANES:(t + 1) * LANES]
        for h in range(N_HEADS):
            shift = k * SERIES + h * ROWS_PER_HEAD
            cols_ref[h, e] = tile if shift == 0 else pltpu.roll(tile, LANES - shift, 1)


def _mlstm_kernel(q_ref, k_ref, vt_ref, kx_ref, vtx_ref, gf_ref, gi_ref, gfx_ref, gix_ref,
                  o_ref, cols_ref, rt_ref, sp_ref, ct_ref, n_ref, cprev_ref, at_ref):
    n_lat = q_ref.shape[0] // MCHUNK
    n_ctx = kx_ref.shape[0] // MCHUNK
    h = pl.program_id(1)

    @pl.when(h == 0)
    def _():
        gf = jnp.concatenate([gf_ref[...], gfx_ref[...]], axis=1)
        gi = jnp.concatenate([gi_ref[...], gix_ref[...]], axis=1)
        _gate_prep(gf, gi, n_lat, n_ctx, cols_ref, rt_ref, sp_ref)

    def series_rows(ref, e):
        r0 = pl.multiple_of(e * SERIES + h * ROWS_PER_HEAD, ROWS_PER_HEAD)
        return ref[pl.ds(r0, ROWS_PER_HEAD), :]

    def row(rt, quantity, d):
        i = 2 * quantity + d
        return rt[i:i + 1, :]

    def state_step(d, e, k, vt, c_lat):
        r0 = pl.multiple_of(e * SERIES + h * ROWS_PER_HEAD, ROWS_PER_HEAD)
        rt_x = rt_ref[pl.ds(r0, EXTRA), :]
        s_row = series_rows(sp_ref, e)[d:d + 1, :]
        s2 = jnp.concatenate([s_row, s_row], axis=1)
        vet = (vt.astype(F32) * row(rt_x, Q_E, d)).astype(BF16)
        upd = _dot(jnp.concatenate([vet, rt_x.astype(BF16)], axis=0), k)
        i_e = HEAD_DIM + 2 * Q_E + d
        ek = upd[i_e:i_e + 1, :]
        ct_old = ct_ref[d]
        n_old = n_ref[d]
        if c_lat is not None:
            cprev_ref[d, c_lat, 0:HEAD_DIM, :] = ct_old.astype(BF16)
            cprev_ref[d, c_lat, HEAD_DIM:, :] = jnp.broadcast_to(n_old, (EXTRA, HEAD_DIM)).astype(BF16)
        ct_ref[d] = s2 * ct_old + upd[0:HEAD_DIM, :]
        n_ref[d] = s2 * n_old + ek

    ct_ref[...] = jnp.zeros_like(ct_ref)
    n_ref[...] = jnp.zeros_like(n_ref)
    for t in range(n_ctx):
        for d in range(2):
            cx = (n_ctx - 1 - t) if d else t
            rows = slice(cx * MCHUNK, (cx + 1) * MCHUNK)
            state_step(d, n_lat + cx, kx_ref[rows, :], vtx_ref[cx], None)

    def state_body(t, carry):
        for d in range(2):
            c = (n_lat - 1 - t) if d else t
            rows = pl.ds(pl.multiple_of(c * MCHUNK, MCHUNK), MCHUNK)
            state_step(d, c, k_ref[rows, :], vt_ref[c], c)
        return carry

    lax.fori_loop(0, n_lat, state_body, 0, unroll=True)

    jj = lax.broadcasted_iota(jnp.int32, (MCHUNK, MCHUNK), 0)
    ii = lax.broadcasted_iota(jnp.int32, (MCHUNK, MCHUNK), 1)
    visible = (jj <= ii, jj >= ii)

    ones_rows = jnp.ones((EXTRA, MCHUNK), BF16)

    def weights_stage(c, slot):
        rows = pl.ds(pl.multiple_of(c * MCHUNK, MCHUNK), MCHUNK)
        cols = cols_ref[h, c]
        rt = series_rows(rt_ref, c)
        st = _dot_nt(k_ref[rows, :], q_ref[rows, :])
        for d in range(2):
            w = jnp.exp(jnp.where(visible[d], cols[:, d:d + 1] - row(rt, Q_G, d), NEG))
            at_ref[slot, :, d * MCHUNK:(d + 1) * MCHUNK] = (w * st).astype(BF16)

    def readout_stage(c, slot):
        rows = pl.ds(pl.multiple_of(c * MCHUNK, MCHUNK), MCHUNK)
        q = q_ref[rows, :]
        rt = series_rows(rt_ref, c)
        at = at_ref[slot]
        num = _dot(jnp.concatenate([vt_ref[c], ones_rows], axis=0), at)
        out = None
        for d in range(2):
            lanes = slice(d * MCHUNK, (d + 1) * MCHUNK)
            s_inter = row(rt, Q_S, d)
            inter = _dot_nt(cprev_ref[d, c], q)
            den = num[HEAD_DIM:HEAD_DIM + 1, lanes] + s_inter * inter[HEAD_DIM:HEAD_DIM + 1, :]
            inv = 1.0 / jnp.maximum(jnp.abs(den), row(rt, Q_FL, d))
            hd = (num[0:HEAD_DIM, lanes] + s_inter * inter[0:HEAD_DIM, :]) * inv
            out = hd if out is None else out + hd
        o_ref[c] = out.astype(o_ref.dtype)

    weights_stage(0, 0)

    def out_body(c, carry):
        readout_stage(c, c % 2)
        weights_stage(c + 1, (c + 1) % 2)
        return carry

    lax.fori_loop(0, n_lat - 1, out_body, 0, unroll=True)
    readout_stage(n_lat - 1, (n_lat - 1) % 2)


def _mlstm(q, k, vt, kx, vtx, gf, gi, gfx, gix):
    B, L, _ = q.shape
    Lx = kx.shape[1]
    n_lat, n_ctx = L // MCHUNK, Lx // MCHUNK
    n_ext = n_lat + n_ctx
    n_tile = gf.shape[2] // LANES + gfx.shape[2] // LANES
    head_cols = lambda n: pl.BlockSpec((None, n, HEAD_DIM), lambda b, h: (b, 0, h))
    head_rows = lambda n: pl.BlockSpec((None, n, HEAD_DIM, MCHUNK), lambda b, h: (b, 0, h, 0))
    gate_spec = lambda a: pl.BlockSpec((None,) + a.shape[1:], lambda b, h: (b, 0, 0))
    return pl.pallas_call(
        _mlstm_kernel,
        grid=(B, N_HEADS),
        in_specs=[
            head_cols(L), head_cols(L), head_rows(n_lat), head_cols(Lx), head_rows(n_ctx),
            gate_spec(gf), gate_spec(gi), gate_spec(gfx), gate_spec(gix),
        ],
        out_specs=head_rows(n_lat),
        out_shape=jax.ShapeDtypeStruct((B, n_lat, D_MODEL, MCHUNK), BF16),
        scratch_shapes=[
            pltpu.VMEM((N_HEADS, n_ext, MCHUNK, LANES), F32),
            pltpu.VMEM((n_tile * LANES, MCHUNK), F32),
            pltpu.VMEM((n_tile * LANES, LANES), F32),
            pltpu.VMEM((2, HEAD_DIM, HEAD_DIM), F32),
            pltpu.VMEM((2, 1, HEAD_DIM), F32),
            pltpu.VMEM((2, n_lat, HEAD_DIM + EXTRA, HEAD_DIM), BF16),
            pltpu.VMEM((2, MCHUNK, 2 * MCHUNK), BF16),
        ],
        compiler_params=pltpu.CompilerParams(
            dimension_semantics=("parallel", "arbitrary"), vmem_limit_bytes=VMEM_LIMIT),
        name="mlstm",
    )(q, k, vt, kx, vtx, gf, gi, gfx, gix)


def _rms(x, g):
    ms = jnp.mean(x * x, axis=-1, keepdims=True)
    return x * lax.rsqrt(ms + EPS) * g


def _merge_mlp_kernel(x_ref, p_ref, ht_ref, g1_ref, sh2_ref, sc2_ref, g2_ref,
                      n2_ref, nf_ref, gmh_ref, gsgu_ref, wa_ref, wb_ref, wo_ref, ws_ref, bst_ref,
                      w1_ref, w2_ref, o_ref, t_ref):
    tm = x_ref.shape[0]
    D = D_MODEL
    seg = lambda s: p_ref[:, s * D:(s + 1) * D].astype(F32)

    h_m = jnp.concatenate([ht_ref[mc].astype(F32).T for mc in range(tm // MCHUNK)], axis=0)
    hm = jax.nn.sigmoid(seg(0)) * h_m
    gmh = gmh_ref[...]
    parts = []
    for h in range(N_HEADS):
        cols = slice(h * HEAD_DIM, (h + 1) * HEAD_DIM)
        parts.append(_rms(hm[:, cols], gmh[:, cols]).astype(BF16))
    ya = _dot(jnp.concatenate(parts, axis=1), wa_ref[...])

    u = jax.nn.gelu(seg(1))
    vg = jax.nn.gelu(seg(2))
    vn = _rms(vg, gsgu_ref[...]).astype(BF16)
    bst = bst_ref[...]
    for cc in range(tm // CHUNK):
        rows = slice(cc * CHUNK, (cc + 1) * CHUNK)
        for g in range(N_GROUPS):
            cols = slice(g * GROUP_DIM, (g + 1) * GROUP_DIM)
            s = _dot(ws_ref[g], vn[rows, cols]) + bst[:, g:g + 1]
            t_ref[rows, cols] = (u[rows, cols] * s).astype(BF16)
    yb = _dot(t_ref[...], wb_ref[...])

    y = jax.nn.sigmoid(seg(3)) * ya + jax.nn.sigmoid(seg(4)) * yb
    mix = _dot(y.astype(BF16), wo_ref[...])
    x1 = x_ref[...] + g1_ref[...] * mix

    xn2 = (_rms(x1, n2_ref[...]) * (1.0 + sc2_ref[...]) + sh2_ref[...]).astype(BF16)
    ff = D_FF // 4
    acc = jnp.zeros((tm, D), F32)
    for kk in range(4):
        hmid = jnp.maximum(_dot(xn2, w1_ref[:, kk * ff:(kk + 1) * ff]), 0.0)
        acc = acc + _dot((hmid * hmid).astype(BF16), w2_ref[kk * ff:(kk + 1) * ff, :])
    x2 = x1 + g2_ref[...] * acc
    o_ref[...] = _rms(x2, nf_ref[...])


def _merge_mlp(x, pm, ht, mod3, norm2, norm_f, g_mh, g_sgu, w_a, w_b, w_out, w_s, b_st, w1, w2, tm):
    B, L, D = x.shape
    mod_spec = lambda k: pl.BlockSpec((None, 1, D), lambda b, i: (b, 0, k))
    return pl.pallas_call(
        _merge_mlp_kernel,
        grid=(B, L // tm),
        in_specs=[
            pl.BlockSpec((None, tm, D), lambda b, i: (b, i, 0)),
            pl.BlockSpec((None, tm, N_MERGE_SEG * D), lambda b, i: (b, i, 0)),
            pl.BlockSpec((None, tm // MCHUNK, D, MCHUNK), lambda b, i: (b, i, 0, 0)),
            mod_spec(2), mod_spec(3), mod_spec(4), mod_spec(5),
            _const_spec((1, D)), _const_spec((1, D)), _const_spec((1, D)), _const_spec((1, D)),
            _const_spec((D, D)), _const_spec((D, D)), _const_spec((D, D)),
            _const_spec(w_s.shape), _const_spec(b_st.shape),
            _const_spec((D, D_FF)), _const_spec((D_FF, D)),
        ],
        out_specs=pl.BlockSpec((None, tm, D), lambda b, i: (b, i, 0)),
        out_shape=jax.ShapeDtypeStruct((B, L, D), F32),
        scratch_shapes=[pltpu.VMEM((tm, D), BF16)],
        compiler_params=pltpu.CompilerParams(
            dimension_semantics=("parallel", "parallel"), vmem_limit_bytes=VMEM_LIMIT),
        name="merge_mlp",
    )(x, pm, ht, mod3, mod3, mod3, mod3, norm2, norm_f, g_mh, g_sgu, w_a, w_b, w_out, w_s, b_st, w1, w2)


def _gate_weights(w_gate, b_gate):
    D = w_gate.shape[0]

    def series(a, kind_of_dir):
        a4 = a.reshape(a.shape[:-1] + (4, N_HEADS))
        per_dir = jnp.stack([a4[..., kind_of_dir[0], :], a4[..., kind_of_dir[1], :]], axis=-1)
        dup = jnp.broadcast_to(per_dir[..., :, None, :], per_dir.shape[:-1] + (N_QUANT, 2))
        return dup.reshape(a.shape[:-1] + (SERIES,))

    wf, wi = series(w_gate, (1, 3)), series(w_gate, (0, 2))
    bf, bi = series(b_gate, (1, 3)), series(b_gate, (0, 2))
    eye = jnp.eye(CHUNKS_PER_TILE, dtype=w_gate.dtype)
    place = lambda w: jnp.einsum("ab,dk->adbk", eye, w).reshape(CHUNKS_PER_TILE, D, LANES)
    wgs = jnp.concatenate([place(wf), place(wi)], axis=2).astype(BF16)
    bgs = jnp.concatenate([jnp.tile(bf, CHUNKS_PER_TILE), jnp.tile(bi, CHUNKS_PER_TILE)])[None, :]
    return wgs, bgs


def kernel(x, c, ctx, c_ctx, norm1, norm2, w_mod, b_mod, w_in, conv_qk, b_gate, g_mh, w_a, w_s, b_s,
           g_sgu, w_b, w_out, w1, w2, norm_f):
    B, S, D = x.shape
    Lx = ctx.shape[1]
    tm = 2 * MCHUNK
    assert D == D_MODEL and S % (MCHUNK * CHUNKS_PER_TILE) == 0 and Lx % MCHUNK == 0
    assert Lx // MCHUNK <= CHUNKS_PER_TILE
    assert w_mod.shape[0] == 1, "single-layer block"
    W = D_MODEL
    off_g = 3 * W
    off_o = off_g + 4 * N_HEADS

    mod_rows = ((B + 1 + 7) // 8) * 8
    cc = jnp.zeros((mod_rows, D), F32).at[:B].set(c).at[B].set(c_ctx)
    mod = _modulation(cc, w_mod[0], b_mod[0][None, :])
    mod3 = mod.reshape(mod_rows, 1, 6 * D)

    wi = w_in[0]
    w_merge = wi[:, off_o:].astype(BF16)
    wq = wi[:, :W].astype(BF16)
    wk = wi[:, W:2 * W].astype(BF16)
    wvt = wi[:, 2 * W:off_g].T.astype(BF16)
    wgs, bgs = _gate_weights(wi[:, off_g:off_o], b_gate[0])

    n1 = norm1[0][None, :]
    conv = conv_qk[0]
    pm, q, k, vt, gf, gi = _inproj(x, mod3, lambda b: b, n1, wgs, bgs, conv, (w_merge, wq, wk, wvt), tm, True)
    kx, vtx, gfx, gix = _inproj(ctx, mod3, lambda b: B, n1, wgs, bgs, conv, (wk, wvt), Lx, False)

    ht = _mlstm(q, k, vt, kx, vtx, gf, gi, gfx, gix)

    return _merge_mlp(
        x, pm, ht, mod3, norm2[0][None, :], norm_f[None, :], g_mh[0][None, :], g_sgu[0][None, :],
        w_a[0].astype(BF16), w_b[0].astype(BF16), w_out[0].astype(BF16),
        w_s[0].astype(BF16), b_s[0].T, w1[0].astype(BF16), w2[0].astype(BF16), 512)
```

```python
import jax
import jax.numpy as jnp
from jax import lax
from jax.experimental import pallas as pl
from jax.experimental.pallas import tpu as pltpu

D_MODEL = 1024
N_HEADS = 4
HEAD_DIM = D_MODEL // N_HEADS
CHUNK = 128
MCHUNK = 256
GROUP_DIM = 128
N_GROUPS = D_MODEL // GROUP_DIM
D_FF = 4 * D_MODEL
CONV_W = 3
NEG = -1e30
EPS = 1e-6
LANES = 128
HALO = 16
EXTRA = 16

N_QUANT = 4
ROWS_PER_HEAD = 2 * N_QUANT
SERIES = N_HEADS * ROWS_PER_HEAD
CHUNKS_PER_TILE = LANES // SERIES
Q_G, Q_E, Q_S, Q_FL = 0, 1, 2, 3

N_MERGE_SEG = 5
MERGE_SUB = 256

VMEM_LIMIT = 60000 * 1024

F32 = jnp.float32
BF16 = jnp.bfloat16


def _dot(a, b):
    return jnp.dot(a, b, preferred_element_type=F32)


def _dot_nt(a, b):
    return lax.dot_general(a, b, (((1,), (1,)), ((), ())), preferred_element_type=F32)


def _split_hi_lo(x):
    hi = x.astype(BF16)
    lo = (x - hi.astype(F32)).astype(BF16)
    return hi, lo


def _silu(x):
    return x * jax.nn.sigmoid(x)


def _const_spec(shape):
    nd = len(shape)
    return pl.BlockSpec(shape, lambda *_: (0,) * nd, pipeline_mode=pl.Buffered(1))


def _mod_kernel(c_ref, w_ref, b_ref, o_ref):
    s = _silu(c_ref[...])
    s_hi, s_lo = _split_hi_lo(s)
    w_hi, w_lo = _split_hi_lo(w_ref[...])
    acc = _dot(s_hi, w_hi) + _dot(s_hi, w_lo) + _dot(s_lo, w_hi)
    o_ref[...] = acc + b_ref[...]


def _modulation(cc, w_mod, b_mod):
    rows, d = cc.shape
    n = w_mod.shape[2]
    tn = 1536
    return pl.pallas_call(
        _mod_kernel,
        grid=(n // tn,),
        in_specs=[
            pl.BlockSpec((rows, d), lambda j: (0, 0)),
            pl.BlockSpec((None, d, tn), lambda j: (0, 0, j)),
            pl.BlockSpec((1, tn), lambda j: (0, j)),
        ],
        out_specs=pl.BlockSpec((rows, tn), lambda j: (0, j)),
        out_shape=jax.ShapeDtypeStruct((rows, n), F32),
        compiler_params=pltpu.CompilerParams(
            dimension_semantics=("arbitrary",), vmem_limit_bytes=VMEM_LIMIT),
        name="modulation",
    )(cc, w_mod, b_mod)


def _log_sigmoid(x):
    return jnp.minimum(x, 0.0) - jnp.log1p(jnp.exp(-jnp.abs(x)))


def _make_inproj_kernel(tm, n_tiles, latent):
    n_cc = tm // CHUNK
    n_mc = tm // MCHUNK
    tiles_per_gate_tile = CHUNKS_PER_TILE // n_mc

    def kernel(*refs):
        if latent:
            (x_ref, xp_ref, xnx_ref, sh_ref, sc_ref, g_ref, wg_ref, bg_ref, cw_ref, wm_ref, wq_ref, wk_ref, wvt_ref,
             pm_ref, q_ref, k_ref, vt_ref, gf_ref, gi_ref, xe_ref, peq_ref, pek_ref) = refs
        else:
            (x_ref, xp_ref, xnx_ref, sh_ref, sc_ref, g_ref, wg_ref, bg_ref, cw_ref, wk_ref, wvt_ref,
             k_ref, vt_ref, gf_ref, gi_ref, xe_ref, pek_ref) = refs
        i = pl.program_id(1)

        def normed(x):
            ms = jnp.mean(x * x, axis=-1, keepdims=True)
            y = x * lax.rsqrt(ms + EPS) * g_ref[...]
            return y * (1.0 + sc_ref[...]) + sh_ref[...]

        xn = normed(x_ref[...]).astype(BF16)
        xe_ref[HALO:HALO + tm, :] = xn
        xe_ref[0:HALO, :] = jnp.where(i > 0, normed(xp_ref[...]), 0.0).astype(BF16)
        xe_ref[HALO + tm:, :] = jnp.where(i < n_tiles - 1, normed(xnx_ref[...]), 0.0).astype(BF16)

        def conv_silu(pe_ref, w_ref, cw, scale, out_ref):
            pe_ref[...] = _dot(xe_ref[...], w_ref[...])
            pad = 8
            n_ext = CHUNK + 2 * pad
            for cc in range(n_cc):
                r = HALO + cc * CHUNK
                x_ext = pe_ref[r - pad:r + CHUNK + pad, :]
                prev = pltpu.roll(x_ext, 1, 0)[pad:pad + CHUNK, :]
                nxt = pltpu.roll(x_ext, n_ext - 1, 0)[pad:pad + CHUNK, :]
                y = prev * cw[0:1, :] + x_ext[pad:pad + CHUNK, :] * cw[1:2, :] + nxt * cw[2:3, :]
                y = _silu(y)
                if scale != 1.0:
                    y = y * scale
                out_ref[cc * CHUNK:(cc + 1) * CHUNK, :] = y.astype(BF16)

        cw = cw_ref[...]
        if latent:
            conv_silu(peq_ref, wq_ref, cw[:, :D_MODEL], 1.0, q_ref)
        conv_silu(pek_ref, wk_ref, cw[:, D_MODEL:], HEAD_DIM ** -0.5, k_ref)

        sub = i % tiles_per_gate_tile
        acc = _dot(xn[0:MCHUNK], wg_ref[sub * n_mc])
        for mc in range(1, n_mc):
            acc = acc + _dot(xn[mc * MCHUNK:(mc + 1) * MCHUNK], wg_ref[sub * n_mc + mc])
        acc = acc + bg_ref[...]
        group = lax.broadcasted_iota(jnp.int32, (1, LANES), 1) // SERIES
        own = (group >= sub * n_mc) & (group < (sub + 1) * n_mc)
        new_f = jnp.where(own, _log_sigmoid(acc[:, :LANES]), 0.0)
        new_i = jnp.where(own, acc[:, LANES:], 0.0)

        @pl.when(sub == 0)
        def _():
            gf_ref[...] = new_f
            gi_ref[...] = new_i

        @pl.when(sub != 0)
        def _():
            gf_ref[...] += new_f
            gi_ref[...] += new_i

        if latent:
            for s in range(N_MERGE_SEG):
                cols = slice(s * D_MODEL, (s + 1) * D_MODEL)
                pm_ref[:, cols] = _dot(xn, wm_ref[:, cols]).astype(BF16)
        vt = _dot_nt(wvt_ref[...], xn)
        for mc in range(n_mc):
            vt_ref[mc] = vt[:, mc * MCHUNK:(mc + 1) * MCHUNK].astype(BF16)

    return kernel


def _inproj(x, mod3, mod_row_of_batch, norm_g, wgs, bgs, conv_qk, weights, tm, latent):
    B, L, D = x.shape
    n_tiles = L // tm
    n_mc = tm // MCHUNK
    tiles_per_gate_tile = CHUNKS_PER_TILE // n_mc
    n_gate_tiles = -(-n_tiles // tiles_per_gate_tile)
    hb = tm // HALO
    n_hblk = L // HALO
    row = mod_row_of_batch
    tok = lambda width: pl.BlockSpec((None, tm, width), lambda b, i: (b, i, 0))
    tok_shape = lambda width: jax.ShapeDtypeStruct((B, L, width), BF16)
    widths = [w.shape[1] for w in weights[:-1]]
    gate_spec = pl.BlockSpec((None, MCHUNK, LANES), lambda b, i: (b, 0, i // tiles_per_gate_tile))
    gate_shape = jax.ShapeDtypeStruct((B, MCHUNK, n_gate_tiles * LANES), F32)
    vt_spec = pl.BlockSpec((None, n_mc, D, MCHUNK), lambda b, i: (b, i, 0, 0))
    vt_shape = jax.ShapeDtypeStruct((B, L // MCHUNK, D, MCHUNK), BF16)
    return pl.pallas_call(
        _make_inproj_kernel(tm, n_tiles, latent),
        grid=(B, n_tiles),
        in_specs=[
            pl.BlockSpec((None, tm, D), lambda b, i: (b, i, 0)),
            pl.BlockSpec((None, HALO, D), lambda b, i: (b, jnp.maximum(i * hb - 1, 0), 0)),
            pl.BlockSpec((None, HALO, D), lambda b, i: (b, jnp.minimum((i + 1) * hb, n_hblk - 1), 0)),
            pl.BlockSpec((None, 1, D), lambda b, i: (row(b), 0, 0)),
            pl.BlockSpec((None, 1, D), lambda b, i: (row(b), 0, 1)),
            _const_spec((1, D)), _const_spec(wgs.shape), _const_spec(bgs.shape), _const_spec(conv_qk.shape),
        ] + [_const_spec(w.shape) for w in weights],
        out_specs=[tok(w) for w in widths] + [vt_spec, gate_spec, gate_spec],
        out_shape=[tok_shape(w) for w in widths] + [vt_shape, gate_shape, gate_shape],
        scratch_shapes=[pltpu.VMEM((tm + 2 * HALO, D), BF16)]
        + [pltpu.VMEM((tm + 2 * HALO, D), F32)] * (2 if latent else 1),
        compiler_params=pltpu.CompilerParams(
            dimension_semantics=("parallel", "arbitrary"), vmem_limit_bytes=VMEM_LIMIT),
        name="inproj_latent" if latent else "inproj_context",
    )(x, x, x, mod3, mod3, norm_g, wgs, bgs, conv_qk, *weights)


def _cummax_rows(x, reverse):
    n = x.shape[0]
    rows = lax.broadcasted_iota(jnp.int32, x.shape, 0)
    k = 1
    while k < n:
        if reverse:
            shifted = jnp.where(rows < n - k, pltpu.roll(x, n - k, 0), NEG)
        else:
            shifted = jnp.where(rows >= k, pltpu.roll(x, k, 0), NEG)
        x = jnp.maximum(x, shifted)
        k *= 2
    return x


def _gate_prep(gf, gi, n_lat, n_ctx, cols_ref, rt_ref, sp_ref):
    n_ext = n_lat + n_ctx
    width = gf.shape[1]
    n_tile = width // LANES
    ii = lax.broadcasted_iota(jnp.int32, (MCHUNK, MCHUNK), 0)
    jj = lax.broadcasted_iota(jnp.int32, (MCHUNK, MCHUNK), 1)
    tril = jnp.where(ii >= jj, 1.0, 0.0).astype(BF16)
    triu = jnp.where(ii <= jj, 1.0, 0.0).astype(BF16)
    lane = lax.broadcasted_iota(jnp.int32, (1, width), 1)
    lane_bwd = (lane % 2) == 1
    lane_q = (lane // 2) % N_QUANT

    hi, lo = _split_hi_lo(gf)
    b = jnp.where(lane_bwd, _dot(triu, hi) + _dot(triu, lo), _dot(tril, hi) + _dot(tril, lo))
    tot = jnp.where(lane_bwd, b[0:1, :], b[MCHUNK - 1:MCHUNK, :])
    r = gi - b
    cm = jnp.where(lane_bwd, _cummax_rows(r, True), _cummax_rows(r, False))
    rmax = jnp.where(lane_bwd, cm[0:1, :], cm[MCHUNK - 1:MCHUNK, :])

    def to_rows(v):
        return [jnp.broadcast_to(v[:, t * LANES:(t + 1) * LANES], (LANES, LANES)).T for t in range(n_tile)]

    tot_t, rmax_t = to_rows(tot), to_rows(rmax)

    def slab(tiles, e):
        t, k = divmod(e, CHUNKS_PER_TILE)
        return tiles[t][k * SERIES:(k + 1) * SERIES, :]

    ctx_ids = list(range(n_lat, n_ext))
    orders = (ctx_ids + list(range(n_lat)), ctx_ids[::-1] + list(range(n_lat - 1, -1, -1)))
    m_prev, m_new = [{}, {}], [{}, {}]
    for d, order in enumerate(orders):
        m = jnp.zeros((SERIES, LANES), F32)
        for e in order:
            m_prev[d][e] = m
            tot_e = slab(tot_t, e)
            m = jnp.maximum(tot_e + m, tot_e + slab(rmax_t, e))
            m_new[d][e] = m
    row_bwd = (lax.broadcasted_iota(jnp.int32, (SERIES, LANES), 0) % 2) == 1
    mp_slabs = [jnp.where(row_bwd, m_prev[1][e], m_prev[0][e]) for e in range(n_ext)]
    mn_slabs = [jnp.where(row_bwd, m_new[1][e], m_new[0][e]) for e in range(n_ext)]
    sp_slabs = [jnp.exp(slab(tot_t, e) + mp_slabs[e] - mn_slabs[e]) for e in range(n_ext)]

    def tiles_of(slabs):
        pad = [jnp.zeros((SERIES, LANES), F32)] * (n_tile * CHUNKS_PER_TILE - n_ext)
        full = slabs + pad
        return [jnp.concatenate(full[t * CHUNKS_PER_TILE:(t + 1) * CHUNKS_PER_TILE], axis=0) for t in range(n_tile)]

    def to_lanes(slabs):
        return jnp.concatenate([t.T[0:1, :] for t in tiles_of(slabs)], axis=1)

    mp = to_lanes(mp_slabs)
    mn = to_lanes(mn_slabs)

    g = jnp.maximum(mp, cm)
    e_w = jnp.exp(tot + r - mn)
    s_inter = jnp.exp(mp - g)
    floor = jnp.exp(-(b + g))
    packed = jnp.where(lane_q == Q_G, g, jnp.where(lane_q == Q_E, e_w, jnp.where(lane_q == Q_S, s_inter, floor)))

    sp_tiles = tiles_of(sp_slabs)
    for t in range(n_tile):
        rt_ref[t * LANES:(t + 1) * LANES, :] = packed[:, t * LANES:(t + 1) * LANES].T
        sp_ref[t * LANES:(t + 1) * LANES, :] = sp_tiles[t]
    for e in range(n_ext):
        t, k = divmod(e, CHUNKS_PER_TILE)
        tile = r[:, t * LANES:(t + 1) * LANES]
        for h in range(N_HEADS):
            shift = k * SERIES + h * ROWS_PER_HEAD
            cols_ref[h, e] = tile if shift == 0 else pltpu.roll(tile, LANES - shift, 1)


def _mlstm_kernel(q_ref, k_ref, vt_ref, kx_ref, vtx_ref, gf_ref, gi_ref, gfx_ref, gix_ref,
                  o_ref, cols_ref, rt_ref, sp_ref, ct_ref, n_ref, cprev_ref, at_ref):
    n_lat = q_ref.shape[0] // MCHUNK
    n_ctx = kx_ref.shape[0] // MCHUNK
    h = pl.program_id(1)

    @pl.when(h == 0)
    def _():
        gf = jnp.concatenate([gf_ref[...], gfx_ref[...]], axis=1)
        gi = jnp.concatenate([gi_ref[...], gix_ref[...]], axis=1)
        _gate_prep(gf, gi, n_lat, n_ctx, cols_ref, rt_ref, sp_ref)

    def series_rows(ref, e):
        r0 = pl.multiple_of(e * SERIES + h * ROWS_PER_HEAD, ROWS_PER_HEAD)
        return ref[pl.ds(r0, ROWS_PER_HEAD), :]

    def row(rt, quantity, d):
        i = 2 * quantity + d
        return rt[i:i + 1, :]

    def state_step(d, e, k, vt, c_lat):
        r0 = pl.multiple_of(e * SERIES + h * ROWS_PER_HEAD, ROWS_PER_HEAD)
        rt_x = rt_ref[pl.ds(r0, EXTRA), :]
        s_row = series_rows(sp_ref, e)[d:d + 1, :]
        s2 = jnp.concatenate([s_row, s_row], axis=1)
        vet = (vt.astype(F32) * row(rt_x, Q_E, d)).astype(BF16)
        upd = _dot(jnp.concatenate([vet, rt_x.astype(BF16)], axis=0), k)
        i_e = HEAD_DIM + 2 * Q_E + d
        ek = upd[i_e:i_e + 1, :]
        ct_old = ct_ref[d]
        n_old = n_ref[d]
        if c_lat is not None:
            cprev_ref[d, c_lat, 0:HEAD_DIM, :] = ct_old.astype(BF16)
            cprev_ref[d, c_lat, HEAD_DIM:, :] = jnp.broadcast_to(n_old, (EXTRA, HEAD_DIM)).astype(BF16)
        ct_ref[d] = s2 * ct_old + upd[0:HEAD_DIM, :]
        n_ref[d] = s2 * n_old + ek

    ct_ref[...] = jnp.zeros_like(ct_ref)
    n_ref[...] = jnp.zeros_like(n_ref)
    for t in range(n_ctx):
        for d in range(2):
            cx = (n_ctx - 1 - t) if d else t
            rows = slice(cx * MCHUNK, (cx + 1) * MCHUNK)
            state_step(d, n_lat + cx, kx_ref[rows, :], vtx_ref[cx], None)

    def state_body(t, carry):
        for d in range(2):
            c = (n_lat - 1 - t) if d else t
            rows = pl.ds(pl.multiple_of(c * MCHUNK, MCHUNK), MCHUNK)
            state_step(d, c, k_ref[rows, :], vt_ref[c], c)
        return carry

    lax.fori_loop(0, n_lat, state_body, 0, unroll=True)

    jj = lax.broadcasted_iota(jnp.int32, (MCHUNK, MCHUNK), 0)
    ii = lax.broadcasted_iota(jnp.int32, (MCHUNK, MCHUNK), 1)
    visible = (jj <= ii, jj >= ii)

    ones_rows = jnp.ones((EXTRA, MCHUNK), BF16)

    def weights_stage(c, slot):
        rows = pl.ds(pl.multiple_of(c * MCHUNK, MCHUNK), MCHUNK)
        cols = cols_ref[h, c]
        rt = series_rows(rt_ref, c)
        st = _dot_nt(k_ref[rows, :], q_ref[rows, :])
        for d in range(2):
            w = jnp.exp(jnp.where(visible[d], cols[:, d:d + 1] - row(rt, Q_G, d), NEG))
            at_ref[slot, :, d * MCHUNK:(d + 1) * MCHUNK] = (w * st).astype(BF16)

    def readout_stage(c, slot):
        rows = pl.ds(pl.multiple_of(c * MCHUNK, MCHUNK), MCHUNK)
        q = q_ref[rows, :]
        rt = series_rows(rt_ref, c)
        at = at_ref[slot]
        num = _dot(jnp.concatenate([vt_ref[c], ones_rows], axis=0), at)
        out = None
        for d in range(2):
            lanes = slice(d * MCHUNK, (d + 1) * MCHUNK)
            s_inter = row(rt, Q_S, d)
            inter = _dot_nt(cprev_ref[d, c], q)
            den = num[HEAD_DIM:HEAD_DIM + 1, lanes] + s_inter * inter[HEAD_DIM:HEAD_DIM + 1, :]
            inv = 1.0 / jnp.maximum(jnp.abs(den), row(rt, Q_FL, d))
            hd = (num[0:HEAD_DIM, lanes] + s_inter * inter[0:HEAD_DIM, :]) * inv
            out = hd if out is None else out + hd
        o_ref[c] = out.astype(o_ref.dtype)

    weights_stage(0, 0)

    def out_body(c, carry):
        readout_stage(c, c % 2)
        weights_stage(c + 1, (c + 1) % 2)
        return carry

    lax.fori_loop(0, n_lat - 1, out_body, 0, unroll=True)
    readout_stage(n_lat - 1, (n_lat - 1) % 2)


def _mlstm(q, k, vt, kx, vtx, gf, gi, gfx, gix):
    B, L, _ = q.shape
    Lx = kx.shape[1]
    n_lat, n_ctx = L // MCHUNK, Lx // MCHUNK
    n_ext = n_lat + n_ctx
    n_tile = gf.shape[2] // LANES + gfx.shape[2] // LANES
    head_cols = lambda n: pl.BlockSpec((None, n, HEAD_DIM), lambda b, h: (b, 0, h))
    head_rows = lambda n: pl.BlockSpec((None, n, HEAD_DIM, MCHUNK), lambda b, h: (b, 0, h, 0))
    gate_spec = lambda a: pl.BlockSpec((None,) + a.shape[1:], lambda b, h: (b, 0, 0))
    return pl.pallas_call(
        _mlstm_kernel,
        grid=(B, N_HEADS),
        in_specs=[
            head_cols(L), head_cols(L), head_rows(n_lat), head_cols(Lx), head_rows(n_ctx),
            gate_spec(gf), gate_spec(gi), gate_spec(gfx), gate_spec(gix),
        ],
        out_specs=head_rows(n_lat),
        out_shape=jax.ShapeDtypeStruct((B, n_lat, D_MODEL, MCHUNK), BF16),
        scratch_shapes=[
            pltpu.VMEM((N_HEADS, n_ext, MCHUNK, LANES), F32),
            pltpu.VMEM((n_tile * LANES, MCHUNK), F32),
            pltpu.VMEM((n_tile * LANES, LANES), F32),
            pltpu.VMEM((2, HEAD_DIM, HEAD_DIM), F32),
            pltpu.VMEM((2, 1, HEAD_DIM), F32),
            pltpu.VMEM((2, n_lat, HEAD_DIM + EXTRA, HEAD_DIM), BF16),
            pltpu.VMEM((2, MCHUNK, 2 * MCHUNK), BF16),
        ],
        compiler_params=pltpu.CompilerParams(
            dimension_semantics=("parallel", "arbitrary"), vmem_limit_bytes=VMEM_LIMIT),
        name="mlstm",
    )(q, k, vt, kx, vtx, gf, gi, gfx, gix)


def _rms(x, g):
    ms = jnp.mean(x * x, axis=-1, keepdims=True)
    return x * lax.rsqrt(ms + EPS) * g


def _merge_mlp_kernel(x_ref, p_ref, ht_ref, g1_ref, sh2_ref, sc2_ref, g2_ref,
                      n2_ref, nf_ref, gmh_ref, gsgu_ref, wa_ref, wb_ref, wo_ref, ws_ref, bst_ref,
                      w1_ref, w2_ref, o_ref, t_ref):
    tm = x_ref.shape[0]
    D = D_MODEL
    for sub in range(tm // MERGE_SUB):
        r_sub = slice(sub * MERGE_SUB, (sub + 1) * MERGE_SUB)
        seg = lambda s: p_ref[r_sub, s * D:(s + 1) * D].astype(F32)

        mcs = range(sub * (MERGE_SUB // MCHUNK), (sub + 1) * (MERGE_SUB // MCHUNK))
        h_m = jnp.concatenate([ht_ref[mc].astype(F32).T for mc in mcs], axis=0)
        hm = jax.nn.sigmoid(seg(0)) * h_m
        gmh = gmh_ref[...]
        parts = []
        for h in range(N_HEADS):
            cols = slice(h * HEAD_DIM, (h + 1) * HEAD_DIM)
            parts.append(_rms(hm[:, cols], gmh[:, cols]).astype(BF16))
        ya = _dot(jnp.concatenate(parts, axis=1), wa_ref[...])

        u = jax.nn.gelu(seg(1))
        vg = jax.nn.gelu(seg(2))
        vn = _rms(vg, gsgu_ref[...]).astype(BF16)
        bst = bst_ref[...]
        for cc in range(MERGE_SUB // CHUNK):
            rows = slice(cc * CHUNK, (cc + 1) * CHUNK)
            t_rows = slice(sub * MERGE_SUB + cc * CHUNK, sub * MERGE_SUB + (cc + 1) * CHUNK)
            for g in range(N_GROUPS):
                cols = slice(g * GROUP_DIM, (g + 1) * GROUP_DIM)
                s = _dot(ws_ref[g], vn[rows, cols]) + bst[:, g:g + 1]
                t_ref[t_rows, cols] = (u[rows, cols] * s).astype(BF16)
        yb = _dot(t_ref[r_sub, :], wb_ref[...])

        y = jax.nn.sigmoid(seg(3)) * ya + jax.nn.sigmoid(seg(4)) * yb
        mix = _dot(y.astype(BF16), wo_ref[...])
        x1 = x_ref[r_sub, :] + g1_ref[...] * mix

        xn2 = (_rms(x1, n2_ref[...]) * (1.0 + sc2_ref[...]) + sh2_ref[...]).astype(BF16)
        ff = D_FF // 4
        acc = jnp.zeros((MERGE_SUB, D), F32)
        for kk in range(4):
            hmid = jnp.maximum(_dot(xn2, w1_ref[:, kk * ff:(kk + 1) * ff]), 0.0)
            acc = acc + _dot((hmid * hmid).astype(BF16), w2_ref[kk * ff:(kk + 1) * ff, :])
        x2 = x1 + g2_ref[...] * acc
        o_ref[r_sub, :] = _rms(x2, nf_ref[...])


def _merge_mlp(x, pm, ht, mod3, norm2, norm_f, g_mh, g_sgu, w_a, w_b, w_out, w_s, b_st, w1, w2, tm):
    B, L, D = x.shape
    mod_spec = lambda k: pl.BlockSpec((None, 1, D), lambda b, i: (b, 0, k))
    return pl.pallas_call(
        _merge_mlp_kernel,
        grid=(B, L // tm),
        in_specs=[
            pl.BlockSpec((None, tm, D), lambda b, i: (b, i, 0)),
            pl.BlockSpec((None, tm, N_MERGE_SEG * D), lambda b, i: (b, i, 0)),
            pl.BlockSpec((None, tm // MCHUNK, D, MCHUNK), lambda b, i: (b, i, 0, 0)),
            mod_spec(2), mod_spec(3), mod_spec(4), mod_spec(5),
            _const_spec((1, D)), _const_spec((1, D)), _const_spec((1, D)), _const_spec((1, D)),
            _const_spec((D, D)), _const_spec((D, D)), _const_spec((D, D)),
            _const_spec(w_s.shape), _const_spec(b_st.shape),
            _const_spec((D, D_FF)), _const_spec((D_FF, D)),
        ],
        out_specs=pl.BlockSpec((None, tm, D), lambda b, i: (b, i, 0)),
        out_shape=jax.ShapeDtypeStruct((B, L, D), F32),
        scratch_shapes=[pltpu.VMEM((tm, D), BF16)],
        compiler_params=pltpu.CompilerParams(
            dimension_semantics=("parallel", "parallel"), vmem_limit_bytes=VMEM_LIMIT),
        name="merge_mlp",
    )(x, pm, ht, mod3, mod3, mod3, mod3, norm2, norm_f, g_mh, g_sgu, w_a, w_b, w_out, w_s, b_st, w1, w2)


def _gate_weights(w_gate, b_gate):
    D = w_gate.shape[0]

    def series(a, kind_of_dir):
        a4 = a.reshape(a.shape[:-1] + (4, N_HEADS))
        per_dir = jnp.stack([a4[..., kind_of_dir[0], :], a4[..., kind_of_dir[1], :]], axis=-1)
        dup = jnp.broadcast_to(per_dir[..., :, None, :], per_dir.shape[:-1] + (N_QUANT, 2))
        return dup.reshape(a.shape[:-1] + (SERIES,))

    wf, wi = series(w_gate, (1, 3)), series(w_gate, (0, 2))
    bf, bi = series(b_gate, (1, 3)), series(b_gate, (0, 2))
    eye = jnp.eye(CHUNKS_PER_TILE, dtype=w_gate.dtype)
    place = lambda w: jnp.einsum("ab,dk->adbk", eye, w).reshape(CHUNKS_PER_TILE, D, LANES)
    wgs = jnp.concatenate([place(wf), place(wi)], axis=2).astype(BF16)
    bgs = jnp.concatenate([jnp.tile(bf, CHUNKS_PER_TILE), jnp.tile(bi, CHUNKS_PER_TILE)])[None, :]
    return wgs, bgs


def kernel(x, c, ctx, c_ctx, norm1, norm2, w_mod, b_mod, w_in, conv_qk, b_gate, g_mh, w_a, w_s, b_s,
           g_sgu, w_b, w_out, w1, w2, norm_f):
    B, S, D = x.shape
    Lx = ctx.shape[1]
    tm = 2 * MCHUNK
    assert D == D_MODEL and S % (MCHUNK * CHUNKS_PER_TILE) == 0 and Lx % MCHUNK == 0
    assert Lx // MCHUNK <= CHUNKS_PER_TILE
    assert w_mod.shape[0] == 1, "single-layer block"
    W = D_MODEL
    off_g = 3 * W
    off_o = off_g + 4 * N_HEADS

    mod_rows = ((B + 1 + 7) // 8) * 8
    cc = jnp.zeros((mod_rows, D), F32).at[:B].set(c).at[B].set(c_ctx)
    mod = _modulation(cc, w_mod, b_mod)
    mod3 = mod.reshape(mod_rows, 1, 6 * D)

    wi = w_in[0]
    w_merge = wi[:, off_o:].astype(BF16)
    wq = wi[:, :W].astype(BF16)
    wk = wi[:, W:2 * W].astype(BF16)
    wvt = wi[:, 2 * W:off_g].T.astype(BF16)
    wgs, bgs = _gate_weights(wi[:, off_g:off_o], b_gate[0])

    n1 = norm1[0][None, :]
    conv = conv_qk[0]
    pm, q, k, vt, gf, gi = _inproj(x, mod3, lambda b: b, n1, wgs, bgs, conv, (w_merge, wq, wk, wvt), tm, True)
    kx, vtx, gfx, gix = _inproj(ctx, mod3, lambda b: B, n1, wgs, bgs, conv, (wk, wvt), Lx, False)

    ht = _mlstm(q, k, vt, kx, vtx, gf, gi, gfx, gix)

    return _merge_mlp(
        x, pm, ht, mod3, norm2[0][None, :], norm_f[None, :], g_mh[0][None, :], g_sgu[0][None, :],
        w_a[0].astype(BF16), w_b[0].astype(BF16), w_out[0].astype(BF16),
        w_s[0].astype(BF16), b_s[0].T, w1[0].astype(BF16), w2[0].astype(BF16), 512)
```

```python
import jax
import jax.numpy as jnp
from jax import lax
from jax.experimental import pallas as pl
from jax.experimental.pallas import tpu as pltpu

D_MODEL = 1024
N_HEADS = 4
HEAD_DIM = D_MODEL // N_HEADS
CHUNK = 128
MCHUNK = 256
GROUP_DIM = 128
N_GROUPS = D_MODEL // GROUP_DIM
D_FF = 4 * D_MODEL
CONV_W = 3
NEG = -1e30
EPS = 1e-6
LANES = 128
HALO = 16
EXTRA = 16

N_QUANT = 4
ROWS_PER_HEAD = 2 * N_QUANT
SERIES = N_HEADS * ROWS_PER_HEAD
CHUNKS_PER_TILE = LANES // SERIES
Q_G, Q_E, Q_S, Q_FL = 0, 1, 2, 3

N_MERGE_SEG = 5

VMEM_LIMIT = 60000 * 1024

F32 = jnp.float32
BF16 = jnp.bfloat16


def _dot(a, b):
    return jnp.dot(a, b, preferred_element_type=F32)


def _dot_nt(a, b):
    return lax.dot_general(a, b, (((1,), (1,)), ((), ())), preferred_element_type=F32)


def _split_hi_lo(x):
    hi = x.astype(BF16)
    lo = (x - hi.astype(F32)).astype(BF16)
    return hi, lo


def _silu(x):
    return x * jax.nn.sigmoid(x)


def _const_spec(shape):
    nd = len(shape)
    return pl.BlockSpec(shape, lambda *_: (0,) * nd, pipeline_mode=pl.Buffered(1))


def _mod_kernel(c_ref, w_ref, b_ref, o_ref):
    s = _silu(c_ref[...])
    s_hi, s_lo = _split_hi_lo(s)
    w_hi, w_lo = _split_hi_lo(w_ref[...])
    acc = _dot(s_hi, w_hi) + _dot(s_hi, w_lo) + _dot(s_lo, w_hi)
    o_ref[...] = acc + b_ref[...]


def _modulation(cc, w_mod, b_mod):
    rows, d = cc.shape
    n = w_mod.shape[2]
    tn = 1536
    return pl.pallas_call(
        _mod_kernel,
        grid=(n // tn,),
        in_specs=[
            pl.BlockSpec((rows, d), lambda j: (0, 0)),
            pl.BlockSpec((None, d, tn), lambda j: (0, 0, j)),
            pl.BlockSpec((1, tn), lambda j: (0, j)),
        ],
        out_specs=pl.BlockSpec((rows, tn), lambda j: (0, j)),
        out_shape=jax.ShapeDtypeStruct((rows, n), F32),
        compiler_params=pltpu.CompilerParams(
            dimension_semantics=("arbitrary",), vmem_limit_bytes=VMEM_LIMIT),
        name="modulation",
    )(cc, w_mod, b_mod)


def _log_sigmoid(x):
    return jnp.minimum(x, 0.0) - jnp.log1p(jnp.exp(-jnp.abs(x)))


def _make_inproj_kernel(tm, n_tiles, latent):
    n_cc = tm // CHUNK
    n_mc = tm // MCHUNK
    tiles_per_gate_tile = CHUNKS_PER_TILE // n_mc

    def kernel(*refs):
        if latent:
            (x_ref, xp_ref, xnx_ref, sh_ref, sc_ref, g_ref, wg_ref, bg_ref, cw_ref, wm_ref, wq_ref, wk_ref, wvt_ref,
             pm_ref, q_ref, k_ref, vt_ref, gf_ref, gi_ref, xe_ref, peq_ref, pek_ref) = refs
        else:
            (x_ref, xp_ref, xnx_ref, sh_ref, sc_ref, g_ref, wg_ref, bg_ref, cw_ref, wk_ref, wvt_ref,
             k_ref, vt_ref, gf_ref, gi_ref, xe_ref, pek_ref) = refs
        i = pl.program_id(1)

        def normed(x):
            ms = jnp.mean(x * x, axis=-1, keepdims=True)
            y = x * lax.rsqrt(ms + EPS) * g_ref[...]
            return y * (1.0 + sc_ref[...]) + sh_ref[...]

        xn = normed(x_ref[...]).astype(BF16)
        xe_ref[HALO:HALO + tm, :] = xn
        xe_ref[0:HALO, :] = jnp.where(i > 0, normed(xp_ref[...]), 0.0).astype(BF16)
        xe_ref[HALO + tm:, :] = jnp.where(i < n_tiles - 1, normed(xnx_ref[...]), 0.0).astype(BF16)

        pad = 8
        n_ext = CHUNK + 2 * pad

        def conv_silu(pe_ref, cw, scale, out_ref, cc):
            r = HALO + cc * CHUNK
            x_ext = pe_ref[r - pad:r + CHUNK + pad, :]
            prev = pltpu.roll(x_ext, 1, 0)[pad:pad + CHUNK, :]
            nxt = pltpu.roll(x_ext, n_ext - 1, 0)[pad:pad + CHUNK, :]
            y = prev * cw[0:1, :] + x_ext[pad:pad + CHUNK, :] * cw[1:2, :] + nxt * cw[2:3, :]
            y = _silu(y)
            if scale != 1.0:
                y = y * scale
            out_ref[cc * CHUNK:(cc + 1) * CHUNK, :] = y.astype(BF16)

        def merge_seg(s):
            cols = slice(s * D_MODEL, (s + 1) * D_MODEL)
            pm_ref[:, cols] = _dot(xn, wm_ref[:, cols]).astype(BF16)

        cw = cw_ref[...]
        pek_ref[...] = _dot(xe_ref[...], wk_ref[...])
        epilogues = [(pek_ref, cw[:, D_MODEL:], HEAD_DIM ** -0.5, k_ref, cc) for cc in range(n_cc)]
        if latent:
            peq_ref[...] = _dot(xe_ref[...], wq_ref[...])
            epilogues += [(peq_ref, cw[:, :D_MODEL], 1.0, q_ref, cc) for cc in range(n_cc)]
            per_seg = -(-len(epilogues) // (N_MERGE_SEG - 1))
            for s in range(N_MERGE_SEG):
                merge_seg(s)
                for args in epilogues[s * per_seg:(s + 1) * per_seg]:
                    conv_silu(*args)
        else:
            for args in epilogues:
                conv_silu(*args)

        sub = i % tiles_per_gate_tile
        acc = _dot(xn[0:MCHUNK], wg_ref[sub * n_mc])
        for mc in range(1, n_mc):
            acc = acc + _dot(xn[mc * MCHUNK:(mc + 1) * MCHUNK], wg_ref[sub * n_mc + mc])
        acc = acc + bg_ref[...]
        group = lax.broadcasted_iota(jnp.int32, (1, LANES), 1) // SERIES
        own = (group >= sub * n_mc) & (group < (sub + 1) * n_mc)
        new_f = jnp.where(own, _log_sigmoid(acc[:, :LANES]), 0.0)
        new_i = jnp.where(own, acc[:, LANES:], 0.0)

        vt = _dot_nt(wvt_ref[...], xn)
        for mc in range(n_mc):
            vt_ref[mc] = vt[:, mc * MCHUNK:(mc + 1) * MCHUNK].astype(BF16)

        @pl.when(sub == 0)
        def _():
            gf_ref[...] = new_f
            gi_ref[...] = new_i

        @pl.when(sub != 0)
        def _():
            gf_ref[...] += new_f
            gi_ref[...] += new_i

    return kernel


def _inproj(x, mod3, mod_row_of_batch, norm_g, wgs, bgs, conv_qk, weights, tm, latent):
    B, L, D = x.shape
    n_tiles = L // tm
    n_mc = tm // MCHUNK
    tiles_per_gate_tile = CHUNKS_PER_TILE // n_mc
    n_gate_tiles = -(-n_tiles // tiles_per_gate_tile)
    hb = tm // HALO
    n_hblk = L // HALO
    row = mod_row_of_batch
    tok = lambda width: pl.BlockSpec((None, tm, width), lambda b, i: (b, i, 0))
    tok_shape = lambda width: jax.ShapeDtypeStruct((B, L, width), BF16)
    widths = [w.shape[1] for w in weights[:-1]]
    gate_spec = pl.BlockSpec((None, MCHUNK, LANES), lambda b, i: (b, 0, i // tiles_per_gate_tile))
    gate_shape = jax.ShapeDtypeStruct((B, MCHUNK, n_gate_tiles * LANES), F32)
    vt_spec = pl.BlockSpec((None, n_mc, D, MCHUNK), lambda b, i: (b, i, 0, 0))
    vt_shape = jax.ShapeDtypeStruct((B, L // MCHUNK, D, MCHUNK), BF16)
    return pl.pallas_call(
        _make_inproj_kernel(tm, n_tiles, latent),
        grid=(B, n_tiles),
        in_specs=[
            pl.BlockSpec((None, tm, D), lambda b, i: (b, i, 0)),
            pl.BlockSpec((None, HALO, D), lambda b, i: (b, jnp.maximum(i * hb - 1, 0), 0)),
            pl.BlockSpec((None, HALO, D), lambda b, i: (b, jnp.minimum((i + 1) * hb, n_hblk - 1), 0)),
            pl.BlockSpec((None, 1, D), lambda b, i: (row(b), 0, 0)),
            pl.BlockSpec((None, 1, D), lambda b, i: (row(b), 0, 1)),
            _const_spec((1, D)), _const_spec(wgs.shape), _const_spec(bgs.shape), _const_spec(conv_qk.shape),
        ] + [_const_spec(w.shape) for w in weights],
        out_specs=[tok(w) for w in widths] + [vt_spec, gate_spec, gate_spec],
        out_shape=[tok_shape(w) for w in widths] + [vt_shape, gate_shape, gate_shape],
        scratch_shapes=[pltpu.VMEM((tm + 2 * HALO, D), BF16)]
        + [pltpu.VMEM((tm + 2 * HALO, D), F32)] * (2 if latent else 1),
        compiler_params=pltpu.CompilerParams(
            dimension_semantics=("parallel", "arbitrary"), vmem_limit_bytes=VMEM_LIMIT),
        name="inproj_latent" if latent else "inproj_context",
    )(x, x, x, mod3, mod3, norm_g, wgs, bgs, conv_qk, *weights)


def _cummax_rows(x, reverse):
    n = x.shape[0]
    rows = lax.broadcasted_iota(jnp.int32, x.shape, 0)
    k = 1
    while k < n:
        if reverse:
            shifted = jnp.where(rows < n - k, pltpu.roll(x, n - k, 0), NEG)
        else:
            shifted = jnp.where(rows >= k, pltpu.roll(x, k, 0), NEG)
        x = jnp.maximum(x, shifted)
        k *= 2
    return x


def _gate_prep(gf, gi, n_lat, n_ctx, cols_ref, rt_ref, sp_ref):
    n_ext = n_lat + n_ctx
    width = gf.shape[1]
    n_tile = width // LANES
    ii = lax.broadcasted_iota(jnp.int32, (MCHUNK, MCHUNK), 0)
    jj = lax.broadcasted_iota(jnp.int32, (MCHUNK, MCHUNK), 1)
    tril = jnp.where(ii >= jj, 1.0, 0.0).astype(BF16)
    triu = jnp.where(ii <= jj, 1.0, 0.0).astype(BF16)
    lane = lax.broadcasted_iota(jnp.int32, (1, width), 1)
    lane_bwd = (lane % 2) == 1
    lane_q = (lane // 2) % N_QUANT

    hi, lo = _split_hi_lo(gf)
    b = jnp.where(lane_bwd, _dot(triu, hi) + _dot(triu, lo), _dot(tril, hi) + _dot(tril, lo))
    tot = jnp.where(lane_bwd, b[0:1, :], b[MCHUNK - 1:MCHUNK, :])
    r = gi - b
    cm = jnp.where(lane_bwd, _cummax_rows(r, True), _cummax_rows(r, False))
    rmax = jnp.where(lane_bwd, cm[0:1, :], cm[MCHUNK - 1:MCHUNK, :])

    def to_rows(v):
        return [jnp.broadcast_to(v[:, t * LANES:(t + 1) * LANES], (LANES, LANES)).T for t in range(n_tile)]

    tot_t, rmax_t = to_rows(tot), to_rows(rmax)

    def slab(tiles, e):
        t, k = divmod(e, CHUNKS_PER_TILE)
        return tiles[t][k * SERIES:(k + 1) * SERIES, :]

    ctx_ids = list(range(n_lat, n_ext))
    orders = (ctx_ids + list(range(n_lat)), ctx_ids[::-1] + list(range(n_lat - 1, -1, -1)))
    m_prev, m_new = [{}, {}], [{}, {}]
    for d, order in enumerate(orders):
        m = jnp.zeros((SERIES, LANES), F32)
        for e in order:
            m_prev[d][e] = m
            tot_e = slab(tot_t, e)
            m = jnp.maximum(tot_e + m, tot_e + slab(rmax_t, e))
            m_new[d][e] = m
    row_bwd = (lax.broadcasted_iota(jnp.int32, (SERIES, LANES), 0) % 2) == 1
    mp_slabs = [jnp.where(row_bwd, m_prev[1][e], m_prev[0][e]) for e in range(n_ext)]
    mn_slabs = [jnp.where(row_bwd, m_new[1][e], m_new[0][e]) for e in range(n_ext)]
    sp_slabs = [jnp.exp(slab(tot_t, e) + mp_slabs[e] - mn_slabs[e]) for e in range(n_ext)]

    def tiles_of(slabs):
        pad = [jnp.zeros((SERIES, LANES), F32)] * (n_tile * CHUNKS_PER_TILE - n_ext)
        full = slabs + pad
        return [jnp.concatenate(full[t * CHUNKS_PER_TILE:(t + 1) * CHUNKS_PER_TILE], axis=0) for t in range(n_tile)]

    def to_lanes(slabs):
        return jnp.concatenate([t.T[0:1, :] for t in tiles_of(slabs)], axis=1)

    mp = to_lanes(mp_slabs)
    mn = to_lanes(mn_slabs)

    g = jnp.maximum(mp, cm)
    e_w = jnp.exp(tot + r - mn)
    s_inter = jnp.exp(mp - g)
    floor = jnp.exp(-(b + g))
    packed = jnp.where(lane_q == Q_G, g, jnp.where(lane_q == Q_E, e_w, jnp.where(lane_q == Q_S, s_inter, floor)))

    sp_tiles = tiles_of(sp_slabs)
    for t in range(n_tile):
        rt_ref[t * LANES:(t + 1) * LANES, :] = packed[:, t * LANES:(t + 1) * LANES].T
        sp_ref[t * LANES:(t + 1) * LANES, :] = sp_tiles[t]
    for e in range(n_ext):
        t, k = divmod(e, CHUNKS_PER_TILE)
        tile = r[:, t * LANES:(t + 1) * LANES]
        for h in range(N_HEADS):
            shift = k * SERIES + h * ROWS_PER_HEAD
            cols_ref[h, e] = tile if shift == 0 else pltpu.roll(tile, LANES - shift, 1)


def _mlstm_kernel(q_ref, k_ref, vt_ref, kx_ref, vtx_ref, gf_ref, gi_ref, gfx_ref, gix_ref,
                  o_ref, cols_ref, rt_ref, sp_ref, ct_ref, n_ref, cprev_ref, at_ref):
    n_lat = q_ref.shape[0] // MCHUNK
    n_ctx = kx_ref.shape[0] // MCHUNK
    h = pl.program_id(1)

    @pl.when(h == 0)
    def _():
        gf = jnp.concatenate([gf_ref[...], gfx_ref[...]], axis=1)
        gi = jnp.concatenate([gi_ref[...], gix_ref[...]], axis=1)
        _gate_prep(gf, gi, n_lat, n_ctx, cols_ref, rt_ref, sp_ref)

    def series_rows(ref, e):
        r0 = pl.multiple_of(e * SERIES + h * ROWS_PER_HEAD, ROWS_PER_HEAD)
        return ref[pl.ds(r0, ROWS_PER_HEAD), :]

    def row(rt, quantity, d):
        i = 2 * quantity + d
        return rt[i:i + 1, :]

    def state_step(d, e, k, vt, c_lat):
        r0 = pl.multiple_of(e * SERIES + h * ROWS_PER_HEAD, ROWS_PER_HEAD)
        rt_x = rt_ref[pl.ds(r0, EXTRA), :]
        s_row = series_rows(sp_ref, e)[d:d + 1, :]
        s2 = jnp.concatenate([s_row, s_row], axis=1)
        vet = (vt.astype(F32) * row(rt_x, Q_E, d)).astype(BF16)
        upd = _dot(jnp.concatenate([vet, rt_x.astype(BF16)], axis=0), k)
        i_e = HEAD_DIM + 2 * Q_E + d
        ek = upd[i_e:i_e + 1, :]
        ct_old = ct_ref[d]
        n_old = n_ref[d]
        if c_lat is not None:
            cprev_ref[d, c_lat, 0:HEAD_DIM, :] = ct_old.astype(BF16)
            cprev_ref[d, c_lat, HEAD_DIM:, :] = jnp.broadcast_to(n_old, (EXTRA, HEAD_DIM)).astype(BF16)
        ct_ref[d] = s2 * ct_old + upd[0:HEAD_DIM, :]
        n_ref[d] = s2 * n_old + ek

    ct_ref[...] = jnp.zeros_like(ct_ref)
    n_ref[...] = jnp.zeros_like(n_ref)
    for t in range(n_ctx):
        for d in range(2):
            cx = (n_ctx - 1 - t) if d else t
            rows = slice(cx * MCHUNK, (cx + 1) * MCHUNK)
            state_step(d, n_lat + cx, kx_ref[rows, :], vtx_ref[cx], None)

    def state_body(t, carry):
        for d in range(2):
            c = (n_lat - 1 - t) if d else t
            rows = pl.ds(pl.multiple_of(c * MCHUNK, MCHUNK), MCHUNK)
            state_step(d, c, k_ref[rows, :], vt_ref[c], c)
        return carry

    lax.fori_loop(0, n_lat, state_body, 0, unroll=True)

    jj = lax.broadcasted_iota(jnp.int32, (MCHUNK, MCHUNK), 0)
    ii = lax.broadcasted_iota(jnp.int32, (MCHUNK, MCHUNK), 1)
    visible = (jj <= ii, jj >= ii)

    ones_rows = jnp.ones((EXTRA, MCHUNK), BF16)

    def weights_stage(c, slot):
        rows = pl.ds(pl.multiple_of(c * MCHUNK, MCHUNK), MCHUNK)
        cols = cols_ref[h, c]
        rt = series_rows(rt_ref, c)
        st = _dot_nt(k_ref[rows, :], q_ref[rows, :])
        for d in range(2):
            w = jnp.exp(jnp.where(visible[d], cols[:, d:d + 1] - row(rt, Q_G, d), NEG))
            at_ref[slot, :, d * MCHUNK:(d + 1) * MCHUNK] = (w * st).astype(BF16)

    def readout_stage(c, slot):
        rows = pl.ds(pl.multiple_of(c * MCHUNK, MCHUNK), MCHUNK)
        q = q_ref[rows, :]
        rt = series_rows(rt_ref, c)
        at = at_ref[slot]
        num = _dot(jnp.concatenate([vt_ref[c], ones_rows], axis=0), at)
        out = None
        for d in range(2):
            lanes = slice(d * MCHUNK, (d + 1) * MCHUNK)
            s_inter = row(rt, Q_S, d)
            inter = _dot_nt(cprev_ref[d, c], q)
            den = num[HEAD_DIM:HEAD_DIM + 1, lanes] + s_inter * inter[HEAD_DIM:HEAD_DIM + 1, :]
            inv = 1.0 / jnp.maximum(jnp.abs(den), row(rt, Q_FL, d))
            hd = (num[0:HEAD_DIM, lanes] + s_inter * inter[0:HEAD_DIM, :]) * inv
            out = hd if out is None else out + hd
        o_ref[c] = out.astype(o_ref.dtype)

    weights_stage(0, 0)

    def out_body(c, carry):
        readout_stage(c, c % 2)
        weights_stage(c + 1, (c + 1) % 2)
        return carry

    lax.fori_loop(0, n_lat - 1, out_body, 0, unroll=True)
    readout_stage(n_lat - 1, (n_lat - 1) % 2)


def _mlstm(q, k, vt, kx, vtx, gf, gi, gfx, gix):
    B, L, _ = q.shape
    Lx = kx.shape[1]
    n_lat, n_ctx = L // MCHUNK, Lx // MCHUNK
    n_ext = n_lat + n_ctx
    n_tile = gf.shape[2] // LANES + gfx.shape[2] // LANES
    head_cols = lambda n: pl.BlockSpec((None, n, HEAD_DIM), lambda b, h: (b, 0, h))
    head_rows = lambda n: pl.BlockSpec((None, n, HEAD_DIM, MCHUNK), lambda b, h: (b, 0, h, 0))
    gate_spec = lambda a: pl.BlockSpec((None,) + a.shape[1:], lambda b, h: (b, 0, 0))
    return pl.pallas_call(
        _mlstm_kernel,
        grid=(B, N_HEADS),
        in_specs=[
            head_cols(L), head_cols(L), head_rows(n_lat), head_cols(Lx), head_rows(n_ctx),
            gate_spec(gf), gate_spec(gi), gate_spec(gfx), gate_spec(gix),
        ],
        out_specs=head_rows(n_lat),
        out_shape=jax.ShapeDtypeStruct((B, n_lat, D_MODEL, MCHUNK), BF16),
        scratch_shapes=[
            pltpu.VMEM((N_HEADS, n_ext, MCHUNK, LANES), F32),
            pltpu.VMEM((n_tile * LANES, MCHUNK), F32),
            pltpu.VMEM((n_tile * LANES, LANES), F32),
            pltpu.VMEM((2, HEAD_DIM, HEAD_DIM), F32),
            pltpu.VMEM((2, 1, HEAD_DIM), F32),
            pltpu.VMEM((2, n_lat, HEAD_DIM + EXTRA, HEAD_DIM), BF16),
            pltpu.VMEM((2, MCHUNK, 2 * MCHUNK), BF16),
        ],
        compiler_params=pltpu.CompilerParams(
            dimension_semantics=("parallel", "arbitrary"), vmem_limit_bytes=VMEM_LIMIT),
        name="mlstm",
    )(q, k, vt, kx, vtx, gf, gi, gfx, gix)


def _rms(x, g):
    ms = jnp.mean(x * x, axis=-1, keepdims=True)
    return x * lax.rsqrt(ms + EPS) * g


def _merge_mlp_kernel(x_ref, p_ref, ht_ref, g1_ref, sh2_ref, sc2_ref, g2_ref,
                      n2_ref, nf_ref, gmh_ref, gsgu_ref, wa_ref, wb_ref, wo_ref, ws_ref, bst_ref,
                      w1_ref, w2_ref, o_ref, t_ref):
    tm = x_ref.shape[0]
    D = D_MODEL
    seg = lambda s: p_ref[:, s * D:(s + 1) * D].astype(F32)

    h_m = jnp.concatenate([ht_ref[mc].astype(F32).T for mc in range(tm // MCHUNK)], axis=0)
    hm = jax.nn.sigmoid(seg(0)) * h_m
    gmh = gmh_ref[...]
    parts = []
    for h in range(N_HEADS):
        cols = slice(h * HEAD_DIM, (h + 1) * HEAD_DIM)
        parts.append(_rms(hm[:, cols], gmh[:, cols]).astype(BF16))
    ya = _dot(jnp.concatenate(parts, axis=1), wa_ref[...])

    u = jax.nn.gelu(seg(1))
    vg = jax.nn.gelu(seg(2))
    vn = _rms(vg, gsgu_ref[...]).astype(BF16)
    bst = bst_ref[...]
    for cc in range(tm // CHUNK):
        rows = slice(cc * CHUNK, (cc + 1) * CHUNK)
        for g in range(N_GROUPS):
            cols = slice(g * GROUP_DIM, (g + 1) * GROUP_DIM)
            s = _dot(ws_ref[g], vn[rows, cols]) + bst[:, g:g + 1]
            t_ref[rows, cols] = (u[rows, cols] * s).astype(BF16)
    yb = _dot(t_ref[...], wb_ref[...])

    y = jax.nn.sigmoid(seg(3)) * ya + jax.nn.sigmoid(seg(4)) * yb
    mix = _dot(y.astype(BF16), wo_ref[...])
    x1 = x_ref[...] + g1_ref[...] * mix

    xn2 = (_rms(x1, n2_ref[...]) * (1.0 + sc2_ref[...]) + sh2_ref[...]).astype(BF16)
    ff = D_FF // 4
    acc = jnp.zeros((tm, D), F32)
    for kk in range(4):
        hmid = jnp.maximum(_dot(xn2, w1_ref[:, kk * ff:(kk + 1) * ff]), 0.0)
        acc = acc + _dot((hmid * hmid).astype(BF16), w2_ref[kk * ff:(kk + 1) * ff, :])
    x2 = x1 + g2_ref[...] * acc
    o_ref[...] = _rms(x2, nf_ref[...])


def _merge_mlp(x, pm, ht, mod3, norm2, norm_f, g_mh, g_sgu, w_a, w_b, w_out, w_s, b_st, w1, w2, tm):
    B, L, D = x.shape
    mod_spec = lambda k: pl.BlockSpec((None, 1, D), lambda b, i: (b, 0, k))
    return pl.pallas_call(
        _merge_mlp_kernel,
        grid=(B, L // tm),
        in_specs=[
            pl.BlockSpec((None, tm, D), lambda b, i: (b, i, 0)),
            pl.BlockSpec((None, tm, N_MERGE_SEG * D), lambda b, i: (b, i, 0)),
            pl.BlockSpec((None, tm // MCHUNK, D, MCHUNK), lambda b, i: (b, i, 0, 0)),
            mod_spec(2), mod_spec(3), mod_spec(4), mod_spec(5),
            _const_spec((1, D)), _const_spec((1, D)), _const_spec((1, D)), _const_spec((1, D)),
            _const_spec((D, D)), _const_spec((D, D)), _const_spec((D, D)),
            _const_spec(w_s.shape), _const_spec(b_st.shape),
            _const_spec((D, D_FF)), _const_spec((D_FF, D)),
        ],
        out_specs=pl.BlockSpec((None, tm, D), lambda b, i: (b, i, 0)),
        out_shape=jax.ShapeDtypeStruct((B, L, D), F32),
        scratch_shapes=[pltpu.VMEM((tm, D), BF16)],
        compiler_params=pltpu.CompilerParams(
            dimension_semantics=("parallel", "parallel"), vmem_limit_bytes=VMEM_LIMIT),
        name="merge_mlp",
    )(x, pm, ht, mod3, mod3, mod3, mod3, norm2, norm_f, g_mh, g_sgu, w_a, w_b, w_out, w_s, b_st, w1, w2)


def _gate_weights(w_gate, b_gate):
    D = w_gate.shape[0]

    def series(a, kind_of_dir):
        a4 = a.reshape(a.shape[:-1] + (4, N_HEADS))
        per_dir = jnp.stack([a4[..., kind_of_dir[0], :], a4[..., kind_of_dir[1], :]], axis=-1)
        dup = jnp.broadcast_to(per_dir[..., :, None, :], per_dir.shape[:-1] + (N_QUANT, 2))
        return dup.reshape(a.shape[:-1] + (SERIES,))

    wf, wi = series(w_gate, (1, 3)), series(w_gate, (0, 2))
    bf, bi = series(b_gate, (1, 3)), series(b_gate, (0, 2))
    eye = jnp.eye(CHUNKS_PER_TILE, dtype=w_gate.dtype)
    place = lambda w: jnp.einsum("ab,dk->adbk", eye, w).reshape(CHUNKS_PER_TILE, D, LANES)
    wgs = jnp.concatenate([place(wf), place(wi)], axis=2).astype(BF16)
    bgs = jnp.concatenate([jnp.tile(bf, CHUNKS_PER_TILE), jnp.tile(bi, CHUNKS_PER_TILE)])[None, :]
    return wgs, bgs


def kernel(x, c, ctx, c_ctx, norm1, norm2, w_mod, b_mod, w_in, conv_qk, b_gate, g_mh, w_a, w_s, b_s,
           g_sgu, w_b, w_out, w1, w2, norm_f):
    B, S, D = x.shape
    Lx = ctx.shape[1]
    tm = 2 * MCHUNK
    assert D == D_MODEL and S % (MCHUNK * CHUNKS_PER_TILE) == 0 and Lx % MCHUNK == 0
    assert Lx // MCHUNK <= CHUNKS_PER_TILE
    assert w_mod.shape[0] == 1, "single-layer block"
    W = D_MODEL
    off_g = 3 * W
    off_o = off_g + 4 * N_HEADS

    mod_rows = ((B + 1 + 7) // 8) * 8
    cc = jnp.zeros((mod_rows, D), F32).at[:B].set(c).at[B].set(c_ctx)
    mod = _modulation(cc, w_mod, b_mod)
    mod3 = mod.reshape(mod_rows, 1, 6 * D)

    wi = w_in[0]
    w_merge = wi[:, off_o:].astype(BF16)
    wq = wi[:, :W].astype(BF16)
    wk = wi[:, W:2 * W].astype(BF16)
    wvt = wi[:, 2 * W:off_g].T.astype(BF16)
    wgs, bgs = _gate_weights(wi[:, off_g:off_o], b_gate[0])

    n1 = norm1[0][None, :]
    conv = conv_qk[0]
    pm, q, k, vt, gf, gi = _inproj(x, mod3, lambda b: b, n1, wgs, bgs, conv, (w_merge, wq, wk, wvt), tm, True)
    kx, vtx, gfx, gix = _inproj(ctx, mod3, lambda b: B, n1, wgs, bgs, conv, (wk, wvt), Lx, False)

    ht = _mlstm(q, k, vt, kx, vtx, gf, gi, gfx, gix)

    return _merge_mlp(
        x, pm, ht, mod3, norm2[0][None, :], norm_f[None, :], g_mh[0][None, :], g_sgu[0][None, :],
        w_a[0].astype(BF16), w_b[0].astype(BF16), w_out[0].astype(BF16),
        w_s[0].astype(BF16), b_s[0].T, w1[0].astype(BF16), w2[0].astype(BF16), 512)
```

```python
import jax
import jax.numpy as jnp
from jax import lax
from jax.experimental import pallas as pl
from jax.experimental.pallas import tpu as pltpu

D_MODEL = 1024
N_HEADS = 4
HEAD_DIM = D_MODEL // N_HEADS
CHUNK = 128
MCHUNK = 256
GROUP_DIM = 128
N_GROUPS = D_MODEL // GROUP_DIM
D_FF = 4 * D_MODEL
CONV_W = 3
NEG = -1e30
EPS = 1e-6
LANES = 128
HALO = 16
EXTRA = 16

N_QUANT = 4
ROWS_PER_HEAD = 2 * N_QUANT
SERIES = N_HEADS * ROWS_PER_HEAD
CHUNKS_PER_TILE = LANES // SERIES
Q_G, Q_E, Q_S, Q_FL = 0, 1, 2, 3

N_MERGE_SEG = 5

VMEM_LIMIT = 60000 * 1024

F32 = jnp.float32
BF16 = jnp.bfloat16


def _dot(a, b):
    return jnp.dot(a, b, preferred_element_type=F32)


def _dot_nt(a, b):
    return lax.dot_general(a, b, (((1,), (1,)), ((), ())), preferred_element_type=F32)


def _split_hi_lo(x):
    hi = x.astype(BF16)
    lo = (x - hi.astype(F32)).astype(BF16)
    return hi, lo


def _sigmoid(x):
    return 0.5 * jnp.tanh(0.5 * x) + 0.5


def _silu(x):
    half = 0.5 * x
    return half * jnp.tanh(half) + half


def _gelu(x):
    c = 0.7978845608028654
    half = 0.5 * x
    return half * jnp.tanh(x * (c + (c * 0.044715) * (x * x))) + half


MERGE_ACT = (_sigmoid, None, None, _sigmoid, _sigmoid)


def _const_spec(shape):
    nd = len(shape)
    return pl.BlockSpec(shape, lambda *_: (0,) * nd, pipeline_mode=pl.Buffered(1))


def _mod_kernel(c_ref, w_ref, b_ref, o_ref):
    s = _silu(c_ref[...])
    s_hi, s_lo = _split_hi_lo(s)
    w_hi, w_lo = _split_hi_lo(w_ref[...])
    acc = _dot(s_hi, w_hi) + _dot(s_hi, w_lo) + _dot(s_lo, w_hi)
    o_ref[...] = acc + b_ref[...]


def _modulation(cc, w_mod, b_mod):
    rows, d = cc.shape
    n = w_mod.shape[2]
    tn = 1536
    return pl.pallas_call(
        _mod_kernel,
        grid=(n // tn,),
        in_specs=[
            pl.BlockSpec((rows, d), lambda j: (0, 0)),
            pl.BlockSpec((None, d, tn), lambda j: (0, 0, j)),
            pl.BlockSpec((1, tn), lambda j: (0, j)),
        ],
        out_specs=pl.BlockSpec((rows, tn), lambda j: (0, j)),
        out_shape=jax.ShapeDtypeStruct((rows, n), F32),
        compiler_params=pltpu.CompilerParams(
            dimension_semantics=("arbitrary",), vmem_limit_bytes=VMEM_LIMIT),
        name="modulation",
    )(cc, w_mod, b_mod)


def _log_sigmoid(x):
    return jnp.minimum(x, 0.0) - jnp.log1p(jnp.exp(-jnp.abs(x)))


def _make_inproj_kernel(tm, n_tiles, latent):
    n_cc = tm // CHUNK
    n_mc = tm // MCHUNK
    tiles_per_gate_tile = CHUNKS_PER_TILE // n_mc

    def kernel(*refs):
        if latent:
            (x_ref, xp_ref, xnx_ref, sh_ref, sc_ref, g_ref, wg_ref, bg_ref, cw_ref, wm_ref, wq_ref, wk_ref, wvt_ref,
             pm_ref, q_ref, k_ref, vt_ref, gf_ref, gi_ref, xe_ref, peq_ref, pek_ref) = refs
        else:
            (x_ref, xp_ref, xnx_ref, sh_ref, sc_ref, g_ref, wg_ref, bg_ref, cw_ref, wk_ref, wvt_ref,
             k_ref, vt_ref, gf_ref, gi_ref, xe_ref, pek_ref) = refs
        i = pl.program_id(1)

        def normed(x):
            ms = jnp.mean(x * x, axis=-1, keepdims=True)
            y = x * lax.rsqrt(ms + EPS) * g_ref[...]
            return y * (1.0 + sc_ref[...]) + sh_ref[...]

        xn = normed(x_ref[...]).astype(BF16)
        xe_ref[HALO:HALO + tm, :] = xn
        xe_ref[0:HALO, :] = jnp.where(i > 0, normed(xp_ref[...]), 0.0).astype(BF16)
        xe_ref[HALO + tm:, :] = jnp.where(i < n_tiles - 1, normed(xnx_ref[...]), 0.0).astype(BF16)

        pad = 8
        n_ext = CHUNK + 2 * pad

        def conv_silu(pe_ref, cw, scale, out_ref, cc):
            r = HALO + cc * CHUNK
            x_ext = pe_ref[r - pad:r + CHUNK + pad, :]
            prev = pltpu.roll(x_ext, 1, 0)[pad:pad + CHUNK, :]
            nxt = pltpu.roll(x_ext, n_ext - 1, 0)[pad:pad + CHUNK, :]
            y = prev * cw[0:1, :] + x_ext[pad:pad + CHUNK, :] * cw[1:2, :] + nxt * cw[2:3, :]
            y = _silu(y)
            if scale != 1.0:
                y = y * scale
            out_ref[cc * CHUNK:(cc + 1) * CHUNK, :] = y.astype(BF16)

        def merge_seg(s):
            cols = slice(s * D_MODEL, (s + 1) * D_MODEL)
            z = _dot(xn, wm_ref[:, cols])
            if MERGE_ACT[s] is not None:
                z = MERGE_ACT[s](z)
            pm_ref[:, cols] = z.astype(BF16)

        cw = cw_ref[...]
        pek_ref[...] = _dot(xe_ref[...], wk_ref[...])
        epilogues = [(pek_ref, cw[:, D_MODEL:], HEAD_DIM ** -0.5, k_ref, cc) for cc in range(n_cc)]
        if latent:
            peq_ref[...] = _dot(xe_ref[...], wq_ref[...])
            epilogues += [(peq_ref, cw[:, :D_MODEL], 1.0, q_ref, cc) for cc in range(n_cc)]
            per_seg = -(-len(epilogues) // (N_MERGE_SEG - 1))
            for s in range(N_MERGE_SEG):
                merge_seg(s)
                for args in epilogues[s * per_seg:(s + 1) * per_seg]:
                    conv_silu(*args)
        else:
            for args in epilogues:
                conv_silu(*args)

        sub = i % tiles_per_gate_tile
        acc = _dot(xn[0:MCHUNK], wg_ref[sub * n_mc])
        for mc in range(1, n_mc):
            acc = acc + _dot(xn[mc * MCHUNK:(mc + 1) * MCHUNK], wg_ref[sub * n_mc + mc])
        acc = acc + bg_ref[...]
        group = lax.broadcasted_iota(jnp.int32, (1, LANES), 1) // SERIES
        own = (group >= sub * n_mc) & (group < (sub + 1) * n_mc)
        new_f = jnp.where(own, _log_sigmoid(acc[:, :LANES]), 0.0)
        new_i = jnp.where(own, acc[:, LANES:], 0.0)

        vt = _dot_nt(wvt_ref[...], xn)
        for mc in range(n_mc):
            vt_ref[mc] = vt[:, mc * MCHUNK:(mc + 1) * MCHUNK].astype(BF16)

        @pl.when(sub == 0)
        def _():
            gf_ref[...] = new_f
            gi_ref[...] = new_i

        @pl.when(sub != 0)
        def _():
            gf_ref[...] += new_f
            gi_ref[...] += new_i

    return kernel


def _inproj(x, mod3, mod_row_of_batch, norm_g, wgs, bgs, conv_qk, weights, tm, latent):
    B, L, D = x.shape
    n_tiles = L // tm
    n_mc = tm // MCHUNK
    tiles_per_gate_tile = CHUNKS_PER_TILE // n_mc
    n_gate_tiles = -(-n_tiles // tiles_per_gate_tile)
    hb = tm // HALO
    n_hblk = L // HALO
    row = mod_row_of_batch
    tok = lambda width: pl.BlockSpec((None, tm, width), lambda b, i: (b, i, 0))
    tok_shape = lambda width: jax.ShapeDtypeStruct((B, L, width), BF16)
    widths = [w.shape[1] for w in weights[:-1]]
    gate_spec = pl.BlockSpec((None, MCHUNK, LANES), lambda b, i: (b, 0, i // tiles_per_gate_tile))
    gate_shape = jax.ShapeDtypeStruct((B, MCHUNK, n_gate_tiles * LANES), F32)
    vt_spec = pl.BlockSpec((None, n_mc, D, MCHUNK), lambda b, i: (b, i, 0, 0))
    vt_shape = jax.ShapeDtypeStruct((B, L // MCHUNK, D, MCHUNK), BF16)
    return pl.pallas_call(
        _make_inproj_kernel(tm, n_tiles, latent),
        grid=(B, n_tiles),
        in_specs=[
            pl.BlockSpec((None, tm, D), lambda b, i: (b, i, 0)),
            pl.BlockSpec((None, HALO, D), lambda b, i: (b, jnp.maximum(i * hb - 1, 0), 0)),
            pl.BlockSpec((None, HALO, D), lambda b, i: (b, jnp.minimum((i + 1) * hb, n_hblk - 1), 0)),
            pl.BlockSpec((None, 1, D), lambda b, i: (row(b), 0, 0)),
            pl.BlockSpec((None, 1, D), lambda b, i: (row(b), 0, 1)),
            _const_spec((1, D)), _const_spec(wgs.shape), _const_spec(bgs.shape), _const_spec(conv_qk.shape),
        ] + [_const_spec(w.shape) for w in weights],
        out_specs=[tok(w) for w in widths] + [vt_spec, gate_spec, gate_spec],
        out_shape=[tok_shape(w) for w in widths] + [vt_shape, gate_shape, gate_shape],
        scratch_shapes=[pltpu.VMEM((tm + 2 * HALO, D), BF16)]
        + [pltpu.VMEM((tm + 2 * HALO, D), F32)] * (2 if latent else 1),
        compiler_params=pltpu.CompilerParams(
            dimension_semantics=("parallel", "arbitrary"), vmem_limit_bytes=VMEM_LIMIT),
        name="inproj_latent" if latent else "inproj_context",
    )(x, x, x, mod3, mod3, norm_g, wgs, bgs, conv_qk, *weights)


def _cummax_rows(x, reverse):
    n = x.shape[0]
    rows = lax.broadcasted_iota(jnp.int32, x.shape, 0)
    k = 1
    while k < n:
        if reverse:
            shifted = jnp.where(rows < n - k, pltpu.roll(x, n - k, 0), NEG)
        else:
            shifted = jnp.where(rows >= k, pltpu.roll(x, k, 0), NEG)
        x = jnp.maximum(x, shifted)
        k *= 2
    return x


def _gate_prep(gf, gi, n_lat, n_ctx, cols_ref, rt_ref, sp_ref):
    n_ext = n_lat + n_ctx
    width = gf.shape[1]
    n_tile = width // LANES
    ii = lax.broadcasted_iota(jnp.int32, (MCHUNK, MCHUNK), 0)
    jj = lax.broadcasted_iota(jnp.int32, (MCHUNK, MCHUNK), 1)
    tril = jnp.where(ii >= jj, 1.0, 0.0).astype(BF16)
    triu = jnp.where(ii <= jj, 1.0, 0.0).astype(BF16)
    lane = lax.broadcasted_iota(jnp.int32, (1, width), 1)
    lane_bwd = (lane % 2) == 1
    lane_q = (lane // 2) % N_QUANT

    hi, lo = _split_hi_lo(gf)
    b = jnp.where(lane_bwd, _dot(triu, hi) + _dot(triu, lo), _dot(tril, hi) + _dot(tril, lo))
    tot = jnp.where(lane_bwd, b[0:1, :], b[MCHUNK - 1:MCHUNK, :])
    r = gi - b
    cm = jnp.where(lane_bwd, _cummax_rows(r, True), _cummax_rows(r, False))
    rmax = jnp.where(lane_bwd, cm[0:1, :], cm[MCHUNK - 1:MCHUNK, :])

    def to_rows(v):
        return [jnp.broadcast_to(v[:, t * LANES:(t + 1) * LANES], (LANES, LANES)).T for t in range(n_tile)]

    tot_t, rmax_t = to_rows(tot), to_rows(rmax)

    def slab(tiles, e):
        t, k = divmod(e, CHUNKS_PER_TILE)
        return tiles[t][k * SERIES:(k + 1) * SERIES, :]

    ctx_ids = list(range(n_lat, n_ext))
    orders = (ctx_ids + list(range(n_lat)), ctx_ids[::-1] + list(range(n_lat - 1, -1, -1)))
    m_prev, m_new = [{}, {}], [{}, {}]
    for d, order in enumerate(orders):
        m = jnp.zeros((SERIES, LANES), F32)
        for e in order:
            m_prev[d][e] = m
            tot_e = slab(tot_t, e)
            m = jnp.maximum(tot_e + m, tot_e + slab(rmax_t, e))
            m_new[d][e] = m
    row_bwd = (lax.broadcasted_iota(jnp.int32, (SERIES, LANES), 0) % 2) == 1
    mp_slabs = [jnp.where(row_bwd, m_prev[1][e], m_prev[0][e]) for e in range(n_ext)]
    mn_slabs = [jnp.where(row_bwd, m_new[1][e], m_new[0][e]) for e in range(n_ext)]
    sp_slabs = [jnp.exp(slab(tot_t, e) + mp_slabs[e] - mn_slabs[e]) for e in range(n_ext)]

    def tiles_of(slabs):
        pad = [jnp.zeros((SERIES, LANES), F32)] * (n_tile * CHUNKS_PER_TILE - n_ext)
        full = slabs + pad
        return [jnp.concatenate(full[t * CHUNKS_PER_TILE:(t + 1) * CHUNKS_PER_TILE], axis=0) for t in range(n_tile)]

    def to_lanes(slabs):
        return jnp.concatenate([t.T[0:1, :] for t in tiles_of(slabs)], axis=1)

    mp = to_lanes(mp_slabs)
    mn = to_lanes(mn_slabs)

    g = jnp.maximum(mp, cm)
    e_w = jnp.exp(tot + r - mn)
    s_inter = jnp.exp(mp - g)
    floor = jnp.exp(-(b + g))
    packed = jnp.where(lane_q == Q_G, g, jnp.where(lane_q == Q_E, e_w, jnp.where(lane_q == Q_S, s_inter, floor)))

    sp_tiles = tiles_of(sp_slabs)
    for t in range(n_tile):
        rt_ref[t * LANES:(t + 1) * LANES, :] = packed[:, t * LANES:(t + 1) * LANES].T
        sp_ref[t * LANES:(t + 1) * LANES, :] = sp_tiles[t]
    for e in range(n_ext):
        t, k = divmod(e, CHUNKS_PER_TILE)
        tile = r[:, t * LANES:(t + 1) * LANES]
        for h in range(N_HEADS):
            shift = k * SERIES + h * ROWS_PER_HEAD
            cols_ref[h, e] = tile if shift == 0 else pltpu.roll(tile, LANES - shift, 1)


def _mlstm_kernel(q_ref, k_ref, vt_ref, kx_ref, vtx_ref, gf_ref, gi_ref, gfx_ref, gix_ref,
                  o_ref, cols_ref, rt_ref, sp_ref, ct_ref, n_ref, cprev_ref, at_ref):
    n_lat = q_ref.shape[0] // MCHUNK
    n_ctx = kx_ref.shape[0] // MCHUNK
    h = pl.program_id(1)

    @pl.when(h == 0)
    def _():
        gf = jnp.concatenate([gf_ref[...], gfx_ref[...]], axis=1)
        gi = jnp.concatenate([gi_ref[...], gix_ref[...]], axis=1)
        _gate_prep(gf, gi, n_lat, n_ctx, cols_ref, rt_ref, sp_ref)

    def series_rows(ref, e):
        r0 = pl.multiple_of(e * SERIES + h * ROWS_PER_HEAD, ROWS_PER_HEAD)
        return ref[pl.ds(r0, ROWS_PER_HEAD), :]

    def row(rt, quantity, d):
        i = 2 * quantity + d
        return rt[i:i + 1, :]

    def state_step(d, e, k, vt, c_lat):
        r0 = pl.multiple_of(e * SERIES + h * ROWS_PER_HEAD, ROWS_PER_HEAD)
        rt_x = rt_ref[pl.ds(r0, EXTRA), :]
        s_row = series_rows(sp_ref, e)[d:d + 1, :]
        s2 = jnp.concatenate([s_row, s_row], axis=1)
        vet = (vt.astype(F32) * row(rt_x, Q_E, d)).astype(BF16)
        upd = _dot(jnp.concatenate([vet, rt_x.astype(BF16)], axis=0), k)
        i_e = HEAD_DIM + 2 * Q_E + d
        ek = upd[i_e:i_e + 1, :]
        ct_old = ct_ref[d]
        n_old = n_ref[d]
        if c_lat is not None:
            cprev_ref[d, c_lat, 0:HEAD_DIM, :] = ct_old.astype(BF16)
            cprev_ref[d, c_lat, HEAD_DIM:, :] = jnp.broadcast_to(n_old, (EXTRA, HEAD_DIM)).astype(BF16)
        ct_ref[d] = s2 * ct_old + upd[0:HEAD_DIM, :]
        n_ref[d] = s2 * n_old + ek

    ct_ref[...] = jnp.zeros_like(ct_ref)
    n_ref[...] = jnp.zeros_like(n_ref)
    for t in range(n_ctx):
        for d in range(2):
            cx = (n_ctx - 1 - t) if d else t
            rows = slice(cx * MCHUNK, (cx + 1) * MCHUNK)
            state_step(d, n_lat + cx, kx_ref[rows, :], vtx_ref[cx], None)

    def state_body(t, carry):
        for d in range(2):
            c = (n_lat - 1 - t) if d else t
            rows = pl.ds(pl.multiple_of(c * MCHUNK, MCHUNK), MCHUNK)
            state_step(d, c, k_ref[rows, :], vt_ref[c], c)
        return carry

    lax.fori_loop(0, n_lat, state_body, 0, unroll=True)

    jj = lax.broadcasted_iota(jnp.int32, (MCHUNK, MCHUNK), 0)
    ii = lax.broadcasted_iota(jnp.int32, (MCHUNK, MCHUNK), 1)
    visible = (jj <= ii, jj >= ii)

    ones_rows = jnp.ones((EXTRA, MCHUNK), BF16)

    def weights_stage(c, slot):
        rows = pl.ds(pl.multiple_of(c * MCHUNK, MCHUNK), MCHUNK)
        cols = cols_ref[h, c]
        rt = series_rows(rt_ref, c)
        st = _dot_nt(k_ref[rows, :], q_ref[rows, :])
        for d in range(2):
            w = jnp.exp(jnp.where(visible[d], cols[:, d:d + 1] - row(rt, Q_G, d), NEG))
            at_ref[slot, :, d * MCHUNK:(d + 1) * MCHUNK] = (w * st).astype(BF16)

    def readout_stage(c, slot):
        rows = pl.ds(pl.multiple_of(c * MCHUNK, MCHUNK), MCHUNK)
        q = q_ref[rows, :]
        rt = series_rows(rt_ref, c)
        at = at_ref[slot]
        num = _dot(jnp.concatenate([vt_ref[c], ones_rows], axis=0), at)
        out = None
        for d in range(2):
            lanes = slice(d * MCHUNK, (d + 1) * MCHUNK)
            s_inter = row(rt, Q_S, d)
            inter = _dot_nt(cprev_ref[d, c], q)
            den = num[HEAD_DIM:HEAD_DIM + 1, lanes] + s_inter * inter[HEAD_DIM:HEAD_DIM + 1, :]
            inv = 1.0 / jnp.maximum(jnp.abs(den), row(rt, Q_FL, d))
            hd = (num[0:HEAD_DIM, lanes] + s_inter * inter[0:HEAD_DIM, :]) * inv
            out = hd if out is None else out + hd
        o_ref[c] = out.astype(o_ref.dtype)

    weights_stage(0, 0)

    def out_body(c, carry):
        readout_stage(c, c % 2)
        weights_stage(c + 1, (c + 1) % 2)
        return carry

    lax.fori_loop(0, n_lat - 1, out_body, 0, unroll=True)
    readout_stage(n_lat - 1, (n_lat - 1) % 2)


def _mlstm(q, k, vt, kx, vtx, gf, gi, gfx, gix):
    B, L, _ = q.shape
    Lx = kx.shape[1]
    n_lat, n_ctx = L // MCHUNK, Lx // MCHUNK
    n_ext = n_lat + n_ctx
    n_tile = gf.shape[2] // LANES + gfx.shape[2] // LANES
    head_cols = lambda n: pl.BlockSpec((None, n, HEAD_DIM), lambda b, h: (b, 0, h))
    head_rows = lambda n: pl.BlockSpec((None, n, HEAD_DIM, MCHUNK), lambda b, h: (b, 0, h, 0))
    gate_spec = lambda a: pl.BlockSpec((None,) + a.shape[1:], lambda b, h: (b, 0, 0))
    return pl.pallas_call(
        _mlstm_kernel,
        grid=(B, N_HEADS),
        in_specs=[
            head_cols(L), head_cols(L), head_rows(n_lat), head_cols(Lx), head_rows(n_ctx),
            gate_spec(gf), gate_spec(gi), gate_spec(gfx), gate_spec(gix),
        ],
        out_specs=head_rows(n_lat),
        out_shape=jax.ShapeDtypeStruct((B, n_lat, D_MODEL, MCHUNK), BF16),
        scratch_shapes=[
            pltpu.VMEM((N_HEADS, n_ext, MCHUNK, LANES), F32),
            pltpu.VMEM((n_tile * LANES, MCHUNK), F32),
            pltpu.VMEM((n_tile * LANES, LANES), F32),
            pltpu.VMEM((2, HEAD_DIM, HEAD_DIM), F32),
            pltpu.VMEM((2, 1, HEAD_DIM), F32),
            pltpu.VMEM((2, n_lat, HEAD_DIM + EXTRA, HEAD_DIM), BF16),
            pltpu.VMEM((2, MCHUNK, 2 * MCHUNK), BF16),
        ],
        compiler_params=pltpu.CompilerParams(
            dimension_semantics=("parallel", "arbitrary"), vmem_limit_bytes=VMEM_LIMIT),
        name="mlstm",
    )(q, k, vt, kx, vtx, gf, gi, gfx, gix)


def _rms(x, g):
    ms = jnp.mean(x * x, axis=-1, keepdims=True)
    return x * lax.rsqrt(ms + EPS) * g


def _merge_mlp_kernel(x_ref, p_ref, ht_ref, g1_ref, sh2_ref, sc2_ref, g2_ref,
                      n2_ref, nf_ref, gmh_ref, gsgu_ref, wa_ref, wb_ref, wo_ref, ws_ref, bst_ref,
                      w1_ref, w2_ref, o_ref, t_ref):
    tm = x_ref.shape[0]
    D = D_MODEL
    seg = lambda s: p_ref[:, s * D:(s + 1) * D].astype(F32)

    h_m = jnp.concatenate([ht_ref[mc].astype(F32).T for mc in range(tm // MCHUNK)], axis=0)
    hm = seg(0) * h_m
    gmh = gmh_ref[...]
    parts = []
    for h in range(N_HEADS):
        cols = slice(h * HEAD_DIM, (h + 1) * HEAD_DIM)
        parts.append(_rms(hm[:, cols], gmh[:, cols]).astype(BF16))
    ya = _dot(jnp.concatenate(parts, axis=1), wa_ref[...])

    u = _gelu(seg(1))
    vn = _rms(_gelu(seg(2)), gsgu_ref[...]).astype(BF16)
    bst = bst_ref[...]
    for cc in range(tm // CHUNK):
        rows = slice(cc * CHUNK, (cc + 1) * CHUNK)
        for g in range(N_GROUPS):
            cols = slice(g * GROUP_DIM, (g + 1) * GROUP_DIM)
            s = _dot(ws_ref[g], vn[rows, cols]) + bst[:, g:g + 1]
            t_ref[rows, cols] = (u[rows, cols] * s).astype(BF16)
    yb = _dot(t_ref[...], wb_ref[...])

    y = seg(3) * ya + seg(4) * yb
    mix = _dot(y.astype(BF16), wo_ref[...])
    x1 = x_ref[...] + g1_ref[...] * mix

    xn2 = (_rms(x1, n2_ref[...]) * (1.0 + sc2_ref[...]) + sh2_ref[...]).astype(BF16)
    ff = D_FF // 4
    acc = jnp.zeros((tm, D), F32)
    for kk in range(4):
        hmid = jnp.maximum(_dot(xn2, w1_ref[:, kk * ff:(kk + 1) * ff]), 0.0)
        acc = acc + _dot((hmid * hmid).astype(BF16), w2_ref[kk * ff:(kk + 1) * ff, :])
    x2 = x1 + g2_ref[...] * acc
    o_ref[...] = _rms(x2, nf_ref[...])


def _merge_mlp(x, pm, ht, mod3, norm2, norm_f, g_mh, g_sgu, w_a, w_b, w_out, w_s, b_st, w1, w2, tm):
    B, L, D = x.shape
    mod_spec = lambda k: pl.BlockSpec((None, 1, D), lambda b, i: (b, 0, k))
    return pl.pallas_call(
        _merge_mlp_kernel,
        grid=(B, L // tm),
        in_specs=[
            pl.BlockSpec((None, tm, D), lambda b, i: (b, i, 0)),
            pl.BlockSpec((None, tm, N_MERGE_SEG * D), lambda b, i: (b, i, 0)),
            pl.BlockSpec((None, tm // MCHUNK, D, MCHUNK), lambda b, i: (b, i, 0, 0)),
            mod_spec(2), mod_spec(3), mod_spec(4), mod_spec(5),
            _const_spec((1, D)), _const_spec((1, D)), _const_spec((1, D)), _const_spec((1, D)),
            _const_spec((D, D)), _const_spec((D, D)), _const_spec((D, D)),
            _const_spec(w_s.shape), _const_spec(b_st.shape),
            _const_spec((D, D_FF)), _const_spec((D_FF, D)),
        ],
        out_specs=pl.BlockSpec((None, tm, D), lambda b, i: (b, i, 0)),
        out_shape=jax.ShapeDtypeStruct((B, L, D), F32),
        scratch_shapes=[pltpu.VMEM((tm, D), BF16)],
        compiler_params=pltpu.CompilerParams(
            dimension_semantics=("parallel", "parallel"), vmem_limit_bytes=VMEM_LIMIT),
        name="merge_mlp",
    )(x, pm, ht, mod3, mod3, mod3, mod3, norm2, norm_f, g_mh, g_sgu, w_a, w_b, w_out, w_s, b_st, w1, w2)


def _gate_weights(w_gate, b_gate):
    D = w_gate.shape[0]

    def series(a, kind_of_dir):
        a4 = a.reshape(a.shape[:-1] + (4, N_HEADS))
        per_dir = jnp.stack([a4[..., kind_of_dir[0], :], a4[..., kind_of_dir[1], :]], axis=-1)
        dup = jnp.broadcast_to(per_dir[..., :, None, :], per_dir.shape[:-1] + (N_QUANT, 2))
        return dup.reshape(a.shape[:-1] + (SERIES,))

    wf, wi = series(w_gate, (1, 3)), series(w_gate, (0, 2))
    bf, bi = series(b_gate, (1, 3)), series(b_gate, (0, 2))
    eye = jnp.eye(CHUNKS_PER_TILE, dtype=w_gate.dtype)
    place = lambda w: jnp.einsum("ab,dk->adbk", eye, w).reshape(CHUNKS_PER_TILE, D, LANES)
    wgs = jnp.concatenate([place(wf), place(wi)], axis=2).astype(BF16)
    bgs = jnp.concatenate([jnp.tile(bf, CHUNKS_PER_TILE), jnp.tile(bi, CHUNKS_PER_TILE)])[None, :]
    return wgs, bgs


def kernel(x, c, ctx, c_ctx, norm1, norm2, w_mod, b_mod, w_in, conv_qk, b_gate, g_mh, w_a, w_s, b_s,
           g_sgu, w_b, w_out, w1, w2, norm_f):
    B, S, D = x.shape
    Lx = ctx.shape[1]
    tm = 2 * MCHUNK
    assert D == D_MODEL and S % (MCHUNK * CHUNKS_PER_TILE) == 0 and Lx % MCHUNK == 0
    assert Lx // MCHUNK <= CHUNKS_PER_TILE
    assert w_mod.shape[0] == 1, "single-layer block"
    W = D_MODEL
    off_g = 3 * W
    off_o = off_g + 4 * N_HEADS

    mod_rows = ((B + 1 + 7) // 8) * 8
    cc = jnp.zeros((mod_rows, D), F32).at[:B].set(c).at[B].set(c_ctx)
    mod = _modulation(cc, w_mod, b_mod)
    mod3 = mod.reshape(mod_rows, 1, 6 * D)

    wi = w_in[0]
    w_merge = wi[:, off_o:].astype(BF16)
    wq = wi[:, :W].astype(BF16)
    wk = wi[:, W:2 * W].astype(BF16)
    wvt = wi[:, 2 * W:off_g].T.astype(BF16)
    wgs, bgs = _gate_weights(wi[:, off_g:off_o], b_gate[0])

    n1 = norm1[0][None, :]
    conv = conv_qk[0]
    pm, q, k, vt, gf, gi = _inproj(x, mod3, lambda b: b, n1, wgs, bgs, conv, (w_merge, wq, wk, wvt), tm, True)
    kx, vtx, gfx, gix = _inproj(ctx, mod3, lambda b: B, n1, wgs, bgs, conv, (wk, wvt), Lx, False)

    ht = _mlstm(q, k, vt, kx, vtx, gf, gi, gfx, gix)

    return _merge_mlp(
        x, pm, ht, mod3, norm2[0][None, :], norm_f[None, :], g_mh[0][None, :], g_sgu[0][None, :],
        w_a[0].astype(BF16), w_b[0].astype(BF16), w_out[0].astype(BF16),
        w_s[0].astype(BF16), b_s[0].T, w1[0].astype(BF16), w2[0].astype(BF16), 512)
```

```python
import jax
import jax.numpy as jnp
from jax import lax
from jax.experimental import pallas as pl
from jax.experimental.pallas import tpu as pltpu

D_MODEL = 1024
N_HEADS = 4
HEAD_DIM = D_MODEL // N_HEADS
CHUNK = 128
MCHUNK = 256
GROUP_DIM = 128
N_GROUPS = D_MODEL // GROUP_DIM
D_FF = 4 * D_MODEL
CONV_W = 3
NEG = -1e30
EPS = 1e-6
LANES = 128
HALO = 16
EXTRA = 16

N_QUANT = 4
ROWS_PER_HEAD = 2 * N_QUANT
SERIES = N_HEADS * ROWS_PER_HEAD
CHUNKS_PER_TILE = LANES // SERIES
Q_G, Q_E, Q_S, Q_FL = 0, 1, 2, 3

N_MERGE_SEG = 5

VMEM_LIMIT = 60000 * 1024
INPROJ_TM = 2 * MCHUNK
MERGE_TM = 512
MOD_TN = 1536

F32 = jnp.float32
BF16 = jnp.bfloat16


def _dot(a, b):
    return jnp.dot(a, b, preferred_element_type=F32)


def _dot_nt(a, b):
    return lax.dot_general(a, b, (((1,), (1,)), ((), ())), preferred_element_type=F32)


def _split_hi_lo(x):
    hi = x.astype(BF16)
    lo = (x - hi.astype(F32)).astype(BF16)
    return hi, lo


def _sigmoid(x):
    return 0.5 * jnp.tanh(0.5 * x) + 0.5


def _silu(x):
    half = 0.5 * x
    return half * jnp.tanh(half) + half


def _gelu(x):
    c = 0.7978845608028654
    half = 0.5 * x
    return half * jnp.tanh(x * (c + (c * 0.044715) * (x * x))) + half


MERGE_ACT = (_sigmoid, None, None, _sigmoid, _sigmoid)


def _const_spec(shape):
    nd = len(shape)
    return pl.BlockSpec(shape, lambda *_: (0,) * nd, pipeline_mode=pl.Buffered(1))


def _mod_kernel(c_ref, w_ref, b_ref, o_ref):
    s = _silu(c_ref[...])
    s_hi, s_lo = _split_hi_lo(s)
    w_hi, w_lo = _split_hi_lo(w_ref[...])
    acc = _dot(s_hi, w_hi) + _dot(s_hi, w_lo) + _dot(s_lo, w_hi)
    o_ref[...] = acc + b_ref[...]


def _modulation(cc, w_mod, b_mod):
    rows, d = cc.shape
    n = w_mod.shape[2]
    tn = MOD_TN
    return pl.pallas_call(
        _mod_kernel,
        grid=(n // tn,),
        in_specs=[
            pl.BlockSpec((rows, d), lambda j: (0, 0)),
            pl.BlockSpec((None, d, tn), lambda j: (0, 0, j)),
            pl.BlockSpec((1, tn), lambda j: (0, j)),
        ],
        out_specs=pl.BlockSpec((rows, tn), lambda j: (0, j)),
        out_shape=jax.ShapeDtypeStruct((rows, n), F32),
        compiler_params=pltpu.CompilerParams(
            dimension_semantics=("arbitrary",), vmem_limit_bytes=VMEM_LIMIT),
        name="modulation",
    )(cc, w_mod, b_mod)


def _log_sigmoid(x):
    return jnp.minimum(x, 0.0) - jnp.log1p(jnp.exp(-jnp.abs(x)))


def _make_inproj_kernel(tm, n_tiles, latent):
    n_cc = tm // CHUNK
    n_mc = tm // MCHUNK
    tiles_per_gate_tile = CHUNKS_PER_TILE // n_mc

    def kernel(*refs):
        if latent:
            (x_ref, xp_ref, xnx_ref, sh_ref, sc_ref, g_ref, wg_ref, bg_ref, cw_ref, wm_ref, wq_ref, wk_ref, wvt_ref,
             pm_ref, q_ref, k_ref, vt_ref, gf_ref, gi_ref, xe_ref, peq_ref, pek_ref) = refs
        else:
            (x_ref, xp_ref, xnx_ref, sh_ref, sc_ref, g_ref, wg_ref, bg_ref, cw_ref, wk_ref, wvt_ref,
             k_ref, vt_ref, gf_ref, gi_ref, xe_ref, pek_ref) = refs
        i = pl.program_id(1)

        def normed(x):
            ms = jnp.mean(x * x, axis=-1, keepdims=True)
            y = x * lax.rsqrt(ms + EPS) * g_ref[...]
            return y * (1.0 + sc_ref[...]) + sh_ref[...]

        xn = normed(x_ref[...]).astype(BF16)
        xe_ref[HALO:HALO + tm, :] = xn
        xe_ref[0:HALO, :] = jnp.where(i > 0, normed(xp_ref[...]), 0.0).astype(BF16)
        xe_ref[HALO + tm:, :] = jnp.where(i < n_tiles - 1, normed(xnx_ref[...]), 0.0).astype(BF16)

        pad = 8
        n_ext = CHUNK + 2 * pad

        def conv_silu(pe_ref, w_ref, cw, scale, out_ref):
            pe_ref[...] = _dot(xe_ref[...], w_ref[...])
            for cc in range(n_cc):
                r = HALO + cc * CHUNK
                x_ext = pe_ref[r - pad:r + CHUNK + pad, :]
                prev = pltpu.roll(x_ext, 1, 0)[pad:pad + CHUNK, :]
                nxt = pltpu.roll(x_ext, n_ext - 1, 0)[pad:pad + CHUNK, :]
                y = prev * cw[0:1, :] + x_ext[pad:pad + CHUNK, :] * cw[1:2, :] + nxt * cw[2:3, :]
                y = _silu(y)
                if scale != 1.0:
                    y = y * scale
                out_ref[cc * CHUNK:(cc + 1) * CHUNK, :] = y.astype(BF16)

        cw = cw_ref[...]
        if latent:
            conv_silu(peq_ref, wq_ref, cw[:, :D_MODEL], 1.0, q_ref)
        conv_silu(pek_ref, wk_ref, cw[:, D_MODEL:], HEAD_DIM ** -0.5, k_ref)

        if latent:
            for s in range(N_MERGE_SEG):
                cols = slice(s * D_MODEL, (s + 1) * D_MODEL)
                z = _dot(xn, wm_ref[:, cols])
                if MERGE_ACT[s] is not None:
                    z = MERGE_ACT[s](z)
                pm_ref[:, cols] = z.astype(BF16)

        sub = i % tiles_per_gate_tile
        acc = _dot(xn[0:MCHUNK], wg_ref[sub * n_mc])
        for mc in range(1, n_mc):
            acc = acc + _dot(xn[mc * MCHUNK:(mc + 1) * MCHUNK], wg_ref[sub * n_mc + mc])
        acc = acc + bg_ref[...]
        group = lax.broadcasted_iota(jnp.int32, (1, LANES), 1) // SERIES
        own = (group >= sub * n_mc) & (group < (sub + 1) * n_mc)
        new_f = jnp.where(own, _log_sigmoid(acc[:, :LANES]), 0.0)
        new_i = jnp.where(own, acc[:, LANES:], 0.0)

        vt = _dot_nt(wvt_ref[...], xn)
        for mc in range(n_mc):
            vt_ref[mc] = vt[:, mc * MCHUNK:(mc + 1) * MCHUNK].astype(BF16)

        @pl.when(sub == 0)
        def _():
            gf_ref[...] = new_f
            gi_ref[...] = new_i

        @pl.when(sub != 0)
        def _():
            gf_ref[...] += new_f
            gi_ref[...] += new_i

    return kernel


def _inproj(x, mod3, mod_row_of_batch, norm_g, wgs, bgs, conv_qk, weights, tm, latent):
    B, L, D = x.shape
    n_tiles = L // tm
    n_mc = tm // MCHUNK
    tiles_per_gate_tile = CHUNKS_PER_TILE // n_mc
    n_gate_tiles = -(-n_tiles // tiles_per_gate_tile)
    hb = tm // HALO
    n_hblk = L // HALO
    row = mod_row_of_batch
    tok = lambda width: pl.BlockSpec((None, tm, width), lambda b, i: (b, i, 0))
    tok_shape = lambda width: jax.ShapeDtypeStruct((B, L, width), BF16)
    widths = [w.shape[1] for w in weights[:-1]]
    gate_spec = pl.BlockSpec((None, MCHUNK, LANES), lambda b, i: (b, 0, i // tiles_per_gate_tile))
    gate_shape = jax.ShapeDtypeStruct((B, MCHUNK, n_gate_tiles * LANES), F32)
    vt_spec = pl.BlockSpec((None, n_mc, D, MCHUNK), lambda b, i: (b, i, 0, 0))
    vt_shape = jax.ShapeDtypeStruct((B, L // MCHUNK, D, MCHUNK), BF16)
    return pl.pallas_call(
        _make_inproj_kernel(tm, n_tiles, latent),
        grid=(B, n_tiles),
        in_specs=[
            pl.BlockSpec((None, tm, D), lambda b, i: (b, i, 0)),
            pl.BlockSpec((None, HALO, D), lambda b, i: (b, jnp.maximum(i * hb - 1, 0), 0)),
            pl.BlockSpec((None, HALO, D), lambda b, i: (b, jnp.minimum((i + 1) * hb, n_hblk - 1), 0)),
            pl.BlockSpec((None, 1, D), lambda b, i: (row(b), 0, 0)),
            pl.BlockSpec((None, 1, D), lambda b, i: (row(b), 0, 1)),
            _const_spec((1, D)), _const_spec(wgs.shape), _const_spec(bgs.shape), _const_spec(conv_qk.shape),
        ] + [_const_spec(w.shape) for w in weights],
        out_specs=[tok(w) for w in widths] + [vt_spec, gate_spec, gate_spec],
        out_shape=[tok_shape(w) for w in widths] + [vt_shape, gate_shape, gate_shape],
        scratch_shapes=[pltpu.VMEM((tm + 2 * HALO, D), BF16)]
        + [pltpu.VMEM((tm + 2 * HALO, D), F32)] * (2 if latent else 1),
        compiler_params=pltpu.CompilerParams(
            dimension_semantics=("parallel", "arbitrary"), vmem_limit_bytes=VMEM_LIMIT),
        name="inproj_latent" if latent else "inproj_context",
    )(x, x, x, mod3, mod3, norm_g, wgs, bgs, conv_qk, *weights)


def _cummax_rows(x, reverse):
    n = x.shape[0]
    rows = lax.broadcasted_iota(jnp.int32, x.shape, 0)
    k = 1
    while k < n:
        if reverse:
            shifted = jnp.where(rows < n - k, pltpu.roll(x, n - k, 0), NEG)
        else:
            shifted = jnp.where(rows >= k, pltpu.roll(x, k, 0), NEG)
        x = jnp.maximum(x, shifted)
        k *= 2
    return x


def _gate_prep(gf, gi, n_lat, n_ctx, cols_ref, rt_ref, sp_ref):
    n_ext = n_lat + n_ctx
    width = gf.shape[1]
    n_tile = width // LANES
    ii = lax.broadcasted_iota(jnp.int32, (MCHUNK, MCHUNK), 0)
    jj = lax.broadcasted_iota(jnp.int32, (MCHUNK, MCHUNK), 1)
    tril = jnp.where(ii >= jj, 1.0, 0.0).astype(BF16)
    triu = jnp.where(ii <= jj, 1.0, 0.0).astype(BF16)
    lane = lax.broadcasted_iota(jnp.int32, (1, width), 1)
    lane_bwd = (lane % 2) == 1
    lane_q = (lane // 2) % N_QUANT

    hi, lo = _split_hi_lo(gf)
    b = jnp.where(lane_bwd, _dot(triu, hi) + _dot(triu, lo), _dot(tril, hi) + _dot(tril, lo))
    tot = jnp.where(lane_bwd, b[0:1, :], b[MCHUNK - 1:MCHUNK, :])
    r = gi - b
    cm = jnp.where(lane_bwd, _cummax_rows(r, True), _cummax_rows(r, False))
    rmax = jnp.where(lane_bwd, cm[0:1, :], cm[MCHUNK - 1:MCHUNK, :])

    def to_rows(v):
        return [jnp.broadcast_to(v[:, t * LANES:(t + 1) * LANES], (LANES, LANES)).T for t in range(n_tile)]

    tot_t, rmax_t = to_rows(tot), to_rows(rmax)

    def slab(tiles, e):
        t, k = divmod(e, CHUNKS_PER_TILE)
        return tiles[t][k * SERIES:(k + 1) * SERIES, :]

    ctx_ids = list(range(n_lat, n_ext))
    orders = (ctx_ids + list(range(n_lat)), ctx_ids[::-1] + list(range(n_lat - 1, -1, -1)))
    m_prev, m_new = [{}, {}], [{}, {}]
    for d, order in enumerate(orders):
        m = jnp.zeros((SERIES, LANES), F32)
        for e in order:
            m_prev[d][e] = m
            tot_e = slab(tot_t, e)
            m = jnp.maximum(tot_e + m, tot_e + slab(rmax_t, e))
            m_new[d][e] = m
    row_bwd = (lax.broadcasted_iota(jnp.int32, (SERIES, LANES), 0) % 2) == 1
    mp_slabs = [jnp.where(row_bwd, m_prev[1][e], m_prev[0][e]) for e in range(n_ext)]
    mn_slabs = [jnp.where(row_bwd, m_new[1][e], m_new[0][e]) for e in range(n_ext)]
    sp_slabs = [jnp.exp(slab(tot_t, e) + mp_slabs[e] - mn_slabs[e]) for e in range(n_ext)]

    def tiles_of(slabs):
        pad = [jnp.zeros((SERIES, LANES), F32)] * (n_tile * CHUNKS_PER_TILE - n_ext)
        full = slabs + pad
        return [jnp.concatenate(full[t * CHUNKS_PER_TILE:(t + 1) * CHUNKS_PER_TILE], axis=0) for t in range(n_tile)]

    def to_lanes(slabs):
        return jnp.concatenate([t.T[0:1, :] for t in tiles_of(slabs)], axis=1)

    mp = to_lanes(mp_slabs)
    mn = to_lanes(mn_slabs)

    g = jnp.maximum(mp, cm)
    e_w = jnp.exp(tot + r - mn)
    s_inter = jnp.exp(mp - g)
    floor = jnp.exp(-(b + g))
    packed = jnp.where(lane_q == Q_G, g, jnp.where(lane_q == Q_E, e_w, jnp.where(lane_q == Q_S, s_inter, floor)))

    sp_tiles = tiles_of(sp_slabs)
    for t in range(n_tile):
        rt_ref[t * LANES:(t + 1) * LANES, :] = packed[:, t * LANES:(t + 1) * LANES].T
        sp_ref[t * LANES:(t + 1) * LANES, :] = sp_tiles[t]
    for e in range(n_ext):
        t, k = divmod(e, CHUNKS_PER_TILE)
        tile = r[:, t * LANES:(t + 1) * LANES]
        for h in range(N_HEADS):
            shift = k * SERIES + h * ROWS_PER_HEAD
            cols_ref[h, e] = tile if shift == 0 else pltpu.roll(tile, LANES - shift, 1)


def _mlstm_kernel(q_ref, k_ref, vt_ref, kx_ref, vtx_ref, gf_ref, gi_ref, gfx_ref, gix_ref,
                  o_ref, cols_ref, rt_ref, sp_ref, ct_ref, n_ref, cprev_ref, at_ref):
    n_lat = q_ref.shape[0] // MCHUNK
    n_ctx = kx_ref.shape[0] // MCHUNK
    h = pl.program_id(1)

    @pl.when(h == 0)
    def _():
        gf = jnp.concatenate([gf_ref[...], gfx_ref[...]], axis=1)
        gi = jnp.concatenate([gi_ref[...], gix_ref[...]], axis=1)
        _gate_prep(gf, gi, n_lat, n_ctx, cols_ref, rt_ref, sp_ref)

    def series_rows(ref, e):
        r0 = pl.multiple_of(e * SERIES + h * ROWS_PER_HEAD, ROWS_PER_HEAD)
        return ref[pl.ds(r0, ROWS_PER_HEAD), :]

    def row(rt, quantity, d):
        i = 2 * quantity + d
        return rt[i:i + 1, :]

    def state_step(d, e, k, vt, c_lat):
        r0 = pl.multiple_of(e * SERIES + h * ROWS_PER_HEAD, ROWS_PER_HEAD)
        rt_x = rt_ref[pl.ds(r0, EXTRA), :]
        s_row = series_rows(sp_ref, e)[d:d + 1, :]
        s2 = jnp.concatenate([s_row, s_row], axis=1)
        vet = (vt.astype(F32) * row(rt_x, Q_E, d)).astype(BF16)
        upd = _dot(jnp.concatenate([vet, rt_x.astype(BF16)], axis=0), k)
        i_e = HEAD_DIM + 2 * Q_E + d
        ek = upd[i_e:i_e + 1, :]
        ct_old = ct_ref[d]
        n_old = n_ref[d]
        if c_lat is not None:
            cprev_ref[d, c_lat, 0:HEAD_DIM, :] = ct_old.astype(BF16)
            cprev_ref[d, c_lat, HEAD_DIM:, :] = jnp.broadcast_to(n_old, (EXTRA, HEAD_DIM)).astype(BF16)
        ct_ref[d] = s2 * ct_old + upd[0:HEAD_DIM, :]
        n_ref[d] = s2 * n_old + ek

    ct_ref[...] = jnp.zeros_like(ct_ref)
    n_ref[...] = jnp.zeros_like(n_ref)
    for t in range(n_ctx):
        for d in range(2):
            cx = (n_ctx - 1 - t) if d else t
            rows = slice(cx * MCHUNK, (cx + 1) * MCHUNK)
            state_step(d, n_lat + cx, kx_ref[rows, :], vtx_ref[cx], None)

    def state_body(t, carry):
        for d in range(2):
            c = (n_lat - 1 - t) if d else t
            rows = pl.ds(pl.multiple_of(c * MCHUNK, MCHUNK), MCHUNK)
            state_step(d, c, k_ref[rows, :], vt_ref[c], c)
        return carry

    lax.fori_loop(0, n_lat, state_body, 0, unroll=True)

    jj = lax.broadcasted_iota(jnp.int32, (MCHUNK, MCHUNK), 0)
    ii = lax.broadcasted_iota(jnp.int32, (MCHUNK, MCHUNK), 1)
    visible = (jj <= ii, jj >= ii)

    ones_rows = jnp.ones((EXTRA, MCHUNK), BF16)

    def weights_stage(c, slot):
        rows = pl.ds(pl.multiple_of(c * MCHUNK, MCHUNK), MCHUNK)
        cols = cols_ref[h, c]
        rt = series_rows(rt_ref, c)
        st = _dot_nt(k_ref[rows, :], q_ref[rows, :])
        for d in range(2):
            w = jnp.exp(jnp.where(visible[d], cols[:, d:d + 1] - row(rt, Q_G, d), NEG))
            at_ref[slot, :, d * MCHUNK:(d + 1) * MCHUNK] = (w * st).astype(BF16)

    def readout_stage(c, slot):
        rows = pl.ds(pl.multiple_of(c * MCHUNK, MCHUNK), MCHUNK)
        q = q_ref[rows, :]
        rt = series_rows(rt_ref, c)
        at = at_ref[slot]
        num = _dot(jnp.concatenate([vt_ref[c], ones_rows], axis=0), at)
        out = None
        for d in range(2):
            lanes = slice(d * MCHUNK, (d + 1) * MCHUNK)
            s_inter = row(rt, Q_S, d)
            inter = _dot_nt(cprev_ref[d, c], q)
            den = num[HEAD_DIM:HEAD_DIM + 1, lanes] + s_inter * inter[HEAD_DIM:HEAD_DIM + 1, :]
            inv = 1.0 / jnp.maximum(jnp.abs(den), row(rt, Q_FL, d))
            hd = (num[0:HEAD_DIM, lanes] + s_inter * inter[0:HEAD_DIM, :]) * inv
            out = hd if out is None else out + hd
        o_ref[c] = out.astype(o_ref.dtype)

    weights_stage(0, 0)

    def out_body(c, carry):
        readout_stage(c, c % 2)
        weights_stage(c + 1, (c + 1) % 2)
        return carry

    lax.fori_loop(0, n_lat - 1, out_body, 0, unroll=True)
    readout_stage(n_lat - 1, (n_lat - 1) % 2)


def _mlstm(q, k, vt, kx, vtx, gf, gi, gfx, gix):
    B, L, _ = q.shape
    Lx = kx.shape[1]
    n_lat, n_ctx = L // MCHUNK, Lx // MCHUNK
    n_ext = n_lat + n_ctx
    n_tile = gf.shape[2] // LANES + gfx.shape[2] // LANES
    head_cols = lambda n: pl.BlockSpec((None, n, HEAD_DIM), lambda b, h: (b, 0, h))
    head_rows = lambda n: pl.BlockSpec((None, n, HEAD_DIM, MCHUNK), lambda b, h: (b, 0, h, 0))
    gate_spec = lambda a: pl.BlockSpec((None,) + a.shape[1:], lambda b, h: (b, 0, 0))
    return pl.pallas_call(
        _mlstm_kernel,
        grid=(B, N_HEADS),
        in_specs=[
            head_cols(L), head_cols(L), head_rows(n_lat), head_cols(Lx), head_rows(n_ctx),
            gate_spec(gf), gate_spec(gi), gate_spec(gfx), gate_spec(gix),
        ],
        out_specs=head_rows(n_lat),
        out_shape=jax.ShapeDtypeStruct((B, n_lat, D_MODEL, MCHUNK), BF16),
        scratch_shapes=[
            pltpu.VMEM((N_HEADS, n_ext, MCHUNK, LANES), F32),
            pltpu.VMEM((n_tile * LANES, MCHUNK), F32),
            pltpu.VMEM((n_tile * LANES, LANES), F32),
            pltpu.VMEM((2, HEAD_DIM, HEAD_DIM), F32),
            pltpu.VMEM((2, 1, HEAD_DIM), F32),
            pltpu.VMEM((2, n_lat, HEAD_DIM + EXTRA, HEAD_DIM), BF16),
            pltpu.VMEM((2, MCHUNK, 2 * MCHUNK), BF16),
        ],
        compiler_params=pltpu.CompilerParams(
            dimension_semantics=("parallel", "arbitrary"), vmem_limit_bytes=VMEM_LIMIT),
        name="mlstm",
    )(q, k, vt, kx, vtx, gf, gi, gfx, gix)


def _rms(x, g):
    ms = jnp.mean(x * x, axis=-1, keepdims=True)
    return x * lax.rsqrt(ms + EPS) * g


def _merge_mlp_kernel(x_ref, p_ref, ht_ref, g1_ref, sh2_ref, sc2_ref, g2_ref,
                      n2_ref, nf_ref, gmh_ref, gsgu_ref, wa_ref, wb_ref, wo_ref, ws_ref, bst_ref,
                      w1_ref, w2_ref, o_ref, t_ref):
    tm = x_ref.shape[0]
    D = D_MODEL
    seg = lambda s: p_ref[:, s * D:(s + 1) * D].astype(F32)

    h_m = jnp.concatenate([ht_ref[mc].T for mc in range(tm // MCHUNK)], axis=0).astype(F32)
    hm = seg(0) * h_m
    gmh = gmh_ref[...]
    parts = []
    for h in range(N_HEADS):
        cols = slice(h * HEAD_DIM, (h + 1) * HEAD_DIM)
        parts.append(_rms(hm[:, cols], gmh[:, cols]).astype(BF16))
    ya = _dot(jnp.concatenate(parts, axis=1), wa_ref[...])

    u = _gelu(seg(1))
    vn = _rms(_gelu(seg(2)), gsgu_ref[...]).astype(BF16)
    bst = bst_ref[...]
    for cc in range(tm // CHUNK):
        rows = slice(cc * CHUNK, (cc + 1) * CHUNK)
        for g in range(N_GROUPS):
            cols = slice(g * GROUP_DIM, (g + 1) * GROUP_DIM)
            s = _dot(ws_ref[g], vn[rows, cols]) + bst[:, g:g + 1]
            t_ref[rows, cols] = (u[rows, cols] * s).astype(BF16)
    yb = _dot(t_ref[...], wb_ref[...])

    y = seg(3) * ya + seg(4) * yb
    mix = _dot(y.astype(BF16), wo_ref[...])
    x1 = x_ref[...] + g1_ref[...] * mix

    xn2 = (_rms(x1, n2_ref[...]) * (1.0 + sc2_ref[...]) + sh2_ref[...]).astype(BF16)
    ff = D_FF // 4
    acc = jnp.zeros((tm, D), F32)
    for kk in range(4):
        hmid = jnp.maximum(_dot(xn2, w1_ref[:, kk * ff:(kk + 1) * ff]), 0.0)
        acc = acc + _dot((hmid * hmid).astype(BF16), w2_ref[kk * ff:(kk + 1) * ff, :])
    x2 = x1 + g2_ref[...] * acc
    o_ref[...] = _rms(x2, nf_ref[...])


def _merge_mlp(x, pm, ht, mod3, norm2, norm_f, g_mh, g_sgu, w_a, w_b, w_out, w_s, b_st, w1, w2, tm):
    B, L, D = x.shape
    mod_spec = lambda k: pl.BlockSpec((None, 1, D), lambda b, i: (b, 0, k))
    return pl.pallas_call(
        _merge_mlp_kernel,
        grid=(B, L // tm),
        in_specs=[
            pl.BlockSpec((None, tm, D), lambda b, i: (b, i, 0)),
            pl.BlockSpec((None, tm, N_MERGE_SEG * D), lambda b, i: (b, i, 0)),
            pl.BlockSpec((None, tm // MCHUNK, D, MCHUNK), lambda b, i: (b, i, 0, 0)),
            mod_spec(2), mod_spec(3), mod_spec(4), mod_spec(5),
            _const_spec((1, D)), _const_spec((1, D)), _const_spec((1, D)), _const_spec((1, D)),
            _const_spec((D, D)), _const_spec((D, D)), _const_spec((D, D)),
            _const_spec(w_s.shape), _const_spec(b_st.shape),
            _const_spec((D, D_FF)), _const_spec((D_FF, D)),
        ],
        out_specs=pl.BlockSpec((None, tm, D), lambda b, i: (b, i, 0)),
        out_shape=jax.ShapeDtypeStruct((B, L, D), F32),
        scratch_shapes=[pltpu.VMEM((tm, D), BF16)],
        compiler_params=pltpu.CompilerParams(
            dimension_semantics=("parallel", "parallel"), vmem_limit_bytes=VMEM_LIMIT),
        name="merge_mlp",
    )(x, pm, ht, mod3, mod3, mod3, mod3, norm2, norm_f, g_mh, g_sgu, w_a, w_b, w_out, w_s, b_st, w1, w2)


def _gate_weights(w_gate, b_gate):
    D = w_gate.shape[0]

    def series(a, kind_of_dir):
        a4 = a.reshape(a.shape[:-1] + (4, N_HEADS))
        per_dir = jnp.stack([a4[..., kind_of_dir[0], :], a4[..., kind_of_dir[1], :]], axis=-1)
        dup = jnp.broadcast_to(per_dir[..., :, None, :], per_dir.shape[:-1] + (N_QUANT, 2))
        return dup.reshape(a.shape[:-1] + (SERIES,))

    wf, wi = series(w_gate, (1, 3)), series(w_gate, (0, 2))
    bf, bi = series(b_gate, (1, 3)), series(b_gate, (0, 2))
    eye = jnp.eye(CHUNKS_PER_TILE, dtype=w_gate.dtype)
    place = lambda w: jnp.einsum("ab,dk->adbk", eye, w).reshape(CHUNKS_PER_TILE, D, LANES)
    wgs = jnp.concatenate([place(wf), place(wi)], axis=2).astype(BF16)
    bgs = jnp.concatenate([jnp.tile(bf, CHUNKS_PER_TILE), jnp.tile(bi, CHUNKS_PER_TILE)])[None, :]
    return wgs, bgs


def kernel(x, c, ctx, c_ctx, norm1, norm2, w_mod, b_mod, w_in, conv_qk, b_gate, g_mh, w_a, w_s, b_s,
           g_sgu, w_b, w_out, w1, w2, norm_f):
    B, S, D = x.shape
    Lx = ctx.shape[1]
    assert D == D_MODEL and S % (MCHUNK * CHUNKS_PER_TILE) == 0 and S % MERGE_TM == 0
    assert Lx % MCHUNK == 0 and Lx // MCHUNK <= CHUNKS_PER_TILE
    assert w_mod.shape[0] == 1, "single-layer block"
    assert conv_qk.shape[1] == CONV_W
    W = D_MODEL
    off_g = 3 * W
    off_o = off_g + 4 * N_HEADS

    mod_rows = ((B + 1 + 7) // 8) * 8
    cc = jnp.zeros((mod_rows, D), F32).at[:B].set(c).at[B].set(c_ctx)
    mod = _modulation(cc, w_mod, b_mod)
    mod3 = mod.reshape(mod_rows, 1, 6 * D)

    wi = w_in[0]
    w_merge = wi[:, off_o:].astype(BF16)
    wq = wi[:, :W].astype(BF16)
    wk = wi[:, W:2 * W].astype(BF16)
    wvt = wi[:, 2 * W:off_g].T.astype(BF16)
    wgs, bgs = _gate_weights(wi[:, off_g:off_o], b_gate[0])

    n1 = norm1[0][None, :]
    conv = conv_qk[0]
    pm, q, k, vt, gf, gi = _inproj(x, mod3, lambda b: b, n1, wgs, bgs, conv, (w_merge, wq, wk, wvt), INPROJ_TM, True)
    kx, vtx, gfx, gix = _inproj(ctx, mod3, lambda b: B, n1, wgs, bgs, conv, (wk, wvt), Lx, False)

    ht = _mlstm(q, k, vt, kx, vtx, gf, gi, gfx, gix)

    return _merge_mlp(
        x, pm, ht, mod3, norm2[0][None, :], norm_f[None, :], g_mh[0][None, :], g_sgu[0][None, :],
        w_a[0].astype(BF16), w_b[0].astype(BF16), w_out[0].astype(BF16),
        w_s[0].astype(BF16), b_s[0].T, w1[0].astype(BF16), w2[0].astype(BF16), MERGE_TM)
```

```python
import jax
import jax.numpy as jnp
from jax import lax
from jax.experimental import pallas as pl
from jax.experimental.pallas import tpu as pltpu

D_MODEL = 1024
N_HEADS = 4
HEAD_DIM = D_MODEL // N_HEADS
CHUNK = 128
MCHUNK = 256
GROUP_DIM = 128
N_GROUPS = D_MODEL // GROUP_DIM
D_FF = 4 * D_MODEL
CONV_W = 3
NEG = -1e30
EPS = 1e-6
LANES = 128
HALO = 16
EXTRA = 16

N_QUANT = 4
ROWS_PER_HEAD = 2 * N_QUANT
SERIES = N_HEADS * ROWS_PER_HEAD
CHUNKS_PER_TILE = LANES // SERIES
Q_G, Q_E, Q_S, Q_FL = 0, 1, 2, 3

N_MERGE_SEG = 5

VMEM_LIMIT = 60000 * 1024
INPROJ_TM = 2 * MCHUNK
MERGE_TM = 512
MOD_TN = 1536

F32 = jnp.float32
BF16 = jnp.bfloat16


def _dot(a, b):
    return jnp.dot(a, b, preferred_element_type=F32)


def _dot_nt(a, b):
    return lax.dot_general(a, b, (((1,), (1,)), ((), ())), preferred_element_type=F32)


def _split_hi_lo(x):
    hi = x.astype(BF16)
    lo = (x - hi.astype(F32)).astype(BF16)
    return hi, lo


def _sigmoid(x):
    return 0.5 * jnp.tanh(0.5 * x) + 0.5


def _silu(x):
    half = 0.5 * x
    return half * jnp.tanh(half) + half


def _gelu(x):
    c = 0.7978845608028654
    half = 0.5 * x
    return half * jnp.tanh(x * (c + (c * 0.044715) * (x * x))) + half


MERGE_ACT = (_sigmoid, None, None, _sigmoid, _sigmoid)


def _const_spec(shape):
    nd = len(shape)
    return pl.BlockSpec(shape, lambda *_: (0,) * nd, pipeline_mode=pl.Buffered(1))


def _mod_kernel(c_ref, w_ref, b_ref, o_ref):
    s = _silu(c_ref[...])
    s_hi, s_lo = _split_hi_lo(s)
    w_hi, w_lo = _split_hi_lo(w_ref[...])
    acc = _dot(s_hi, w_hi) + _dot(s_hi, w_lo) + _dot(s_lo, w_hi)
    o_ref[...] = acc + b_ref[...]


def _modulation(cc, w_mod, b_mod):
    rows, d = cc.shape
    n = w_mod.shape[2]
    tn = MOD_TN
    return pl.pallas_call(
        _mod_kernel,
        grid=(n // tn,),
        in_specs=[
            pl.BlockSpec((rows, d), lambda j: (0, 0)),
            pl.BlockSpec((None, d, tn), lambda j: (0, 0, j)),
            pl.BlockSpec((1, tn), lambda j: (0, j)),
        ],
        out_specs=pl.BlockSpec((rows, tn), lambda j: (0, j)),
        out_shape=jax.ShapeDtypeStruct((rows, n), F32),
        compiler_params=pltpu.CompilerParams(
            dimension_semantics=("arbitrary",), vmem_limit_bytes=VMEM_LIMIT),
        name="modulation",
    )(cc, w_mod, b_mod)


def _log_sigmoid(x):
    return jnp.minimum(x, 0.0) - jnp.log1p(jnp.exp(-jnp.abs(x)))


def _make_inproj_kernel(tm, n_tiles, latent):
    n_cc = tm // CHUNK
    n_mc = tm // MCHUNK
    tiles_per_gate_tile = CHUNKS_PER_TILE // n_mc

    def kernel(*refs):
        if latent:
            (x_ref, xp_ref, xnx_ref, sh_ref, sc_ref, g_ref, wg_ref, bg_ref, cw_ref, wm_ref, wq_ref, wk_ref, wvt_ref,
             pm_ref, q_ref, k_ref, vt_ref, gf_ref, gi_ref, xe_ref, peq_ref, pek_ref) = refs
        else:
            (x_ref, xp_ref, xnx_ref, sh_ref, sc_ref, g_ref, wg_ref, bg_ref, cw_ref, wk_ref, wvt_ref,
             k_ref, vt_ref, gf_ref, gi_ref, xe_ref, pek_ref) = refs
        i = pl.program_id(1)

        def normed(x):
            ms = jnp.mean(x * x, axis=-1, keepdims=True)
            y = x * lax.rsqrt(ms + EPS) * g_ref[...]
            return y * (1.0 + sc_ref[...]) + sh_ref[...]

        xn = normed(x_ref[...]).astype(BF16)
        xe_ref[HALO:HALO + tm, :] = xn
        xe_ref[0:HALO, :] = jnp.where(i > 0, normed(xp_ref[...]), 0.0).astype(BF16)
        xe_ref[HALO + tm:, :] = jnp.where(i < n_tiles - 1, normed(xnx_ref[...]), 0.0).astype(BF16)

        pad = 8
        n_ext = CHUNK + 2 * pad

        def conv_silu(pe_ref, w_ref, cw, scale, out_ref):
            pe_ref[...] = _dot(xe_ref[...], w_ref[...])
            for cc in range(n_cc):
                r = HALO + cc * CHUNK
                x_ext = pe_ref[r - pad:r + CHUNK + pad, :]
                prev = pltpu.roll(x_ext, 1, 0)[pad:pad + CHUNK, :]
                nxt = pltpu.roll(x_ext, n_ext - 1, 0)[pad:pad + CHUNK, :]
                y = prev * cw[0:1, :] + x_ext[pad:pad + CHUNK, :] * cw[1:2, :] + nxt * cw[2:3, :]
                y = _silu(y)
                if scale != 1.0:
                    y = y * scale
                out_ref[cc * CHUNK:(cc + 1) * CHUNK, :] = y.astype(BF16)

        cw = cw_ref[...]
        if latent:
            conv_silu(peq_ref, wq_ref, cw[:, :D_MODEL], 1.0, q_ref)
        conv_silu(pek_ref, wk_ref, cw[:, D_MODEL:], HEAD_DIM ** -0.5, k_ref)

        if latent:
            for s in range(N_MERGE_SEG):
                cols = slice(s * D_MODEL, (s + 1) * D_MODEL)
                z = _dot(xn, wm_ref[:, cols])
                if MERGE_ACT[s] is not None:
                    z = MERGE_ACT[s](z)
                pm_ref[:, cols] = z.astype(BF16)

        sub = i % tiles_per_gate_tile
        acc = _dot(xn[0:MCHUNK], wg_ref[sub * n_mc])
        for mc in range(1, n_mc):
            acc = acc + _dot(xn[mc * MCHUNK:(mc + 1) * MCHUNK], wg_ref[sub * n_mc + mc])
        acc = acc + bg_ref[...]
        group = lax.broadcasted_iota(jnp.int32, (1, LANES), 1) // SERIES
        own = (group >= sub * n_mc) & (group < (sub + 1) * n_mc)
        new_f = jnp.where(own, _log_sigmoid(acc[:, :LANES]), 0.0)
        new_i = jnp.where(own, acc[:, LANES:], 0.0)

        vt = _dot_nt(wvt_ref[...], xn)
        for mc in range(n_mc):
            vt_ref[mc] = vt[:, mc * MCHUNK:(mc + 1) * MCHUNK].astype(BF16)

        @pl.when(sub == 0)
        def _():
            gf_ref[...] = new_f
            gi_ref[...] = new_i

        @pl.when(sub != 0)
        def _():
            gf_ref[...] += new_f
            gi_ref[...] += new_i

    return kernel


def _inproj(x, mod3, mod_row_of_batch, norm_g, wgs, bgs, conv_qk, weights, tm, latent):
    B, L, D = x.shape
    n_tiles = L // tm
    n_mc = tm // MCHUNK
    tiles_per_gate_tile = CHUNKS_PER_TILE // n_mc
    n_gate_tiles = -(-n_tiles // tiles_per_gate_tile)
    hb = tm // HALO
    n_hblk = L // HALO
    row = mod_row_of_batch
    tok = lambda width: pl.BlockSpec((None, tm, width), lambda b, i: (b, i, 0))
    tok_shape = lambda width: jax.ShapeDtypeStruct((B, L, width), BF16)
    widths = [w.shape[1] for w in weights[:-1]]
    gate_spec = pl.BlockSpec((None, MCHUNK, LANES), lambda b, i: (b, 0, i // tiles_per_gate_tile))
    gate_shape = jax.ShapeDtypeStruct((B, MCHUNK, n_gate_tiles * LANES), F32)
    vt_spec = pl.BlockSpec((None, n_mc, D, MCHUNK), lambda b, i: (b, i, 0, 0))
    vt_shape = jax.ShapeDtypeStruct((B, L // MCHUNK, D, MCHUNK), BF16)
    return pl.pallas_call(
        _make_inproj_kernel(tm, n_tiles, latent),
        grid=(B, n_tiles),
        in_specs=[
            pl.BlockSpec((None, tm, D), lambda b, i: (b, i, 0)),
            pl.BlockSpec((None, HALO, D), lambda b, i: (b, jnp.maximum(i * hb - 1, 0), 0)),
            pl.BlockSpec((None, HALO, D), lambda b, i: (b, jnp.minimum((i + 1) * hb, n_hblk - 1), 0)),
            pl.BlockSpec((None, 1, D), lambda b, i: (row(b), 0, 0)),
            pl.BlockSpec((None, 1, D), lambda b, i: (row(b), 0, 1)),
            _const_spec((1, D)), _const_spec(wgs.shape), _const_spec(bgs.shape), _const_spec(conv_qk.shape),
        ] + [_const_spec(w.shape) for w in weights],
        out_specs=[tok(w) for w in widths] + [vt_spec, gate_spec, gate_spec],
        out_shape=[tok_shape(w) for w in widths] + [vt_shape, gate_shape, gate_shape],
        scratch_shapes=[pltpu.VMEM((tm + 2 * HALO, D), BF16)]
        + [pltpu.VMEM((tm + 2 * HALO, D), F32)] * (2 if latent else 1),
        compiler_params=pltpu.CompilerParams(
            dimension_semantics=("parallel", "arbitrary"), vmem_limit_bytes=VMEM_LIMIT),
        name="inproj_latent" if latent else "inproj_context",
    )(x, x, x, mod3, mod3, norm_g, wgs, bgs, conv_qk, *weights)


def _cummax_rows(x, reverse):
    n = x.shape[0]
    rows = lax.broadcasted_iota(jnp.int32, x.shape, 0)
    k = 1
    while k < n:
        if reverse:
            shifted = jnp.where(rows < n - k, pltpu.roll(x, n - k, 0), NEG)
        else:
            shifted = jnp.where(rows >= k, pltpu.roll(x, k, 0), NEG)
        x = jnp.maximum(x, shifted)
        k *= 2
    return x


def _gate_prep(gf, gi, n_lat, n_ctx, cols_ref, rt_ref, sp_ref):
    n_ext = n_lat + n_ctx
    width = gf.shape[1]
    n_tile = width // LANES
    ii = lax.broadcasted_iota(jnp.int32, (MCHUNK, MCHUNK), 0)
    jj = lax.broadcasted_iota(jnp.int32, (MCHUNK, MCHUNK), 1)
    tril = jnp.where(ii >= jj, 1.0, 0.0).astype(BF16)
    triu = jnp.where(ii <= jj, 1.0, 0.0).astype(BF16)
    lane = lax.broadcasted_iota(jnp.int32, (1, width), 1)
    lane_bwd = (lane % 2) == 1
    lane_q = (lane // 2) % N_QUANT

    hi, lo = _split_hi_lo(gf)
    b = jnp.where(lane_bwd, _dot(triu, hi) + _dot(triu, lo), _dot(tril, hi) + _dot(tril, lo))
    tot = jnp.where(lane_bwd, b[0:1, :], b[MCHUNK - 1:MCHUNK, :])
    r = gi - b
    cm = jnp.where(lane_bwd, _cummax_rows(r, True), _cummax_rows(r, False))
    rmax = jnp.where(lane_bwd, cm[0:1, :], cm[MCHUNK - 1:MCHUNK, :])

    def to_rows(v):
        return [jnp.broadcast_to(v[:, t * LANES:(t + 1) * LANES], (LANES, LANES)).T for t in range(n_tile)]

    tot_t, rmax_t = to_rows(tot), to_rows(rmax)

    def slab(tiles, e):
        t, k = divmod(e, CHUNKS_PER_TILE)
        return tiles[t][k * SERIES:(k + 1) * SERIES, :]

    ctx_ids = list(range(n_lat, n_ext))
    orders = (ctx_ids + list(range(n_lat)), ctx_ids[::-1] + list(range(n_lat - 1, -1, -1)))
    m_prev, m_new = [{}, {}], [{}, {}]
    for d, order in enumerate(orders):
        m = jnp.zeros((SERIES, LANES), F32)
        for e in order:
            m_prev[d][e] = m
            tot_e = slab(tot_t, e)
            m = jnp.maximum(tot_e + m, tot_e + slab(rmax_t, e))
            m_new[d][e] = m
    row_bwd = (lax.broadcasted_iota(jnp.int32, (SERIES, LANES), 0) % 2) == 1
    mp_slabs = [jnp.where(row_bwd, m_prev[1][e], m_prev[0][e]) for e in range(n_ext)]
    mn_slabs = [jnp.where(row_bwd, m_new[1][e], m_new[0][e]) for e in range(n_ext)]
    sp_slabs = [jnp.exp(slab(tot_t, e) + mp_slabs[e] - mn_slabs[e]) for e in range(n_ext)]

    def tiles_of(slabs):
        pad = [jnp.zeros((SERIES, LANES), F32)] * (n_tile * CHUNKS_PER_TILE - n_ext)
        full = slabs + pad
        return [jnp.concatenate(full[t * CHUNKS_PER_TILE:(t + 1) * CHUNKS_PER_TILE], axis=0) for t in range(n_tile)]

    def to_lanes(slabs):
        return jnp.concatenate([t.T[0:1, :] for t in tiles_of(slabs)], axis=1)

    mp = to_lanes(mp_slabs)
    mn = to_lanes(mn_slabs)

    g = jnp.maximum(mp, cm)
    e_w = jnp.exp(tot + r - mn)
    s_inter = jnp.exp(mp - g)
    floor = jnp.exp(-(b + g))
    packed = jnp.where(lane_q == Q_G, g, jnp.where(lane_q == Q_E, e_w, jnp.where(lane_q == Q_S, s_inter, floor)))

    sp_tiles = tiles_of(sp_slabs)
    for t in range(n_tile):
        rt_ref[t * LANES:(t + 1) * LANES, :] = packed[:, t * LANES:(t + 1) * LANES].T
        sp_ref[t * LANES:(t + 1) * LANES, :] = sp_tiles[t]
    for e in range(n_ext):
        t, k = divmod(e, CHUNKS_PER_TILE)
        tile = r[:, t * LANES:(t + 1) * LANES]
        for h in range(N_HEADS):
            shift = k * SERIES + h * ROWS_PER_HEAD
            cols_ref[h, e] = tile if shift == 0 else pltpu.roll(tile, LANES - shift, 1)


def _mlstm_kernel(q_ref, k_ref, vt_ref, kx_ref, vtx_ref, gf_ref, gi_ref, gfx_ref, gix_ref,
                  o_ref, cols_ref, rt_ref, sp_ref, ct_ref, n_ref, cprev_ref, at_ref):
    n_lat = q_ref.shape[0] // MCHUNK
    n_ctx = kx_ref.shape[0] // MCHUNK
    h = pl.program_id(1)

    @pl.when(h == 0)
    def _():
        gf = jnp.concatenate([gf_ref[...], gfx_ref[...]], axis=1)
        gi = jnp.concatenate([gi_ref[...], gix_ref[...]], axis=1)
        _gate_prep(gf, gi, n_lat, n_ctx, cols_ref, rt_ref, sp_ref)

    def series_rows(ref, e):
        r0 = pl.multiple_of(e * SERIES + h * ROWS_PER_HEAD, ROWS_PER_HEAD)
        return ref[pl.ds(r0, ROWS_PER_HEAD), :]

    def row(rt, quantity, d):
        i = 2 * quantity + d
        return rt[i:i + 1, :]

    def state_step(d, e, k, vt, c_lat):
        r0 = pl.multiple_of(e * SERIES + h * ROWS_PER_HEAD, ROWS_PER_HEAD)
        rt_x = rt_ref[pl.ds(r0, EXTRA), :]
        s_row = series_rows(sp_ref, e)[d:d + 1, :]
        s2 = jnp.concatenate([s_row, s_row], axis=1)
        vet = (vt.astype(F32) * row(rt_x, Q_E, d)).astype(BF16)
        upd = _dot(jnp.concatenate([vet, rt_x.astype(BF16)], axis=0), k)
        i_e = HEAD_DIM + 2 * Q_E + d
        ek = upd[i_e:i_e + 1, :]
        ct_old = ct_ref[d]
        n_old = n_ref[d]
        if c_lat is not None:
            cprev_ref[d, c_lat, 0:HEAD_DIM, :] = ct_old.astype(BF16)
            cprev_ref[d, c_lat, HEAD_DIM:, :] = jnp.broadcast_to(n_old, (EXTRA, HEAD_DIM)).astype(BF16)
        ct_ref[d] = s2 * ct_old + upd[0:HEAD_DIM, :]
        n_ref[d] = s2 * n_old + ek

    ct_ref[...] = jnp.zeros_like(ct_ref)
    n_ref[...] = jnp.zeros_like(n_ref)
    for t in range(n_ctx):
        for d in range(2):
            cx = (n_ctx - 1 - t) if d else t
            rows = slice(cx * MCHUNK, (cx + 1) * MCHUNK)
            state_step(d, n_lat + cx, kx_ref[rows, :], vtx_ref[cx], None)

    def state_body(t, carry):
        for d in range(2):
            c = (n_lat - 1 - t) if d else t
            rows = pl.ds(pl.multiple_of(c * MCHUNK, MCHUNK), MCHUNK)
            state_step(d, c, k_ref[rows, :], vt_ref[c], c)
        return carry

    lax.fori_loop(0, n_lat, state_body, 0, unroll=True)

    jj = lax.broadcasted_iota(jnp.int32, (MCHUNK, MCHUNK), 0)
    ii = lax.broadcasted_iota(jnp.int32, (MCHUNK, MCHUNK), 1)
    visible = (jj <= ii, jj >= ii)

    ones_rows = jnp.ones((EXTRA, MCHUNK), BF16)

    def weights_stage(c, slot):
        rows = pl.ds(pl.multiple_of(c * MCHUNK, MCHUNK), MCHUNK)
        cols = cols_ref[h, c]
        rt = series_rows(rt_ref, c)
        st = _dot_nt(k_ref[rows, :], q_ref[rows, :])
        for d in range(2):
            w = jnp.exp(jnp.where(visible[d], cols[:, d:d + 1] - row(rt, Q_G, d), NEG))
            at_ref[slot, :, d * MCHUNK:(d + 1) * MCHUNK] = (w * st).astype(BF16)

    def readout_stage(c, slot):
        rows = pl.ds(pl.multiple_of(c * MCHUNK, MCHUNK), MCHUNK)
        q = q_ref[rows, :]
        rt = series_rows(rt_ref, c)
        at = at_ref[slot]
        num = _dot(jnp.concatenate([vt_ref[c], ones_rows], axis=0), at)
        out = None
        for d in range(2):
            lanes = slice(d * MCHUNK, (d + 1) * MCHUNK)
            s_inter = row(rt, Q_S, d)
            inter = _dot_nt(cprev_ref[d, c], q)
            den = num[HEAD_DIM:HEAD_DIM + 1, lanes] + s_inter * inter[HEAD_DIM:HEAD_DIM + 1, :]
            inv = 1.0 / jnp.maximum(jnp.abs(den), row(rt, Q_FL, d))
            hd = (num[0:HEAD_DIM, lanes] + s_inter * inter[0:HEAD_DIM, :]) * inv
            out = hd if out is None else out + hd
        o_ref[c] = out.astype(o_ref.dtype)

    weights_stage(0, 0)

    def out_body(c, carry):
        readout_stage(c, c % 2)
        weights_stage(c + 1, (c + 1) % 2)
        return carry

    lax.fori_loop(0, n_lat - 1, out_body, 0, unroll=True)
    readout_stage(n_lat - 1, (n_lat - 1) % 2)


def _mlstm(q, k, vt, kx, vtx, gf, gi, gfx, gix):
    B, L, _ = q.shape
    Lx = kx.shape[1]
    n_lat, n_ctx = L // MCHUNK, Lx // MCHUNK
    n_ext = n_lat + n_ctx
    n_tile = gf.shape[2] // LANES + gfx.shape[2] // LANES
    assert (n_ext - 1) * SERIES + (N_HEADS - 1) * ROWS_PER_HEAD + EXTRA <= n_tile * LANES
    head_cols = lambda n: pl.BlockSpec((None, n, HEAD_DIM), lambda b, h: (b, 0, h))
    head_rows = lambda n: pl.BlockSpec((None, n, HEAD_DIM, MCHUNK), lambda b, h: (b, 0, h, 0))
    gate_spec = lambda a: pl.BlockSpec((None,) + a.shape[1:], lambda b, h: (b, 0, 0))
    return pl.pallas_call(
        _mlstm_kernel,
        grid=(B, N_HEADS),
        in_specs=[
            head_cols(L), head_cols(L), head_rows(n_lat), head_cols(Lx), head_rows(n_ctx),
            gate_spec(gf), gate_spec(gi), gate_spec(gfx), gate_spec(gix),
        ],
        out_specs=head_rows(n_lat),
        out_shape=jax.ShapeDtypeStruct((B, n_lat, D_MODEL, MCHUNK), BF16),
        scratch_shapes=[
            pltpu.VMEM((N_HEADS, n_ext, MCHUNK, LANES), F32),
            pltpu.VMEM((n_tile * LANES, MCHUNK), F32),
            pltpu.VMEM((n_tile * LANES, LANES), F32),
            pltpu.VMEM((2, HEAD_DIM, HEAD_DIM), F32),
            pltpu.VMEM((2, 1, HEAD_DIM), F32),
            pltpu.VMEM((2, n_lat, HEAD_DIM + EXTRA, HEAD_DIM), BF16),
            pltpu.VMEM((2, MCHUNK, 2 * MCHUNK), BF16),
        ],
        compiler_params=pltpu.CompilerParams(
            dimension_semantics=("parallel", "arbitrary"), vmem_limit_bytes=VMEM_LIMIT),
        name="mlstm",
    )(q, k, vt, kx, vtx, gf, gi, gfx, gix)


def _rms(x, g):
    ms = jnp.mean(x * x, axis=-1, keepdims=True)
    return x * lax.rsqrt(ms + EPS) * g


def _merge_mlp_kernel(x_ref, p_ref, ht_ref, g1_ref, sh2_ref, sc2_ref, g2_ref,
                      n2_ref, nf_ref, gmh_ref, gsgu_ref, wa_ref, wb_ref, wo_ref, ws_ref, bst_ref,
                      w1_ref, w2_ref, o_ref, t_ref):
    tm = x_ref.shape[0]
    D = D_MODEL
    seg = lambda s: p_ref[:, s * D:(s + 1) * D].astype(F32)

    h_m = jnp.concatenate([ht_ref[mc].T for mc in range(tm // MCHUNK)], axis=0).astype(F32)
    hm = seg(0) * h_m
    gmh = gmh_ref[...]
    parts = []
    for h in range(N_HEADS):
        cols = slice(h * HEAD_DIM, (h + 1) * HEAD_DIM)
        parts.append(_rms(hm[:, cols], gmh[:, cols]).astype(BF16))
    ya = _dot(jnp.concatenate(parts, axis=1), wa_ref[...])

    u = _gelu(seg(1))
    vn = _rms(_gelu(seg(2)), gsgu_ref[...]).astype(BF16)
    bst = bst_ref[...]
    for cc in range(tm // CHUNK):
        rows = slice(cc * CHUNK, (cc + 1) * CHUNK)
        for g in range(N_GROUPS):
            cols = slice(g * GROUP_DIM, (g + 1) * GROUP_DIM)
            s = _dot(ws_ref[g], vn[rows, cols]) + bst[:, g:g + 1]
            t_ref[rows, cols] = (u[rows, cols] * s).astype(BF16)
    yb = _dot(t_ref[...], wb_ref[...])

    y = seg(3) * ya + seg(4) * yb
    mix = _dot(y.astype(BF16), wo_ref[...])
    x1 = x_ref[...] + g1_ref[...] * mix

    xn2 = (_rms(x1, n2_ref[...]) * (1.0 + sc2_ref[...]) + sh2_ref[...]).astype(BF16)
    ff = D_FF // 4
    acc = jnp.zeros((tm, D), F32)
    for kk in range(4):
        hmid = jnp.maximum(_dot(xn2, w1_ref[:, kk * ff:(kk + 1) * ff]), 0.0)
        acc = acc + _dot((hmid * hmid).astype(BF16), w2_ref[kk * ff:(kk + 1) * ff, :])
    x2 = x1 + g2_ref[...] * acc
    o_ref[...] = _rms(x2, nf_ref[...])


def _merge_mlp(x, pm, ht, mod3, norm2, norm_f, g_mh, g_sgu, w_a, w_b, w_out, w_s, b_st, w1, w2, tm):
    B, L, D = x.shape
    mod_spec = lambda k: pl.BlockSpec((None, 1, D), lambda b, i: (b, 0, k))
    return pl.pallas_call(
        _merge_mlp_kernel,
        grid=(B, L // tm),
        in_specs=[
            pl.BlockSpec((None, tm, D), lambda b, i: (b, i, 0)),
            pl.BlockSpec((None, tm, N_MERGE_SEG * D), lambda b, i: (b, i, 0)),
            pl.BlockSpec((None, tm // MCHUNK, D, MCHUNK), lambda b, i: (b, i, 0, 0)),
            mod_spec(2), mod_spec(3), mod_spec(4), mod_spec(5),
            _const_spec((1, D)), _const_spec((1, D)), _const_spec((1, D)), _const_spec((1, D)),
            _const_spec((D, D)), _const_spec((D, D)), _const_spec((D, D)),
            _const_spec(w_s.shape), _const_spec(b_st.shape),
            _const_spec((D, D_FF)), _const_spec((D_FF, D)),
        ],
        out_specs=pl.BlockSpec((None, tm, D), lambda b, i: (b, i, 0)),
        out_shape=jax.ShapeDtypeStruct((B, L, D), F32),
        scratch_shapes=[pltpu.VMEM((tm, D), BF16)],
        compiler_params=pltpu.CompilerParams(
            dimension_semantics=("parallel", "parallel"), vmem_limit_bytes=VMEM_LIMIT),
        name="merge_mlp",
    )(x, pm, ht, mod3, mod3, mod3, mod3, norm2, norm_f, g_mh, g_sgu, w_a, w_b, w_out, w_s, b_st, w1, w2)


def _gate_weights(w_gate, b_gate):
    D = w_gate.shape[0]

    def series(a, kind_of_dir):
        a4 = a.reshape(a.shape[:-1] + (4, N_HEADS))
        per_dir = jnp.stack([a4[..., kind_of_dir[0], :], a4[..., kind_of_dir[1], :]], axis=-1)
        dup = jnp.broadcast_to(per_dir[..., :, None, :], per_dir.shape[:-1] + (N_QUANT, 2))
        return dup.reshape(a.shape[:-1] + (SERIES,))

    wf, wi = series(w_gate, (1, 3)), series(w_gate, (0, 2))
    bf, bi = series(b_gate, (1, 3)), series(b_gate, (0, 2))
    eye = jnp.eye(CHUNKS_PER_TILE, dtype=w_gate.dtype)
    place = lambda w: jnp.einsum("ab,dk->adbk", eye, w).reshape(CHUNKS_PER_TILE, D, LANES)
    wgs = jnp.concatenate([place(wf), place(wi)], axis=2).astype(BF16)
    bgs = jnp.concatenate([jnp.tile(bf, CHUNKS_PER_TILE), jnp.tile(bi, CHUNKS_PER_TILE)])[None, :]
    return wgs, bgs


def kernel(x, c, ctx, c_ctx, norm1, norm2, w_mod, b_mod, w_in, conv_qk, b_gate, g_mh, w_a, w_s, b_s,
           g_sgu, w_b, w_out, w1, w2, norm_f):
    B, S, D = x.shape
    Lx = ctx.shape[1]
    assert D == D_MODEL and S % (MCHUNK * CHUNKS_PER_TILE) == 0 and S % MERGE_TM == 0
    assert Lx % MCHUNK == 0 and Lx // MCHUNK <= CHUNKS_PER_TILE
    assert w_mod.shape[0] == 1, "single-layer block"
    assert conv_qk.shape[1] == CONV_W
    W = D_MODEL
    off_g = 3 * W
    off_o = off_g + 4 * N_HEADS

    mod_rows = ((B + 1 + 7) // 8) * 8
    cc = jnp.zeros((mod_rows, D), F32).at[:B].set(c).at[B].set(c_ctx)
    mod = _modulation(cc, w_mod, b_mod)
    mod3 = mod.reshape(mod_rows, 1, 6 * D)

    wi = w_in[0]
    w_merge = wi[:, off_o:].astype(BF16)
    wq = wi[:, :W].astype(BF16)
    wk = wi[:, W:2 * W].astype(BF16)
    wvt = wi[:, 2 * W:off_g].T.astype(BF16)
    wgs, bgs = _gate_weights(wi[:, off_g:off_o], b_gate[0])

    n1 = norm1[0][None, :]
    conv = conv_qk[0]
    pm, q, k, vt, gf, gi = _inproj(x, mod3, lambda b: b, n1, wgs, bgs, conv, (w_merge, wq, wk, wvt), INPROJ_TM, True)
    kx, vtx, gfx, gix = _inproj(ctx, mod3, lambda b: B, n1, wgs, bgs, conv, (wk, wvt), Lx, False)

    ht = _mlstm(q, k, vt, kx, vtx, gf, gi, gfx, gix)

    return _merge_mlp(
        x, pm, ht, mod3, norm2[0][None, :], norm_f[None, :], g_mh[0][None, :], g_sgu[0][None, :],
        w_a[0].astype(BF16), w_b[0].astype(BF16), w_out[0].astype(BF16),
        w_s[0].astype(BF16), b_s[0].T, w1[0].astype(BF16), w2[0].astype(BF16), MERGE_TM)
```

```python
import jax
import jax.numpy as jnp
from jax import lax
from jax.experimental import pallas as pl
from jax.experimental.pallas import tpu as pltpu

D_MODEL = 1024
N_HEADS = 4
HEAD_DIM = D_MODEL // N_HEADS
CHUNK = 128
MCHUNK = 256
GROUP_DIM = 128
N_GROUPS = D_MODEL // GROUP_DIM
D_FF = 4 * D_MODEL
CONV_W = 3
NEG = -1e30
EPS = 1e-6
LANES = 128
HALO = 16
EXTRA = 16

N_QUANT = 4
ROWS_PER_HEAD = 2 * N_QUANT
SERIES = N_HEADS * ROWS_PER_HEAD
CHUNKS_PER_TILE = LANES // SERIES
Q_G, Q_E, Q_S, Q_FL = 0, 1, 2, 3

N_MERGE_SEG = 5

VMEM_LIMIT = 60000 * 1024
INPROJ_TM = 2 * MCHUNK
MERGE_TM = 512
MOD_TN = 1536

F32 = jnp.float32
BF16 = jnp.bfloat16


def _dot(a, b):
    return jnp.dot(a, b, preferred_element_type=F32)


def _dot_nt(a, b):
    return lax.dot_general(a, b, (((1,), (1,)), ((), ())), preferred_element_type=F32)


def _split_hi_lo(x):
    hi = x.astype(BF16)
    lo = (x - hi.astype(F32)).astype(BF16)
    return hi, lo


def _sigmoid(x):
    return 0.5 * jnp.tanh(0.5 * x) + 0.5


def _silu(x):
    half = 0.5 * x
    return half * jnp.tanh(half) + half


def _gelu(x):
    c = 0.7978845608028654
    half = 0.5 * x
    return half * jnp.tanh(x * (c + (c * 0.044715) * (x * x))) + half


MERGE_ACT = (_sigmoid, None, None, _sigmoid, _sigmoid)


def _const_spec(shape):
    nd = len(shape)
    return pl.BlockSpec(shape, lambda *_: (0,) * nd, pipeline_mode=pl.Buffered(1))


def _mod_kernel(c_ref, w_ref, b_ref, o_ref):
    s = _silu(c_ref[...])
    s_hi, s_lo = _split_hi_lo(s)
    w_hi, w_lo = _split_hi_lo(w_ref[...])
    acc = _dot(s_hi, w_hi) + _dot(s_hi, w_lo) + _dot(s_lo, w_hi)
    o_ref[...] = acc + b_ref[...]


def _modulation(cc, w_mod, b_mod):
    rows, d = cc.shape
    n = w_mod.shape[2]
    tn = MOD_TN
    return pl.pallas_call(
        _mod_kernel,
        grid=(n // tn,),
        in_specs=[
            pl.BlockSpec((rows, d), lambda j: (0, 0)),
            pl.BlockSpec((None, d, tn), lambda j: (0, 0, j)),
            pl.BlockSpec((1, tn), lambda j: (0, j)),
        ],
        out_specs=pl.BlockSpec((rows, tn), lambda j: (0, j)),
        out_shape=jax.ShapeDtypeStruct((rows, n), F32),
        compiler_params=pltpu.CompilerParams(
            dimension_semantics=("arbitrary",), vmem_limit_bytes=VMEM_LIMIT),
        name="modulation",
    )(cc, w_mod, b_mod)


def _log_sigmoid(x):
    return jnp.minimum(x, 0.0) - jnp.log1p(jnp.exp(-jnp.abs(x)))


def _make_inproj_kernel(tm, n_tiles, latent):
    n_cc = tm // CHUNK
    n_mc = tm // MCHUNK
    tiles_per_gate_tile = CHUNKS_PER_TILE // n_mc

    def kernel(*refs):
        if latent:
            (x_ref, xp_ref, xnx_ref, sh_ref, sc_ref, g_ref, wg_ref, bg_ref, cw_ref, wm_ref, wq_ref, wk_ref, wvt_ref,
             pm_ref, q_ref, k_ref, vt_ref, gf_ref, gi_ref, xe_ref, peq_ref, pek_ref) = refs
        else:
            (x_ref, xp_ref, xnx_ref, sh_ref, sc_ref, g_ref, wg_ref, bg_ref, cw_ref, wk_ref, wvt_ref,
             k_ref, vt_ref, gf_ref, gi_ref, xe_ref, pek_ref) = refs
        i = pl.program_id(1)

        def normed(x):
            ms = jnp.mean(x * x, axis=-1, keepdims=True)
            y = x * lax.rsqrt(ms + EPS) * g_ref[...]
            return y * (1.0 + sc_ref[...]) + sh_ref[...]

        xn = normed(x_ref[...]).astype(BF16)
        xe_ref[HALO:HALO + tm, :] = xn
        xe_ref[0:HALO, :] = jnp.where(i > 0, normed(xp_ref[...]), 0.0).astype(BF16)
        xe_ref[HALO + tm:, :] = jnp.where(i < n_tiles - 1, normed(xnx_ref[...]), 0.0).astype(BF16)

        pad = 8
        n_ext = CHUNK + 2 * pad

        def conv_silu(pe_ref, w_ref, cw, scale, out_ref):
            pe_ref[...] = _dot(xe_ref[...], w_ref[...])
            for cc in range(n_cc):
                r = HALO + cc * CHUNK
                x_ext = pe_ref[r - pad:r + CHUNK + pad, :]
                prev = pltpu.roll(x_ext, 1, 0)[pad:pad + CHUNK, :]
                nxt = pltpu.roll(x_ext, n_ext - 1, 0)[pad:pad + CHUNK, :]
                y = prev * cw[0:1, :] + x_ext[pad:pad + CHUNK, :] * cw[1:2, :] + nxt * cw[2:3, :]
                y = _silu(y)
                if scale != 1.0:
                    y = y * scale
                out_ref[cc * CHUNK:(cc + 1) * CHUNK, :] = y.astype(BF16)

        cw = cw_ref[...]
        if latent:
            conv_silu(peq_ref, wq_ref, cw[:, :D_MODEL], 1.0, q_ref)
        conv_silu(pek_ref, wk_ref, cw[:, D_MODEL:], HEAD_DIM ** -0.5, k_ref)

        if latent:
            for s in range(N_MERGE_SEG):
                cols = slice(s * D_MODEL, (s + 1) * D_MODEL)
                z = _dot(xn, wm_ref[:, cols])
                if MERGE_ACT[s] is not None:
                    z = MERGE_ACT[s](z)
                pm_ref[:, cols] = z.astype(BF16)

        sub = i % tiles_per_gate_tile
        acc = _dot(xn[0:MCHUNK], wg_ref[sub * n_mc])
        for mc in range(1, n_mc):
            acc = acc + _dot(xn[mc * MCHUNK:(mc + 1) * MCHUNK], wg_ref[sub * n_mc + mc])
        acc = acc + bg_ref[...]
        group = lax.broadcasted_iota(jnp.int32, (1, LANES), 1) // SERIES
        own = (group >= sub * n_mc) & (group < (sub + 1) * n_mc)
        new_f = jnp.where(own, _log_sigmoid(acc[:, :LANES]), 0.0)
        new_i = jnp.where(own, acc[:, LANES:], 0.0)

        vt = _dot_nt(wvt_ref[...], xn)
        for mc in range(n_mc):
            vt_ref[mc] = vt[:, mc * MCHUNK:(mc + 1) * MCHUNK].astype(BF16)

        @pl.when(sub == 0)
        def _():
            gf_ref[...] = new_f
            gi_ref[...] = new_i

        @pl.when(sub != 0)
        def _():
            gf_ref[...] += new_f
            gi_ref[...] += new_i

    return kernel


def _inproj(x, mod3, mod_row_of_batch, norm_g, wgs, bgs, conv_qk, weights, tm, latent):
    B, L, D = x.shape
    n_tiles = L // tm
    n_mc = tm // MCHUNK
    tiles_per_gate_tile = CHUNKS_PER_TILE // n_mc
    n_gate_tiles = -(-n_tiles // tiles_per_gate_tile)
    hb = tm // HALO
    n_hblk = L // HALO
    row = mod_row_of_batch
    tok = lambda width: pl.BlockSpec((None, tm, width), lambda b, i: (b, i, 0))
    tok_shape = lambda width: jax.ShapeDtypeStruct((B, L, width), BF16)
    widths = [w.shape[1] for w in weights[:-1]]
    gate_spec = pl.BlockSpec((None, MCHUNK, LANES), lambda b, i: (b, 0, i // tiles_per_gate_tile))
    gate_shape = jax.ShapeDtypeStruct((B, MCHUNK, n_gate_tiles * LANES), F32)
    vt_spec = pl.BlockSpec((None, n_mc, D, MCHUNK), lambda b, i: (b, i, 0, 0))
    vt_shape = jax.ShapeDtypeStruct((B, L // MCHUNK, D, MCHUNK), BF16)
    return pl.pallas_call(
        _make_inproj_kernel(tm, n_tiles, latent),
        grid=(B, n_tiles),
        in_specs=[
            pl.BlockSpec((None, tm, D), lambda b, i: (b, i, 0)),
            pl.BlockSpec((None, HALO, D), lambda b, i: (b, jnp.maximum(i * hb - 1, 0), 0)),
            pl.BlockSpec((None, HALO, D), lambda b, i: (b, jnp.minimum((i + 1) * hb, n_hblk - 1), 0)),
            pl.BlockSpec((None, 1, D), lambda b, i: (row(b), 0, 0)),
            pl.BlockSpec((None, 1, D), lambda b, i: (row(b), 0, 1)),
            _const_spec((1, D)), _const_spec(wgs.shape), _const_spec(bgs.shape), _const_spec(conv_qk.shape),
        ] + [_const_spec(w.shape) for w in weights],
        out_specs=[tok(w) for w in widths] + [vt_spec, gate_spec, gate_spec],
        out_shape=[tok_shape(w) for w in widths] + [vt_shape, gate_shape, gate_shape],
        scratch_shapes=[pltpu.VMEM((tm + 2 * HALO, D), BF16)]
        + [pltpu.VMEM((tm + 2 * HALO, D), F32)] * (2 if latent else 1),
        compiler_params=pltpu.CompilerParams(
            dimension_semantics=("parallel", "arbitrary"), vmem_limit_bytes=VMEM_LIMIT),
        name="inproj_latent" if latent else "inproj_context",
    )(x, x, x, mod3, mod3, norm_g, wgs, bgs, conv_qk, *weights)


def _cummax_rows(x, reverse):
    n = x.shape[0]
    rows = lax.broadcasted_iota(jnp.int32, x.shape, 0)
    k = 1
    while k < n:
        if reverse:
            shifted = jnp.where(rows < n - k, pltpu.roll(x, n - k, 0), NEG)
        else:
            shifted = jnp.where(rows >= k, pltpu.roll(x, k, 0), NEG)
        x = jnp.maximum(x, shifted)
        k *= 2
    return x


def _gate_prep(gf, gi, n_lat, n_ctx, r_ref, rt_ref, sp_ref):
    n_ext = n_lat + n_ctx
    width = gf.shape[1]
    n_tile = width // LANES
    ii = lax.broadcasted_iota(jnp.int32, (MCHUNK, MCHUNK), 0)
    jj = lax.broadcasted_iota(jnp.int32, (MCHUNK, MCHUNK), 1)
    tril = jnp.where(ii >= jj, 1.0, 0.0).astype(BF16)
    triu = jnp.where(ii <= jj, 1.0, 0.0).astype(BF16)
    lane = lax.broadcasted_iota(jnp.int32, (1, width), 1)
    lane_bwd = (lane % 2) == 1
    lane_q = (lane // 2) % N_QUANT

    hi, lo = _split_hi_lo(gf)
    b = jnp.where(lane_bwd, _dot(triu, hi) + _dot(triu, lo), _dot(tril, hi) + _dot(tril, lo))
    tot = jnp.where(lane_bwd, b[0:1, :], b[MCHUNK - 1:MCHUNK, :])
    r = gi - b
    cm = jnp.where(lane_bwd, _cummax_rows(r, True), _cummax_rows(r, False))
    rmax = jnp.where(lane_bwd, cm[0:1, :], cm[MCHUNK - 1:MCHUNK, :])

    def to_rows(v):
        return [jnp.broadcast_to(v[:, t * LANES:(t + 1) * LANES], (LANES, LANES)).T for t in range(n_tile)]

    tot_t, rmax_t = to_rows(tot), to_rows(rmax)

    def slab(tiles, e):
        t, k = divmod(e, CHUNKS_PER_TILE)
        return tiles[t][k * SERIES:(k + 1) * SERIES, :]

    ctx_ids = list(range(n_lat, n_ext))
    orders = (ctx_ids + list(range(n_lat)), ctx_ids[::-1] + list(range(n_lat - 1, -1, -1)))
    m_prev, m_new = [{}, {}], [{}, {}]
    for d, order in enumerate(orders):
        m = jnp.zeros((SERIES, LANES), F32)
        for e in order:
            m_prev[d][e] = m
            tot_e = slab(tot_t, e)
            m = jnp.maximum(tot_e + m, tot_e + slab(rmax_t, e))
            m_new[d][e] = m
    row_bwd = (lax.broadcasted_iota(jnp.int32, (SERIES, LANES), 0) % 2) == 1
    mp_slabs = [jnp.where(row_bwd, m_prev[1][e], m_prev[0][e]) for e in range(n_ext)]
    mn_slabs = [jnp.where(row_bwd, m_new[1][e], m_new[0][e]) for e in range(n_ext)]
    sp_slabs = [jnp.exp(slab(tot_t, e) + mp_slabs[e] - mn_slabs[e]) for e in range(n_ext)]

    def tiles_of(slabs):
        pad = [jnp.zeros((SERIES, LANES), F32)] * (n_tile * CHUNKS_PER_TILE - n_ext)
        full = slabs + pad
        return [jnp.concatenate(full[t * CHUNKS_PER_TILE:(t + 1) * CHUNKS_PER_TILE], axis=0) for t in range(n_tile)]

    def to_lanes(slabs):
        return jnp.concatenate([t.T[0:1, :] for t in tiles_of(slabs)], axis=1)

    mp = to_lanes(mp_slabs)
    mn = to_lanes(mn_slabs)

    g = jnp.maximum(mp, cm)
    e_w = jnp.exp(tot + r - mn)
    s_inter = jnp.exp(mp - g)
    floor = jnp.exp(-(b + g))
    packed = jnp.where(lane_q == Q_G, g, jnp.where(lane_q == Q_E, e_w, jnp.where(lane_q == Q_S, s_inter, floor)))

    sp_tiles = tiles_of(sp_slabs)
    for t in range(n_tile):
        rt_ref[t * LANES:(t + 1) * LANES, :] = packed[:, t * LANES:(t + 1) * LANES].T
        sp_ref[t * LANES:(t + 1) * LANES, :] = sp_tiles[t]
    r_ref[...] = r


def _mlstm_kernel(q_ref, k_ref, vt_ref, kx_ref, vtx_ref, gf_ref, gi_ref, gfx_ref, gix_ref,
                  o_ref, r_ref, rt_ref, sp_ref, ct_ref, n_ref, cprev_ref, at_ref):
    n_lat = q_ref.shape[0] // MCHUNK
    n_ctx = kx_ref.shape[0] // MCHUNK

    gf = jnp.concatenate([gf_ref[...], gfx_ref[...]], axis=1)
    gi = jnp.concatenate([gi_ref[...], gix_ref[...]], axis=1)
    _gate_prep(gf, gi, n_lat, n_ctx, r_ref, rt_ref, sp_ref)

    jj = lax.broadcasted_iota(jnp.int32, (MCHUNK, MCHUNK), 0)
    ii = lax.broadcasted_iota(jnp.int32, (MCHUNK, MCHUNK), 1)
    visible = (jj <= ii, jj >= ii)
    ones_rows = jnp.ones((EXTRA, MCHUNK), BF16)

    def row(rt, quantity, d):
        i = 2 * quantity + d
        return rt[i:i + 1, :]

    for h in range(N_HEADS):
        hs = slice(h * HEAD_DIM, (h + 1) * HEAD_DIM)
        par = h % 2
        series0 = lambda e, h=h: e * SERIES + h * ROWS_PER_HEAD

        def state_step(d, e, k, vt, c_lat):
            rt_x = rt_ref[series0(e):series0(e) + EXTRA, :]
            s_row = sp_ref[series0(e) + d:series0(e) + d + 1, :]
            s2 = jnp.concatenate([s_row, s_row], axis=1)
            vet = (vt.astype(F32) * row(rt_x, Q_E, d)).astype(BF16)
            upd = _dot(jnp.concatenate([vet, rt_x.astype(BF16)], axis=0), k)
            i_e = HEAD_DIM + 2 * Q_E + d
            ek = upd[i_e:i_e + 1, :]
            ct_old = ct_ref[par, d]
            n_old = n_ref[par, d]
            if c_lat is not None:
                cprev_ref[par, d, c_lat, 0:HEAD_DIM, :] = ct_old.astype(BF16)
                cprev_ref[par, d, c_lat, HEAD_DIM:, :] = jnp.broadcast_to(n_old, (EXTRA, HEAD_DIM)).astype(BF16)
            ct_ref[par, d] = s2 * ct_old + upd[0:HEAD_DIM, :]
            n_ref[par, d] = s2 * n_old + ek

        ct_ref[par] = jnp.zeros(ct_ref.shape[1:], F32)
        n_ref[par] = jnp.zeros(n_ref.shape[1:], F32)
        for t in range(n_ctx):
            for d in range(2):
                cx = (n_ctx - 1 - t) if d else t
                rows = slice(cx * MCHUNK, (cx + 1) * MCHUNK)
                state_step(d, n_lat + cx, kx_ref[rows, hs], vtx_ref[cx, hs, :], None)
        for t in range(n_lat):
            for d in range(2):
                c = (n_lat - 1 - t) if d else t
                rows = slice(c * MCHUNK, (c + 1) * MCHUNK)
                state_step(d, c, k_ref[rows, hs], vt_ref[c, hs, :], c)

        def weights_stage(c, slot):
            rows = slice(c * MCHUNK, (c + 1) * MCHUNK)
            rt = rt_ref[series0(c):series0(c) + ROWS_PER_HEAD, :]
            st = _dot_nt(k_ref[rows, hs], q_ref[rows, hs])
            for d in range(2):
                r_col = r_ref[:, series0(c) + d:series0(c) + d + 1]
                w = jnp.exp(jnp.where(visible[d], r_col - row(rt, Q_G, d), NEG))
                at_ref[par, slot, :, d * MCHUNK:(d + 1) * MCHUNK] = (w * st).astype(BF16)

        def readout_stage(c, slot):
            rows = slice(c * MCHUNK, (c + 1) * MCHUNK)
            q = q_ref[rows, hs]
            rt = rt_ref[series0(c):series0(c) + ROWS_PER_HEAD, :]
            at = at_ref[par, slot]
            num = _dot(jnp.concatenate([vt_ref[c, hs, :], ones_rows], axis=0), at)
            out = None
            for d in range(2):
                lanes = slice(d * MCHUNK, (d + 1) * MCHUNK)
                s_inter = row(rt, Q_S, d)
                inter = _dot_nt(cprev_ref[par, d, c], q)
                den = num[HEAD_DIM:HEAD_DIM + 1, lanes] + s_inter * inter[HEAD_DIM:HEAD_DIM + 1, :]
                inv = 1.0 / jnp.maximum(jnp.abs(den), row(rt, Q_FL, d))
                hd = (num[0:HEAD_DIM, lanes] + s_inter * inter[0:HEAD_DIM, :]) * inv
                out = hd if out is None else out + hd
            o_ref[c, hs, :] = out.astype(o_ref.dtype)

        weights_stage(0, 0)
        for c in range(n_lat - 1):
            readout_stage(c, c % 2)
            weights_stage(c + 1, (c + 1) % 2)
        readout_stage(n_lat - 1, (n_lat - 1) % 2)


def _mlstm(q, k, vt, kx, vtx, gf, gi, gfx, gix):
    B, L, _ = q.shape
    Lx = kx.shape[1]
    n_lat, n_ctx = L // MCHUNK, Lx // MCHUNK
    n_ext = n_lat + n_ctx
    n_tile = gf.shape[2] // LANES + gfx.shape[2] // LANES
    assert (n_ext - 1) * SERIES + (N_HEADS - 1) * ROWS_PER_HEAD + EXTRA <= n_tile * LANES
    tokens = lambda n: pl.BlockSpec((None, n, D_MODEL), lambda b: (b, 0, 0))
    chunks_t = lambda n: pl.BlockSpec((None, n, D_MODEL, MCHUNK), lambda b: (b, 0, 0, 0))
    gate_spec = lambda a: pl.BlockSpec((None,) + a.shape[1:], lambda b: (b, 0, 0))
    return pl.pallas_call(
        _mlstm_kernel,
        grid=(B,),
        in_specs=[
            tokens(L), tokens(L), chunks_t(n_lat), tokens(Lx), chunks_t(n_ctx),
            gate_spec(gf), gate_spec(gi), gate_spec(gfx), gate_spec(gix),
        ],
        out_specs=chunks_t(n_lat),
        out_shape=jax.ShapeDtypeStruct((B, n_lat, D_MODEL, MCHUNK), BF16),
        scratch_shapes=[
            pltpu.VMEM((MCHUNK, n_tile * LANES), F32),
            pltpu.VMEM((n_tile * LANES, MCHUNK), F32),
            pltpu.VMEM((n_tile * LANES, LANES), F32),
            pltpu.VMEM((2, 2, HEAD_DIM, HEAD_DIM), F32),
            pltpu.VMEM((2, 2, 1, HEAD_DIM), F32),
            pltpu.VMEM((2, 2, n_lat, HEAD_DIM + EXTRA, HEAD_DIM), BF16),
            pltpu.VMEM((2, 2, MCHUNK, 2 * MCHUNK), BF16),
        ],
        compiler_params=pltpu.CompilerParams(
            dimension_semantics=("parallel",), vmem_limit_bytes=VMEM_LIMIT),
        name="mlstm",
    )(q, k, vt, kx, vtx, gf, gi, gfx, gix)


def _rms(x, g):
    ms = jnp.mean(x * x, axis=-1, keepdims=True)
    return x * lax.rsqrt(ms + EPS) * g


def _merge_mlp_kernel(x_ref, p_ref, ht_ref, g1_ref, sh2_ref, sc2_ref, g2_ref,
                      n2_ref, nf_ref, gmh_ref, gsgu_ref, wa_ref, wb_ref, wo_ref, ws_ref, bst_ref,
                      w1_ref, w2_ref, o_ref, t_ref):
    tm = x_ref.shape[0]
    D = D_MODEL
    seg = lambda s: p_ref[:, s * D:(s + 1) * D].astype(F32)

    h_m = jnp.concatenate([ht_ref[mc].T for mc in range(tm // MCHUNK)], axis=0).astype(F32)
    hm = seg(0) * h_m
    gmh = gmh_ref[...]
    parts = []
    for h in range(N_HEADS):
        cols = slice(h * HEAD_DIM, (h + 1) * HEAD_DIM)
        parts.append(_rms(hm[:, cols], gmh[:, cols]).astype(BF16))
    ya = _dot(jnp.concatenate(parts, axis=1), wa_ref[...])

    u = _gelu(seg(1))
    vn = _rms(_gelu(seg(2)), gsgu_ref[...]).astype(BF16)
    bst = bst_ref[...]
    for cc in range(tm // CHUNK):
        rows = slice(cc * CHUNK, (cc + 1) * CHUNK)
        for g in range(N_GROUPS):
            cols = slice(g * GROUP_DIM, (g + 1) * GROUP_DIM)
            s = _dot(ws_ref[g], vn[rows, cols]) + bst[:, g:g + 1]
            t_ref[rows, cols] = (u[rows, cols] * s).astype(BF16)
    yb = _dot(t_ref[...], wb_ref[...])

    y = seg(3) * ya + seg(4) * yb
    mix = _dot(y.astype(BF16), wo_ref[...])
    x1 = x_ref[...] + g1_ref[...] * mix

    xn2 = (_rms(x1, n2_ref[...]) * (1.0 + sc2_ref[...]) + sh2_ref[...]).astype(BF16)
    ff = D_FF // 4
    acc = jnp.zeros((tm, D), F32)
    for kk in range(4):
        hmid = jnp.maximum(_dot(xn2, w1_ref[:, kk * ff:(kk + 1) * ff]), 0.0)
        acc = acc + _dot((hmid * hmid).astype(BF16), w2_ref[kk * ff:(kk + 1) * ff, :])
    x2 = x1 + g2_ref[...] * acc
    o_ref[...] = _rms(x2, nf_ref[...])


def _merge_mlp(x, pm, ht, mod3, norm2, norm_f, g_mh, g_sgu, w_a, w_b, w_out, w_s, b_st, w1, w2, tm):
    B, L, D = x.shape
    mod_spec = lambda k: pl.BlockSpec((None, 1, D), lambda b, i: (b, 0, k))
    return pl.pallas_call(
        _merge_mlp_kernel,
        grid=(B, L // tm),
        in_specs=[
            pl.BlockSpec((None, tm, D), lambda b, i: (b, i, 0)),
            pl.BlockSpec((None, tm, N_MERGE_SEG * D), lambda b, i: (b, i, 0)),
            pl.BlockSpec((None, tm // MCHUNK, D, MCHUNK), lambda b, i: (b, i, 0, 0)),
            mod_spec(2), mod_spec(3), mod_spec(4), mod_spec(5),
            _const_spec((1, D)), _const_spec((1, D)), _const_spec((1, D)), _const_spec((1, D)),
            _const_spec((D, D)), _const_spec((D, D)), _const_spec((D, D)),
            _const_spec(w_s.shape), _const_spec(b_st.shape),
            _const_spec((D, D_FF)), _const_spec((D_FF, D)),
        ],
        out_specs=pl.BlockSpec((None, tm, D), lambda b, i: (b, i, 0)),
        out_shape=jax.ShapeDtypeStruct((B, L, D), F32),
        scratch_shapes=[pltpu.VMEM((tm, D), BF16)],
        compiler_params=pltpu.CompilerParams(
            dimension_semantics=("parallel", "parallel"), vmem_limit_bytes=VMEM_LIMIT),
        name="merge_mlp",
    )(x, pm, ht, mod3, mod3, mod3, mod3, norm2, norm_f, g_mh, g_sgu, w_a, w_b, w_out, w_s, b_st, w1, w2)


def _gate_weights(w_gate, b_gate):
    D = w_gate.shape[0]

    def series(a, kind_of_dir):
        a4 = a.reshape(a.shape[:-1] + (4, N_HEADS))
        per_dir = jnp.stack([a4[..., kind_of_dir[0], :], a4[..., kind_of_dir[1], :]], axis=-1)
        dup = jnp.broadcast_to(per_dir[..., :, None, :], per_dir.shape[:-1] + (N_QUANT, 2))
        return dup.reshape(a.shape[:-1] + (SERIES,))

    wf, wi = series(w_gate, (1, 3)), series(w_gate, (0, 2))
    bf, bi = series(b_gate, (1, 3)), series(b_gate, (0, 2))
    eye = jnp.eye(CHUNKS_PER_TILE, dtype=w_gate.dtype)
    place = lambda w: jnp.einsum("ab,dk->adbk", eye, w).reshape(CHUNKS_PER_TILE, D, LANES)
    wgs = jnp.concatenate([place(wf), place(wi)], axis=2).astype(BF16)
    bgs = jnp.concatenate([jnp.tile(bf, CHUNKS_PER_TILE), jnp.tile(bi, CHUNKS_PER_TILE)])[None, :]
    return wgs, bgs


def kernel(x, c, ctx, c_ctx, norm1, norm2, w_mod, b_mod, w_in, conv_qk, b_gate, g_mh, w_a, w_s, b_s,
           g_sgu, w_b, w_out, w1, w2, norm_f):
    B, S, D = x.shape
    Lx = ctx.shape[1]
    assert D == D_MODEL and S % (MCHUNK * CHUNKS_PER_TILE) == 0 and S % MERGE_TM == 0
    assert Lx % MCHUNK == 0 and Lx // MCHUNK <= CHUNKS_PER_TILE
    assert w_mod.shape[0] == 1, "single-layer block"
    assert conv_qk.shape[1] == CONV_W
    W = D_MODEL
    off_g = 3 * W
    off_o = off_g + 4 * N_HEADS

    mod_rows = ((B + 1 + 7) // 8) * 8
    cc = jnp.zeros((mod_rows, D), F32).at[:B].set(c).at[B].set(c_ctx)
    mod = _modulation(cc, w_mod, b_mod)
    mod3 = mod.reshape(mod_rows, 1, 6 * D)

    wi = w_in[0]
    w_merge = wi[:, off_o:].astype(BF16)
    wq = wi[:, :W].astype(BF16)
    wk = wi[:, W:2 * W].astype(BF16)
    wvt = wi[:, 2 * W:off_g].T.astype(BF16)
    wgs, bgs = _gate_weights(wi[:, off_g:off_o], b_gate[0])

    n1 = norm1[0][None, :]
    conv = conv_qk[0]
    pm, q, k, vt, gf, gi = _inproj(x, mod3, lambda b: b, n1, wgs, bgs, conv, (w_merge, wq, wk, wvt), INPROJ_TM, True)
    kx, vtx, gfx, gix = _inproj(ctx, mod3, lambda b: B, n1, wgs, bgs, conv, (wk, wvt), Lx, False)

    ht = _mlstm(q, k, vt, kx, vtx, gf, gi, gfx, gix)

    return _merge_mlp(
        x, pm, ht, mod3, norm2[0][None, :], norm_f[None, :], g_mh[0][None, :], g_sgu[0][None, :],
        w_a[0].astype(BF16), w_b[0].astype(BF16), w_out[0].astype(BF16),
        w_s[0].astype(BF16), b_s[0].T, w1[0].astype(BF16), w2[0].astype(BF16), MERGE_TM)
```

```python
import jax
import jax.numpy as jnp
from jax import lax
from jax.experimental import pallas as pl
from jax.experimental.pallas import tpu as pltpu

D_MODEL = 1024
N_HEADS = 4
HEAD_DIM = D_MODEL // N_HEADS
CHUNK = 128
MCHUNK = 256
GROUP_DIM = 128
N_GROUPS = D_MODEL // GROUP_DIM
D_FF = 4 * D_MODEL
CONV_W = 3
NEG = -1e30
EPS = 1e-6
LANES = 128
HALO = 16
EXTRA = 16

N_QUANT = 4
ROWS_PER_HEAD = 2 * N_QUANT
SERIES = N_HEADS * ROWS_PER_HEAD
CHUNKS_PER_TILE = LANES // SERIES
Q_G, Q_E, Q_S, Q_FL = 0, 1, 2, 3

N_MERGE_SEG = 5

VMEM_LIMIT = 60000 * 1024
INPROJ_TM = 2 * MCHUNK
MERGE_TM = 512
MOD_TN = 1536

F32 = jnp.float32
BF16 = jnp.bfloat16


def _dot(a, b):
    return jnp.dot(a, b, preferred_element_type=F32)


def _dot_nt(a, b):
    return lax.dot_general(a, b, (((1,), (1,)), ((), ())), preferred_element_type=F32)


def _split_hi_lo(x):
    hi = x.astype(BF16)
    lo = (x - hi.astype(F32)).astype(BF16)
    return hi, lo


def _sigmoid(x):
    return 0.5 * jnp.tanh(0.5 * x) + 0.5


def _silu(x):
    half = 0.5 * x
    return half * jnp.tanh(half) + half


def _gelu(x):
    c = 0.7978845608028654
    half = 0.5 * x
    return half * jnp.tanh(x * (c + (c * 0.044715) * (x * x))) + half


MERGE_ACT = (_sigmoid, None, None, _sigmoid, _sigmoid)


def _const_spec(shape):
    nd = len(shape)
    return pl.BlockSpec(shape, lambda *_: (0,) * nd, pipeline_mode=pl.Buffered(1))


def _mod_kernel(c_ref, w_ref, b_ref, o_ref):
    s = _silu(c_ref[...])
    s_hi, s_lo = _split_hi_lo(s)
    w_hi, w_lo = _split_hi_lo(w_ref[...])
    acc = _dot(s_hi, w_hi) + _dot(s_hi, w_lo) + _dot(s_lo, w_hi)
    o_ref[...] = acc + b_ref[...]


def _modulation(cc, w_mod, b_mod):
    rows, d = cc.shape
    n = w_mod.shape[2]
    tn = MOD_TN
    return pl.pallas_call(
        _mod_kernel,
        grid=(n // tn,),
        in_specs=[
            pl.BlockSpec((rows, d), lambda j: (0, 0)),
            pl.BlockSpec((None, d, tn), lambda j: (0, 0, j)),
            pl.BlockSpec((1, tn), lambda j: (0, j)),
        ],
        out_specs=pl.BlockSpec((rows, tn), lambda j: (0, j)),
        out_shape=jax.ShapeDtypeStruct((rows, n), F32),
        compiler_params=pltpu.CompilerParams(
            dimension_semantics=("arbitrary",), vmem_limit_bytes=VMEM_LIMIT),
        name="modulation",
    )(cc, w_mod, b_mod)


def _log_sigmoid(x):
    return jnp.minimum(x, 0.0) - jnp.log1p(jnp.exp(-jnp.abs(x)))


def _make_inproj_kernel(tm, n_tiles, latent):
    n_cc = tm // CHUNK
    n_mc = tm // MCHUNK
    tiles_per_gate_tile = CHUNKS_PER_TILE // n_mc

    def kernel(*refs):
        if latent:
            (x_ref, xp_ref, xnx_ref, sh_ref, sc_ref, g_ref, wg_ref, bg_ref, cw_ref, wm_ref, wq_ref, wk_ref, wvt_ref,
             pm_ref, q_ref, k_ref, vt_ref, gf_ref, gi_ref, xe_ref, peq_ref, pek_ref) = refs
        else:
            (x_ref, xp_ref, xnx_ref, sh_ref, sc_ref, g_ref, wg_ref, bg_ref, cw_ref, wk_ref, wvt_ref,
             k_ref, vt_ref, gf_ref, gi_ref, xe_ref, pek_ref) = refs
        i = pl.program_id(1)

        gain = g_ref[...] * (1.0 + sc_ref[...])

        def normed(x):
            ms = jnp.mean(x * x, axis=-1, keepdims=True)
            return x * lax.rsqrt(ms + EPS) * gain + sh_ref[...]

        xn = normed(x_ref[...]).astype(BF16)
        xe_ref[HALO:HALO + tm, :] = xn
        xe_ref[0:HALO, :] = jnp.where(i > 0, normed(xp_ref[...]), 0.0).astype(BF16)
        xe_ref[HALO + tm:, :] = jnp.where(i < n_tiles - 1, normed(xnx_ref[...]), 0.0).astype(BF16)

        pad = 8
        n_ext = CHUNK + 2 * pad

        def conv_silu(pe_ref, w_ref, cw, scale, out_ref):
            pe_ref[...] = _dot(xe_ref[...], w_ref[...])
            for cc in range(n_cc):
                r = HALO + cc * CHUNK
                x_ext = pe_ref[r - pad:r + CHUNK + pad, :]
                prev = pltpu.roll(x_ext, 1, 0)[pad:pad + CHUNK, :]
                nxt = pltpu.roll(x_ext, n_ext - 1, 0)[pad:pad + CHUNK, :]
                y = prev * cw[0:1, :] + x_ext[pad:pad + CHUNK, :] * cw[1:2, :] + nxt * cw[2:3, :]
                y = _silu(y)
                if scale != 1.0:
                    y = y * scale
                out_ref[cc * CHUNK:(cc + 1) * CHUNK, :] = y.astype(BF16)

        cw = cw_ref[...]
        if latent:
            conv_silu(peq_ref, wq_ref, cw[:, :D_MODEL], 1.0, q_ref)
        conv_silu(pek_ref, wk_ref, cw[:, D_MODEL:], HEAD_DIM ** -0.5, k_ref)

        if latent:
            for s in range(N_MERGE_SEG):
                cols = slice(s * D_MODEL, (s + 1) * D_MODEL)
                z = _dot(xn, wm_ref[:, cols])
                if MERGE_ACT[s] is not None:
                    z = MERGE_ACT[s](z)
                pm_ref[:, cols] = z.astype(BF16)

        sub = i % tiles_per_gate_tile
        acc = _dot(xn[0:MCHUNK], wg_ref[sub * n_mc])
        for mc in range(1, n_mc):
            acc = acc + _dot(xn[mc * MCHUNK:(mc + 1) * MCHUNK], wg_ref[sub * n_mc + mc])
        acc = acc + bg_ref[...]
        group = lax.broadcasted_iota(jnp.int32, (1, LANES), 1) // SERIES
        own = (group >= sub * n_mc) & (group < (sub + 1) * n_mc)
        new_f = jnp.where(own, _log_sigmoid(acc[:, :LANES]), 0.0)
        new_i = jnp.where(own, acc[:, LANES:], 0.0)

        vt = _dot_nt(wvt_ref[...], xn)
        for mc in range(n_mc):
            vt_ref[mc] = vt[:, mc * MCHUNK:(mc + 1) * MCHUNK].astype(BF16)

        @pl.when(sub == 0)
        def _():
            gf_ref[...] = new_f
            gi_ref[...] = new_i

        @pl.when(sub != 0)
        def _():
            gf_ref[...] += new_f
            gi_ref[...] += new_i

    return kernel


def _inproj(x, mod3, mod_row_of_batch, norm_g, wgs, bgs, conv_qk, weights, tm, latent):
    B, L, D = x.shape
    n_tiles = L // tm
    n_mc = tm // MCHUNK
    tiles_per_gate_tile = CHUNKS_PER_TILE // n_mc
    n_gate_tiles = -(-n_tiles // tiles_per_gate_tile)
    hb = tm // HALO
    n_hblk = L // HALO
    row = mod_row_of_batch
    tok = lambda width: pl.BlockSpec((None, tm, width), lambda b, i: (b, i, 0))
    tok_shape = lambda width: jax.ShapeDtypeStruct((B, L, width), BF16)
    widths = [w.shape[1] for w in weights[:-1]]
    gate_spec = pl.BlockSpec((None, MCHUNK, LANES), lambda b, i: (b, 0, i // tiles_per_gate_tile))
    gate_shape = jax.ShapeDtypeStruct((B, MCHUNK, n_gate_tiles * LANES), F32)
    vt_spec = pl.BlockSpec((None, n_mc, D, MCHUNK), lambda b, i: (b, i, 0, 0))
    vt_shape = jax.ShapeDtypeStruct((B, L // MCHUNK, D, MCHUNK), BF16)
    return pl.pallas_call(
        _make_inproj_kernel(tm, n_tiles, latent),
        grid=(B, n_tiles),
        in_specs=[
            pl.BlockSpec((None, tm, D), lambda b, i: (b, i, 0)),
            pl.BlockSpec((None, HALO, D), lambda b, i: (b, jnp.maximum(i * hb - 1, 0), 0)),
            pl.BlockSpec((None, HALO, D), lambda b, i: (b, jnp.minimum((i + 1) * hb, n_hblk - 1), 0)),
            pl.BlockSpec((None, 1, D), lambda b, i: (row(b), 0, 0)),
            pl.BlockSpec((None, 1, D), lambda b, i: (row(b), 0, 1)),
            _const_spec((1, D)), _const_spec(wgs.shape), _const_spec(bgs.shape), _const_spec(conv_qk.shape),
        ] + [_const_spec(w.shape) for w in weights],
        out_specs=[tok(w) for w in widths] + [vt_spec, gate_spec, gate_spec],
        out_shape=[tok_shape(w) for w in widths] + [vt_shape, gate_shape, gate_shape],
        scratch_shapes=[pltpu.VMEM((tm + 2 * HALO, D), BF16)]
        + [pltpu.VMEM((tm + 2 * HALO, D), F32)] * (2 if latent else 1),
        compiler_params=pltpu.CompilerParams(
            dimension_semantics=("parallel", "arbitrary"), vmem_limit_bytes=VMEM_LIMIT),
        name="inproj_latent" if latent else "inproj_context",
    )(x, x, x, mod3, mod3, norm_g, wgs, bgs, conv_qk, *weights)


def _cummax_rows(x, reverse):
    n = x.shape[0]
    rows = lax.broadcasted_iota(jnp.int32, x.shape, 0)
    k = 1
    while k < n:
        if reverse:
            shifted = jnp.where(rows < n - k, pltpu.roll(x, n - k, 0), NEG)
        else:
            shifted = jnp.where(rows >= k, pltpu.roll(x, k, 0), NEG)
        x = jnp.maximum(x, shifted)
        k *= 2
    return x


def _gate_prep(gf, gi, n_lat, n_ctx, r_ref, rt_ref, sp_ref):
    n_ext = n_lat + n_ctx
    width = gf.shape[1]
    n_tile = width // LANES
    ii = lax.broadcasted_iota(jnp.int32, (MCHUNK, MCHUNK), 0)
    jj = lax.broadcasted_iota(jnp.int32, (MCHUNK, MCHUNK), 1)
    tril = jnp.where(ii >= jj, 1.0, 0.0).astype(BF16)
    triu = jnp.where(ii <= jj, 1.0, 0.0).astype(BF16)
    lane = lax.broadcasted_iota(jnp.int32, (1, width), 1)
    lane_bwd = (lane % 2) == 1
    lane_q = (lane // 2) % N_QUANT

    hi, lo = _split_hi_lo(gf)
    b = jnp.where(lane_bwd, _dot(triu, hi) + _dot(triu, lo), _dot(tril, hi) + _dot(tril, lo))
    tot = jnp.where(lane_bwd, b[0:1, :], b[MCHUNK - 1:MCHUNK, :])
    r = gi - b
    cm = jnp.where(lane_bwd, _cummax_rows(r, True), _cummax_rows(r, False))
    rmax = jnp.where(lane_bwd, cm[0:1, :], cm[MCHUNK - 1:MCHUNK, :])

    def to_rows(v):
        return [jnp.broadcast_to(v[:, t * LANES:(t + 1) * LANES], (LANES, LANES)).T for t in range(n_tile)]

    tot_t, rmax_t = to_rows(tot), to_rows(rmax)

    def slab(tiles, e):
        t, k = divmod(e, CHUNKS_PER_TILE)
        return tiles[t][k * SERIES:(k + 1) * SERIES, :]

    ctx_ids = list(range(n_lat, n_ext))
    orders = (ctx_ids + list(range(n_lat)), ctx_ids[::-1] + list(range(n_lat - 1, -1, -1)))
    m_prev, m_new = [{}, {}], [{}, {}]
    for d, order in enumerate(orders):
        m = jnp.zeros((SERIES, LANES), F32)
        for e in order:
            m_prev[d][e] = m
            tot_e = slab(tot_t, e)
            m = jnp.maximum(tot_e + m, tot_e + slab(rmax_t, e))
            m_new[d][e] = m
    row_bwd = (lax.broadcasted_iota(jnp.int32, (SERIES, LANES), 0) % 2) == 1
    mp_slabs = [jnp.where(row_bwd, m_prev[1][e], m_prev[0][e]) for e in range(n_ext)]
    mn_slabs = [jnp.where(row_bwd, m_new[1][e], m_new[0][e]) for e in range(n_ext)]
    sp_slabs = [jnp.exp(slab(tot_t, e) + mp_slabs[e] - mn_slabs[e]) for e in range(n_ext)]

    def tiles_of(slabs):
        pad = [jnp.zeros((SERIES, LANES), F32)] * (n_tile * CHUNKS_PER_TILE - n_ext)
        full = slabs + pad
        return [jnp.concatenate(full[t * CHUNKS_PER_TILE:(t + 1) * CHUNKS_PER_TILE], axis=0) for t in range(n_tile)]

    def to_lanes(slabs):
        return jnp.concatenate([t.T[0:1, :] for t in tiles_of(slabs)], axis=1)

    mp = to_lanes(mp_slabs)
    mn = to_lanes(mn_slabs)

    g = jnp.maximum(mp, cm)
    e_w = jnp.exp(tot + r - mn)
    s_inter = jnp.exp(mp - g)
    floor = jnp.exp(-(b + g))
    packed = jnp.where(lane_q == Q_G, g, jnp.where(lane_q == Q_E, e_w, jnp.where(lane_q == Q_S, s_inter, floor)))

    sp_tiles = tiles_of(sp_slabs)
    for t in range(n_tile):
        rt_ref[t * LANES:(t + 1) * LANES, :] = packed[:, t * LANES:(t + 1) * LANES].T
        sp_ref[t * LANES:(t + 1) * LANES, :] = sp_tiles[t]
    r_ref[...] = r


def _mlstm_kernel(q_ref, k_ref, vt_ref, kx_ref, vtx_ref, gf_ref, gi_ref, gfx_ref, gix_ref,
                  o_ref, r_ref, rt_ref, sp_ref, ct_ref, n_ref, cprev_ref, at_ref):
    n_lat = q_ref.shape[0] // MCHUNK
    n_ctx = kx_ref.shape[0] // MCHUNK

    gf = jnp.concatenate([gf_ref[...], gfx_ref[...]], axis=1)
    gi = jnp.concatenate([gi_ref[...], gix_ref[...]], axis=1)
    _gate_prep(gf, gi, n_lat, n_ctx, r_ref, rt_ref, sp_ref)

    jj = lax.broadcasted_iota(jnp.int32, (MCHUNK, MCHUNK), 0)
    ii = lax.broadcasted_iota(jnp.int32, (MCHUNK, MCHUNK), 1)
    visible = (jj <= ii, jj >= ii)
    ones_rows = jnp.ones((EXTRA, MCHUNK), BF16)

    def row(rt, quantity, d):
        i = 2 * quantity + d
        return rt[i:i + 1, :]

    for h in range(N_HEADS):
        hs = slice(h * HEAD_DIM, (h + 1) * HEAD_DIM)
        par = h % 2
        series0 = lambda e, h=h: e * SERIES + h * ROWS_PER_HEAD

        def state_step(d, e, k, vt, c_lat):
            rt_x = rt_ref[series0(e):series0(e) + EXTRA, :]
            s_row = sp_ref[series0(e) + d:series0(e) + d + 1, :]
            s2 = jnp.concatenate([s_row, s_row], axis=1)
            vet = (vt.astype(F32) * row(rt_x, Q_E, d)).astype(BF16)
            upd = _dot(jnp.concatenate([vet, rt_x.astype(BF16)], axis=0), k)
            i_e = HEAD_DIM + 2 * Q_E + d
            ek = upd[i_e:i_e + 1, :]
            ct_old = ct_ref[par, d]
            n_old = n_ref[par, d]
            if c_lat is not None:
                cprev_ref[par, d, c_lat, 0:HEAD_DIM, :] = ct_old.astype(BF16)
                cprev_ref[par, d, c_lat, HEAD_DIM:, :] = jnp.broadcast_to(n_old, (EXTRA, HEAD_DIM)).astype(BF16)
            ct_ref[par, d] = s2 * ct_old + upd[0:HEAD_DIM, :]
            n_ref[par, d] = s2 * n_old + ek

        ct_ref[par] = jnp.zeros(ct_ref.shape[1:], F32)
        n_ref[par] = jnp.zeros(n_ref.shape[1:], F32)
        for t in range(n_ctx):
            for d in range(2):
                cx = (n_ctx - 1 - t) if d else t
                rows = slice(cx * MCHUNK, (cx + 1) * MCHUNK)
                state_step(d, n_lat + cx, kx_ref[rows, hs], vtx_ref[cx, hs, :], None)
        for t in range(n_lat):
            for d in range(2):
                c = (n_lat - 1 - t) if d else t
                rows = slice(c * MCHUNK, (c + 1) * MCHUNK)
                state_step(d, c, k_ref[rows, hs], vt_ref[c, hs, :], c)

        def weights_stage(c, slot):
            rows = slice(c * MCHUNK, (c + 1) * MCHUNK)
            rt = rt_ref[series0(c):series0(c) + ROWS_PER_HEAD, :]
            st = _dot_nt(k_ref[rows, hs], q_ref[rows, hs])
            for d in range(2):
                r_col = r_ref[:, series0(c) + d:series0(c) + d + 1]
                w = jnp.exp(jnp.where(visible[d], r_col - row(rt, Q_G, d), NEG))
                at_ref[par, slot, :, d * MCHUNK:(d + 1) * MCHUNK] = (w * st).astype(BF16)

        def readout_stage(c, slot):
            rows = slice(c * MCHUNK, (c + 1) * MCHUNK)
            q = q_ref[rows, hs]
            rt = rt_ref[series0(c):series0(c) + ROWS_PER_HEAD, :]
            at = at_ref[par, slot]
            num = _dot(jnp.concatenate([vt_ref[c, hs, :], ones_rows], axis=0), at)
            out = None
            for d in range(2):
                lanes = slice(d * MCHUNK, (d + 1) * MCHUNK)
                s_inter = row(rt, Q_S, d)
                inter = _dot_nt(cprev_ref[par, d, c], q)
                den = num[HEAD_DIM:HEAD_DIM + 1, lanes] + s_inter * inter[HEAD_DIM:HEAD_DIM + 1, :]
                inv = 1.0 / jnp.maximum(jnp.abs(den), row(rt, Q_FL, d))
                hd = (num[0:HEAD_DIM, lanes] + s_inter * inter[0:HEAD_DIM, :]) * inv
                out = hd if out is None else out + hd
            o_ref[c, hs, :] = out.astype(o_ref.dtype)

        weights_stage(0, 0)
        for c in range(n_lat - 1):
            readout_stage(c, c % 2)
            weights_stage(c + 1, (c + 1) % 2)
        readout_stage(n_lat - 1, (n_lat - 1) % 2)


def _mlstm(q, k, vt, kx, vtx, gf, gi, gfx, gix):
    B, L, _ = q.shape
    Lx = kx.shape[1]
    n_lat, n_ctx = L // MCHUNK, Lx // MCHUNK
    n_ext = n_lat + n_ctx
    n_tile = gf.shape[2] // LANES + gfx.shape[2] // LANES
    assert (n_ext - 1) * SERIES + (N_HEADS - 1) * ROWS_PER_HEAD + EXTRA <= n_tile * LANES
    tokens = lambda n: pl.BlockSpec((None, n, D_MODEL), lambda b: (b, 0, 0))
    chunks_t = lambda n: pl.BlockSpec((None, n, D_MODEL, MCHUNK), lambda b: (b, 0, 0, 0))
    gate_spec = lambda a: pl.BlockSpec((None,) + a.shape[1:], lambda b: (b, 0, 0))
    return pl.pallas_call(
        _mlstm_kernel,
        grid=(B,),
        in_specs=[
            tokens(L), tokens(L), chunks_t(n_lat), tokens(Lx), chunks_t(n_ctx),
            gate_spec(gf), gate_spec(gi), gate_spec(gfx), gate_spec(gix),
        ],
        out_specs=chunks_t(n_lat),
        out_shape=jax.ShapeDtypeStruct((B, n_lat, D_MODEL, MCHUNK), BF16),
        scratch_shapes=[
            pltpu.VMEM((MCHUNK, n_tile * LANES), F32),
            pltpu.VMEM((n_tile * LANES, MCHUNK), F32),
            pltpu.VMEM((n_tile * LANES, LANES), F32),
            pltpu.VMEM((2, 2, HEAD_DIM, HEAD_DIM), F32),
            pltpu.VMEM((2, 2, 1, HEAD_DIM), F32),
            pltpu.VMEM((2, 2, n_lat, HEAD_DIM + EXTRA, HEAD_DIM), BF16),
            pltpu.VMEM((2, 2, MCHUNK, 2 * MCHUNK), BF16),
        ],
        compiler_params=pltpu.CompilerParams(
            dimension_semantics=("parallel",), vmem_limit_bytes=VMEM_LIMIT),
        name="mlstm",
    )(q, k, vt, kx, vtx, gf, gi, gfx, gix)


def _rms(x, g):
    ms = jnp.mean(x * x, axis=-1, keepdims=True)
    return x * lax.rsqrt(ms + EPS) * g


def _merge_mlp_kernel(x_ref, p_ref, ht_ref, g1_ref, sh2_ref, sc2_ref, g2_ref,
                      n2_ref, nf_ref, gmh_ref, gsgu_ref, wa_ref, wb_ref, wo_ref, ws_ref, bst_ref,
                      w1_ref, w2_ref, o_ref, t_ref):
    tm = x_ref.shape[0]
    D = D_MODEL
    seg = lambda s: p_ref[:, s * D:(s + 1) * D].astype(F32)

    h_m = jnp.concatenate([ht_ref[mc].T for mc in range(tm // MCHUNK)], axis=0).astype(F32)
    hm = seg(0) * h_m
    gmh = gmh_ref[...]
    parts = []
    for h in range(N_HEADS):
        cols = slice(h * HEAD_DIM, (h + 1) * HEAD_DIM)
        parts.append(_rms(hm[:, cols], gmh[:, cols]).astype(BF16))
    ya = _dot(jnp.concatenate(parts, axis=1), wa_ref[...])

    u = _gelu(seg(1))
    vn = _rms(_gelu(seg(2)), gsgu_ref[...]).astype(BF16)
    bst = bst_ref[...]
    for cc in range(tm // CHUNK):
        rows = slice(cc * CHUNK, (cc + 1) * CHUNK)
        for g in range(N_GROUPS):
            cols = slice(g * GROUP_DIM, (g + 1) * GROUP_DIM)
            s = _dot(ws_ref[g], vn[rows, cols]) + bst[:, g:g + 1]
            t_ref[rows, cols] = (u[rows, cols] * s).astype(BF16)
    yb = _dot(t_ref[...], wb_ref[...])

    y = seg(3) * ya + seg(4) * yb
    mix = _dot(y.astype(BF16), wo_ref[...])
    x1 = x_ref[...] + g1_ref[...] * mix

    xn2 = (_rms(x1, n2_ref[...] * (1.0 + sc2_ref[...])) + sh2_ref[...]).astype(BF16)
    ff = D_FF // 4
    acc = jnp.zeros((tm, D), F32)
    for kk in range(4):
        hmid = jnp.maximum(_dot(xn2, w1_ref[:, kk * ff:(kk + 1) * ff]), 0.0)
        acc = acc + _dot((hmid * hmid).astype(BF16), w2_ref[kk * ff:(kk + 1) * ff, :])
    x2 = x1 + g2_ref[...] * acc
    o_ref[...] = _rms(x2, nf_ref[...])


def _merge_mlp(x, pm, ht, mod3, norm2, norm_f, g_mh, g_sgu, w_a, w_b, w_out, w_s, b_st, w1, w2, tm):
    B, L, D = x.shape
    mod_spec = lambda k: pl.BlockSpec((None, 1, D), lambda b, i: (b, 0, k))
    return pl.pallas_call(
        _merge_mlp_kernel,
        grid=(B, L // tm),
        in_specs=[
            pl.BlockSpec((None, tm, D), lambda b, i: (b, i, 0)),
            pl.BlockSpec((None, tm, N_MERGE_SEG * D), lambda b, i: (b, i, 0)),
            pl.BlockSpec((None, tm // MCHUNK, D, MCHUNK), lambda b, i: (b, i, 0, 0)),
            mod_spec(2), mod_spec(3), mod_spec(4), mod_spec(5),
            _const_spec((1, D)), _const_spec((1, D)), _const_spec((1, D)), _const_spec((1, D)),
            _const_spec((D, D)), _const_spec((D, D)), _const_spec((D, D)),
            _const_spec(w_s.shape), _const_spec(b_st.shape),
            _const_spec((D, D_FF)), _const_spec((D_FF, D)),
        ],
        out_specs=pl.BlockSpec((None, tm, D), lambda b, i: (b, i, 0)),
        out_shape=jax.ShapeDtypeStruct((B, L, D), F32),
        scratch_shapes=[pltpu.VMEM((tm, D), BF16)],
        compiler_params=pltpu.CompilerParams(
            dimension_semantics=("parallel", "parallel"), vmem_limit_bytes=VMEM_LIMIT),
        name="merge_mlp",
    )(x, pm, ht, mod3, mod3, mod3, mod3, norm2, norm_f, g_mh, g_sgu, w_a, w_b, w_out, w_s, b_st, w1, w2)


def _gate_weights(w_gate, b_gate):
    D = w_gate.shape[0]

    def series(a, kind_of_dir):
        a4 = a.reshape(a.shape[:-1] + (4, N_HEADS))
        per_dir = jnp.stack([a4[..., kind_of_dir[0], :], a4[..., kind_of_dir[1], :]], axis=-1)
        dup = jnp.broadcast_to(per_dir[..., :, None, :], per_dir.shape[:-1] + (N_QUANT, 2))
        return dup.reshape(a.shape[:-1] + (SERIES,))

    wf, wi = series(w_gate, (1, 3)), series(w_gate, (0, 2))
    bf, bi = series(b_gate, (1, 3)), series(b_gate, (0, 2))
    eye = jnp.eye(CHUNKS_PER_TILE, dtype=w_gate.dtype)
    place = lambda w: jnp.einsum("ab,dk->adbk", eye, w).reshape(CHUNKS_PER_TILE, D, LANES)
    wgs = jnp.concatenate([place(wf), place(wi)], axis=2).astype(BF16)
    bgs = jnp.concatenate([jnp.tile(bf, CHUNKS_PER_TILE), jnp.tile(bi, CHUNKS_PER_TILE)])[None, :]
    return wgs, bgs


def kernel(x, c, ctx, c_ctx, norm1, norm2, w_mod, b_mod, w_in, conv_qk, b_gate, g_mh, w_a, w_s, b_s,
           g_sgu, w_b, w_out, w1, w2, norm_f):
    B, S, D = x.shape
    Lx = ctx.shape[1]
    assert D == D_MODEL and S % (MCHUNK * CHUNKS_PER_TILE) == 0 and S % MERGE_TM == 0
    assert Lx % MCHUNK == 0 and Lx // MCHUNK <= CHUNKS_PER_TILE
    assert w_mod.shape[0] == 1, "single-layer block"
    assert conv_qk.shape[1] == CONV_W
    W = D_MODEL
    off_g = 3 * W
    off_o = off_g + 4 * N_HEADS

    mod_rows = ((B + 1 + 7) // 8) * 8
    cc = jnp.zeros((mod_rows, D), F32).at[:B].set(c).at[B].set(c_ctx)
    mod = _modulation(cc, w_mod, b_mod)
    mod3 = mod.reshape(mod_rows, 1, 6 * D)

    wi = w_in[0]
    w_merge = wi[:, off_o:].astype(BF16)
    wq = wi[:, :W].astype(BF16)
    wk = wi[:, W:2 * W].astype(BF16)
    wvt = wi[:, 2 * W:off_g].T.astype(BF16)
    wgs, bgs = _gate_weights(wi[:, off_g:off_o], b_gate[0])

    n1 = norm1[0][None, :]
    conv = conv_qk[0]
    pm, q, k, vt, gf, gi = _inproj(x, mod3, lambda b: b, n1, wgs, bgs, conv, (w_merge, wq, wk, wvt), INPROJ_TM, True)
    kx, vtx, gfx, gix = _inproj(ctx, mod3, lambda b: B, n1, wgs, bgs, conv, (wk, wvt), Lx, False)

    ht = _mlstm(q, k, vt, kx, vtx, gf, gi, gfx, gix)

    return _merge_mlp(
        x, pm, ht, mod3, norm2[0][None, :], norm_f[None, :], g_mh[0][None, :], g_sgu[0][None, :],
        w_a[0].astype(BF16), w_b[0].astype(BF16), w_out[0].astype(BF16),
        w_s[0].astype(BF16), b_s[0].T, w1[0].astype(BF16), w2[0].astype(BF16), MERGE_TM)
```

```python
import jax
import jax.numpy as jnp
from jax import lax
from jax.experimental import pallas as pl
from jax.experimental.pallas import tpu as pltpu

D_MODEL = 1024
N_HEADS = 4
HEAD_DIM = D_MODEL // N_HEADS
CHUNK = 128
MCHUNK = 256
GROUP_DIM = 128
N_GROUPS = D_MODEL // GROUP_DIM
D_FF = 4 * D_MODEL
CONV_W = 3
NEG = -1e30
LOG2E = 1.4426950408889634
EPS = 1e-6
LANES = 128
HALO = 16
EXTRA = 16

N_QUANT = 4
ROWS_PER_HEAD = 2 * N_QUANT
SERIES = N_HEADS * ROWS_PER_HEAD
CHUNKS_PER_TILE = LANES // SERIES
Q_G, Q_E, Q_S, Q_FL = 0, 1, 2, 3

N_MERGE_SEG = 5

VMEM_LIMIT = 60000 * 1024
INPROJ_TM = 2 * MCHUNK
MERGE_TM = 512
MOD_TN = 1536

F32 = jnp.float32
BF16 = jnp.bfloat16


def _dot(a, b):
    return jnp.dot(a, b, preferred_element_type=F32)


def _dot_nt(a, b):
    return lax.dot_general(a, b, (((1,), (1,)), ((), ())), preferred_element_type=F32)


def _split_hi_lo(x):
    hi = x.astype(BF16)
    lo = (x - hi.astype(F32)).astype(BF16)
    return hi, lo


def _sigmoid(x):
    return 0.5 * jnp.tanh(0.5 * x) + 0.5


def _silu(x):
    half = 0.5 * x
    return half * jnp.tanh(half) + half


def _gelu(x):
    c = 0.7978845608028654
    half = 0.5 * x
    return half * jnp.tanh(x * (c + (c * 0.044715) * (x * x))) + half


MERGE_ACT = (_sigmoid, None, None, _sigmoid, _sigmoid)


def _const_spec(shape):
    nd = len(shape)
    return pl.BlockSpec(shape, lambda *_: (0,) * nd, pipeline_mode=pl.Buffered(1))


def _mod_kernel(c_ref, w_ref, b_ref, o_ref):
    s = _silu(c_ref[...])
    s_hi, s_lo = _split_hi_lo(s)
    w_hi, w_lo = _split_hi_lo(w_ref[...])
    acc = _dot(s_hi, w_hi) + _dot(s_hi, w_lo) + _dot(s_lo, w_hi)
    o_ref[...] = acc + b_ref[...]


def _modulation(cc, w_mod, b_mod):
    rows, d = cc.shape
    n = w_mod.shape[2]
    tn = MOD_TN
    return pl.pallas_call(
        _mod_kernel,
        grid=(n // tn,),
        in_specs=[
            pl.BlockSpec((rows, d), lambda j: (0, 0)),
            pl.BlockSpec((None, d, tn), lambda j: (0, 0, j)),
            pl.BlockSpec((1, tn), lambda j: (0, j)),
        ],
        out_specs=pl.BlockSpec((rows, tn), lambda j: (0, j)),
        out_shape=jax.ShapeDtypeStruct((rows, n), F32),
        compiler_params=pltpu.CompilerParams(
            dimension_semantics=("arbitrary",), vmem_limit_bytes=VMEM_LIMIT),
        name="modulation",
    )(cc, w_mod, b_mod)


def _log_sigmoid(x):
    return jnp.minimum(x, 0.0) - jnp.log1p(jnp.exp(-jnp.abs(x)))


def _make_inproj_kernel(tm, n_tiles, latent):
    n_cc = tm // CHUNK
    n_mc = tm // MCHUNK
    tiles_per_gate_tile = CHUNKS_PER_TILE // n_mc

    def kernel(*refs):
        if latent:
            (x_ref, xp_ref, xnx_ref, sh_ref, sc_ref, g_ref, wg_ref, bg_ref, cw_ref, wm_ref, wq_ref, wk_ref, wvt_ref,
             pm_ref, q_ref, k_ref, vt_ref, gf_ref, gi_ref, xe_ref, peq_ref, pek_ref) = refs
        else:
            (x_ref, xp_ref, xnx_ref, sh_ref, sc_ref, g_ref, wg_ref, bg_ref, cw_ref, wk_ref, wvt_ref,
             k_ref, vt_ref, gf_ref, gi_ref, xe_ref, pek_ref) = refs
        i = pl.program_id(1)

        gain = g_ref[...] * (1.0 + sc_ref[...])

        def normed(x):
            ms = jnp.mean(x * x, axis=-1, keepdims=True)
            return x * lax.rsqrt(ms + EPS) * gain + sh_ref[...]

        xn = normed(x_ref[...]).astype(BF16)
        xe_ref[HALO:HALO + tm, :] = xn
        xe_ref[0:HALO, :] = jnp.where(i > 0, normed(xp_ref[...]), 0.0).astype(BF16)
        xe_ref[HALO + tm:, :] = jnp.where(i < n_tiles - 1, normed(xnx_ref[...]), 0.0).astype(BF16)

        pad = 8
        n_ext = CHUNK + 2 * pad

        def conv_silu(pe_ref, w_ref, cw, scale, out_ref):
            pe_ref[...] = _dot(xe_ref[...], w_ref[...])
            for cc in range(n_cc):
                r = HALO + cc * CHUNK
                x_ext = pe_ref[r - pad:r + CHUNK + pad, :]
                prev = pltpu.roll(x_ext, 1, 0)[pad:pad + CHUNK, :]
                nxt = pltpu.roll(x_ext, n_ext - 1, 0)[pad:pad + CHUNK, :]
                y = prev * cw[0:1, :] + x_ext[pad:pad + CHUNK, :] * cw[1:2, :] + nxt * cw[2:3, :]
                y = _silu(y)
                if scale != 1.0:
                    y = y * scale
                out_ref[cc * CHUNK:(cc + 1) * CHUNK, :] = y.astype(BF16)

        cw = cw_ref[...]
        if latent:
            conv_silu(peq_ref, wq_ref, cw[:, :D_MODEL], 1.0, q_ref)
        conv_silu(pek_ref, wk_ref, cw[:, D_MODEL:], HEAD_DIM ** -0.5, k_ref)

        if latent:
            for s in range(N_MERGE_SEG):
                cols = slice(s * D_MODEL, (s + 1) * D_MODEL)
                z = _dot(xn, wm_ref[:, cols])
                if MERGE_ACT[s] is not None:
                    z = MERGE_ACT[s](z)
                pm_ref[:, cols] = z.astype(BF16)

        sub = i % tiles_per_gate_tile
        acc = _dot(xn[0:MCHUNK], wg_ref[sub * n_mc])
        for mc in range(1, n_mc):
            acc = acc + _dot(xn[mc * MCHUNK:(mc + 1) * MCHUNK], wg_ref[sub * n_mc + mc])
        acc = acc + bg_ref[...]
        group = lax.broadcasted_iota(jnp.int32, (1, LANES), 1) // SERIES
        own = (group >= sub * n_mc) & (group < (sub + 1) * n_mc)
        new_f = jnp.where(own, _log_sigmoid(acc[:, :LANES]), 0.0)
        new_i = jnp.where(own, acc[:, LANES:], 0.0)

        vt = _dot_nt(wvt_ref[...], xn)
        for mc in range(n_mc):
            vt_ref[mc] = vt[:, mc * MCHUNK:(mc + 1) * MCHUNK].astype(BF16)

        @pl.when(sub == 0)
        def _():
            gf_ref[...] = new_f
            gi_ref[...] = new_i

        @pl.when(sub != 0)
        def _():
            gf_ref[...] += new_f
            gi_ref[...] += new_i

    return kernel


def _inproj(x, mod3, mod_row_of_batch, norm_g, wgs, bgs, conv_qk, weights, tm, latent):
    B, L, D = x.shape
    n_tiles = L // tm
    n_mc = tm // MCHUNK
    tiles_per_gate_tile = CHUNKS_PER_TILE // n_mc
    n_gate_tiles = -(-n_tiles // tiles_per_gate_tile)
    hb = tm // HALO
    n_hblk = L // HALO
    row = mod_row_of_batch
    tok = lambda width: pl.BlockSpec((None, tm, width), lambda b, i: (b, i, 0))
    tok_shape = lambda width: jax.ShapeDtypeStruct((B, L, width), BF16)
    widths = [w.shape[1] for w in weights[:-1]]
    gate_spec = pl.BlockSpec((None, MCHUNK, LANES), lambda b, i: (b, 0, i // tiles_per_gate_tile))
    gate_shape = jax.ShapeDtypeStruct((B, MCHUNK, n_gate_tiles * LANES), F32)
    vt_spec = pl.BlockSpec((None, n_mc, D, MCHUNK), lambda b, i: (b, i, 0, 0))
    vt_shape = jax.ShapeDtypeStruct((B, L // MCHUNK, D, MCHUNK), BF16)
    return pl.pallas_call(
        _make_inproj_kernel(tm, n_tiles, latent),
        grid=(B, n_tiles),
        in_specs=[
            pl.BlockSpec((None, tm, D), lambda b, i: (b, i, 0)),
            pl.BlockSpec((None, HALO, D), lambda b, i: (b, jnp.maximum(i * hb - 1, 0), 0)),
            pl.BlockSpec((None, HALO, D), lambda b, i: (b, jnp.minimum((i + 1) * hb, n_hblk - 1), 0)),
            pl.BlockSpec((None, 1, D), lambda b, i: (row(b), 0, 0)),
            pl.BlockSpec((None, 1, D), lambda b, i: (row(b), 0, 1)),
            _const_spec((1, D)), _const_spec(wgs.shape), _const_spec(bgs.shape), _const_spec(conv_qk.shape),
        ] + [_const_spec(w.shape) for w in weights],
        out_specs=[tok(w) for w in widths] + [vt_spec, gate_spec, gate_spec],
        out_shape=[tok_shape(w) for w in widths] + [vt_shape, gate_shape, gate_shape],
        scratch_shapes=[pltpu.VMEM((tm + 2 * HALO, D), BF16)]
        + [pltpu.VMEM((tm + 2 * HALO, D), F32)] * (2 if latent else 1),
        compiler_params=pltpu.CompilerParams(
            dimension_semantics=("parallel", "arbitrary"), vmem_limit_bytes=VMEM_LIMIT),
        name="inproj_latent" if latent else "inproj_context",
    )(x, x, x, mod3, mod3, norm_g, wgs, bgs, conv_qk, *weights)


def _cummax_rows(x, reverse):
    n = x.shape[0]
    rows = lax.broadcasted_iota(jnp.int32, x.shape, 0)
    k = 1
    while k < n:
        if reverse:
            shifted = jnp.where(rows < n - k, pltpu.roll(x, n - k, 0), NEG)
        else:
            shifted = jnp.where(rows >= k, pltpu.roll(x, k, 0), NEG)
        x = jnp.maximum(x, shifted)
        k *= 2
    return x


def _gate_prep(gf, gi, n_lat, n_ctx, r_ref, rt_ref, sp_ref):
    n_ext = n_lat + n_ctx
    width = gf.shape[1]
    n_tile = width // LANES
    ii = lax.broadcasted_iota(jnp.int32, (MCHUNK, MCHUNK), 0)
    jj = lax.broadcasted_iota(jnp.int32, (MCHUNK, MCHUNK), 1)
    tril = jnp.where(ii >= jj, 1.0, 0.0).astype(BF16)
    triu = jnp.where(ii <= jj, 1.0, 0.0).astype(BF16)
    lane = lax.broadcasted_iota(jnp.int32, (1, width), 1)
    lane_bwd = (lane % 2) == 1
    lane_q = (lane // 2) % N_QUANT

    hi, lo = _split_hi_lo(gf)
    b = jnp.where(lane_bwd, _dot(triu, hi) + _dot(triu, lo), _dot(tril, hi) + _dot(tril, lo))
    tot = jnp.where(lane_bwd, b[0:1, :], b[MCHUNK - 1:MCHUNK, :])
    r = gi - b
    cm = jnp.where(lane_bwd, _cummax_rows(r, True), _cummax_rows(r, False))
    rmax = jnp.where(lane_bwd, cm[0:1, :], cm[MCHUNK - 1:MCHUNK, :])

    def to_rows(v):
        return [jnp.broadcast_to(v[:, t * LANES:(t + 1) * LANES], (LANES, LANES)).T for t in range(n_tile)]

    tot_t, rmax_t = to_rows(tot), to_rows(rmax)

    def slab(tiles, e):
        t, k = divmod(e, CHUNKS_PER_TILE)
        return tiles[t][k * SERIES:(k + 1) * SERIES, :]

    ctx_ids = list(range(n_lat, n_ext))
    orders = (ctx_ids + list(range(n_lat)), ctx_ids[::-1] + list(range(n_lat - 1, -1, -1)))
    m_prev, m_new = [{}, {}], [{}, {}]
    for d, order in enumerate(orders):
        m = jnp.zeros((SERIES, LANES), F32)
        for e in order:
            m_prev[d][e] = m
            tot_e = slab(tot_t, e)
            m = jnp.maximum(tot_e + m, tot_e + slab(rmax_t, e))
            m_new[d][e] = m
    row_bwd = (lax.broadcasted_iota(jnp.int32, (SERIES, LANES), 0) % 2) == 1
    mp_slabs = [jnp.where(row_bwd, m_prev[1][e], m_prev[0][e]) for e in range(n_ext)]
    mn_slabs = [jnp.where(row_bwd, m_new[1][e], m_new[0][e]) for e in range(n_ext)]
    sp_slabs = [jnp.exp(slab(tot_t, e) + mp_slabs[e] - mn_slabs[e]) for e in range(n_ext)]

    def tiles_of(slabs):
        pad = [jnp.zeros((SERIES, LANES), F32)] * (n_tile * CHUNKS_PER_TILE - n_ext)
        full = slabs + pad
        return [jnp.concatenate(full[t * CHUNKS_PER_TILE:(t + 1) * CHUNKS_PER_TILE], axis=0) for t in range(n_tile)]

    def to_lanes(slabs):
        return jnp.concatenate([t.T[0:1, :] for t in tiles_of(slabs)], axis=1)

    mp = to_lanes(mp_slabs)
    mn = to_lanes(mn_slabs)

    g = jnp.maximum(mp, cm)
    e_w = jnp.exp(tot + r - mn)
    s_inter = jnp.exp(mp - g)
    floor = jnp.exp(-(b + g))
    packed = jnp.where(lane_q == Q_G, g * LOG2E,
                       jnp.where(lane_q == Q_E, e_w, jnp.where(lane_q == Q_S, s_inter, floor)))

    sp_tiles = tiles_of(sp_slabs)
    for t in range(n_tile):
        rt_ref[t * LANES:(t + 1) * LANES, :] = packed[:, t * LANES:(t + 1) * LANES].T
        sp_ref[t * LANES:(t + 1) * LANES, :] = sp_tiles[t]
    r_ref[...] = r * LOG2E


def _mlstm_kernel(q_ref, k_ref, vt_ref, kx_ref, vtx_ref, gf_ref, gi_ref, gfx_ref, gix_ref,
                  o_ref, r_ref, rt_ref, sp_ref, ct_ref, n_ref, cprev_ref, at_ref):
    n_lat = q_ref.shape[0] // MCHUNK
    n_ctx = kx_ref.shape[0] // MCHUNK

    gf = jnp.concatenate([gf_ref[...], gfx_ref[...]], axis=1)
    gi = jnp.concatenate([gi_ref[...], gix_ref[...]], axis=1)
    _gate_prep(gf, gi, n_lat, n_ctx, r_ref, rt_ref, sp_ref)

    jj = lax.broadcasted_iota(jnp.int32, (MCHUNK, MCHUNK), 0)
    ii = lax.broadcasted_iota(jnp.int32, (MCHUNK, MCHUNK), 1)
    visible = (jj <= ii, jj >= ii)
    ones_rows = jnp.ones((EXTRA, MCHUNK), BF16)

    def row(rt, quantity, d):
        i = 2 * quantity + d
        return rt[i:i + 1, :]

    for h in range(N_HEADS):
        hs = slice(h * HEAD_DIM, (h + 1) * HEAD_DIM)
        par = h % 2
        series0 = lambda e, h=h: e * SERIES + h * ROWS_PER_HEAD

        def state_step(d, e, k, vt, c_lat):
            rt_x = rt_ref[series0(e):series0(e) + EXTRA, :]
            s_row = sp_ref[series0(e) + d:series0(e) + d + 1, :]
            s2 = jnp.concatenate([s_row, s_row], axis=1)
            vet = (vt.astype(F32) * row(rt_x, Q_E, d)).astype(BF16)
            upd = _dot(jnp.concatenate([vet, rt_x.astype(BF16)], axis=0), k)
            i_e = HEAD_DIM + 2 * Q_E + d
            ek = upd[i_e:i_e + 1, :]
            ct_old = ct_ref[par, d]
            n_old = n_ref[par, d]
            if c_lat is not None:
                cprev_ref[par, d, c_lat, 0:HEAD_DIM, :] = ct_old.astype(BF16)
                cprev_ref[par, d, c_lat, HEAD_DIM:, :] = jnp.broadcast_to(n_old, (EXTRA, HEAD_DIM)).astype(BF16)
            ct_ref[par, d] = s2 * ct_old + upd[0:HEAD_DIM, :]
            n_ref[par, d] = s2 * n_old + ek

        ct_ref[par] = jnp.zeros(ct_ref.shape[1:], F32)
        n_ref[par] = jnp.zeros(n_ref.shape[1:], F32)
        for t in range(n_ctx):
            for d in range(2):
                cx = (n_ctx - 1 - t) if d else t
                rows = slice(cx * MCHUNK, (cx + 1) * MCHUNK)
                state_step(d, n_lat + cx, kx_ref[rows, hs], vtx_ref[cx, hs, :], None)
        for t in range(n_lat):
            for d in range(2):
                c = (n_lat - 1 - t) if d else t
                rows = slice(c * MCHUNK, (c + 1) * MCHUNK)
                state_step(d, c, k_ref[rows, hs], vt_ref[c, hs, :], c)

        def weights_stage(c, slot):
            rows = slice(c * MCHUNK, (c + 1) * MCHUNK)
            rt = rt_ref[series0(c):series0(c) + ROWS_PER_HEAD, :]
            st = _dot_nt(k_ref[rows, hs], q_ref[rows, hs])
            for d in range(2):
                r_col = r_ref[:, series0(c) + d:series0(c) + d + 1]
                w = jnp.exp2(jnp.where(visible[d], r_col - row(rt, Q_G, d), NEG))
                at_ref[par, slot, :, d * MCHUNK:(d + 1) * MCHUNK] = (w * st).astype(BF16)

        def readout_stage(c, slot):
            rows = slice(c * MCHUNK, (c + 1) * MCHUNK)
            q = q_ref[rows, hs]
            rt = rt_ref[series0(c):series0(c) + ROWS_PER_HEAD, :]
            at = at_ref[par, slot]
            num = _dot(jnp.concatenate([vt_ref[c, hs, :], ones_rows], axis=0), at)
            out = None
            for d in range(2):
                lanes = slice(d * MCHUNK, (d + 1) * MCHUNK)
                s_inter = row(rt, Q_S, d)
                inter = _dot_nt(cprev_ref[par, d, c], q)
                den = num[HEAD_DIM:HEAD_DIM + 1, lanes] + s_inter * inter[HEAD_DIM:HEAD_DIM + 1, :]
                inv = 1.0 / jnp.maximum(jnp.abs(den), row(rt, Q_FL, d))
                hd = (num[0:HEAD_DIM, lanes] + s_inter * inter[0:HEAD_DIM, :]) * inv
                out = hd if out is None else out + hd
            o_ref[c, hs, :] = out.astype(o_ref.dtype)

        weights_stage(0, 0)
        for c in range(n_lat - 1):
            readout_stage(c, c % 2)
            weights_stage(c + 1, (c + 1) % 2)
        readout_stage(n_lat - 1, (n_lat - 1) % 2)


def _mlstm(q, k, vt, kx, vtx, gf, gi, gfx, gix):
    B, L, _ = q.shape
    Lx = kx.shape[1]
    n_lat, n_ctx = L // MCHUNK, Lx // MCHUNK
    n_ext = n_lat + n_ctx
    n_tile = gf.shape[2] // LANES + gfx.shape[2] // LANES
    assert (n_ext - 1) * SERIES + (N_HEADS - 1) * ROWS_PER_HEAD + EXTRA <= n_tile * LANES
    tokens = lambda n: pl.BlockSpec((None, n, D_MODEL), lambda b: (b, 0, 0))
    chunks_t = lambda n: pl.BlockSpec((None, n, D_MODEL, MCHUNK), lambda b: (b, 0, 0, 0))
    gate_spec = lambda a: pl.BlockSpec((None,) + a.shape[1:], lambda b: (b, 0, 0))
    return pl.pallas_call(
        _mlstm_kernel,
        grid=(B,),
        in_specs=[
            tokens(L), tokens(L), chunks_t(n_lat), tokens(Lx), chunks_t(n_ctx),
            gate_spec(gf), gate_spec(gi), gate_spec(gfx), gate_spec(gix),
        ],
        out_specs=chunks_t(n_lat),
        out_shape=jax.ShapeDtypeStruct((B, n_lat, D_MODEL, MCHUNK), BF16),
        scratch_shapes=[
            pltpu.VMEM((MCHUNK, n_tile * LANES), F32),
            pltpu.VMEM((n_tile * LANES, MCHUNK), F32),
            pltpu.VMEM((n_tile * LANES, LANES), F32),
            pltpu.VMEM((2, 2, HEAD_DIM, HEAD_DIM), F32),
            pltpu.VMEM((2, 2, 1, HEAD_DIM), F32),
            pltpu.VMEM((2, 2, n_lat, HEAD_DIM + EXTRA, HEAD_DIM), BF16),
            pltpu.VMEM((2, 2, MCHUNK, 2 * MCHUNK), BF16),
        ],
        compiler_params=pltpu.CompilerParams(
            dimension_semantics=("parallel",), vmem_limit_bytes=VMEM_LIMIT),
        name="mlstm",
    )(q, k, vt, kx, vtx, gf, gi, gfx, gix)


def _rms(x, g):
    ms = jnp.mean(x * x, axis=-1, keepdims=True)
    return x * lax.rsqrt(ms + EPS) * g


def _merge_mlp_kernel(x_ref, p_ref, ht_ref, g1_ref, sh2_ref, sc2_ref, g2_ref,
                      n2_ref, nf_ref, gmh_ref, gsgu_ref, wa_ref, wb_ref, wo_ref, ws_ref, bst_ref,
                      w1_ref, w2_ref, o_ref, t_ref):
    tm = x_ref.shape[0]
    D = D_MODEL
    seg = lambda s: p_ref[:, s * D:(s + 1) * D].astype(F32)

    h_m = jnp.concatenate([ht_ref[mc].T for mc in range(tm // MCHUNK)], axis=0).astype(F32)
    hm = seg(0) * h_m
    gmh = gmh_ref[...]
    parts = []
    for h in range(N_HEADS):
        cols = slice(h * HEAD_DIM, (h + 1) * HEAD_DIM)
        parts.append(_rms(hm[:, cols], gmh[:, cols]).astype(BF16))
    ya = _dot(jnp.concatenate(parts, axis=1), wa_ref[...])

    u = _gelu(seg(1))
    vn = _rms(_gelu(seg(2)), gsgu_ref[...]).astype(BF16)
    bst = bst_ref[...]
    for cc in range(tm // CHUNK):
        rows = slice(cc * CHUNK, (cc + 1) * CHUNK)
        for g in range(N_GROUPS):
            cols = slice(g * GROUP_DIM, (g + 1) * GROUP_DIM)
            s = _dot(ws_ref[g], vn[rows, cols]) + bst[:, g:g + 1]
            t_ref[rows, cols] = (u[rows, cols] * s).astype(BF16)
    yb = _dot(t_ref[...], wb_ref[...])

    y = seg(3) * ya + seg(4) * yb
    mix = _dot(y.astype(BF16), wo_ref[...])
    x1 = x_ref[...] + g1_ref[...] * mix

    xn2 = (_rms(x1, n2_ref[...] * (1.0 + sc2_ref[...])) + sh2_ref[...]).astype(BF16)
    ff = D_FF // 4
    acc = jnp.zeros((tm, D), F32)
    for kk in range(3):
        hmid = jnp.maximum(_dot(xn2, w1_ref[:, kk * ff:(kk + 1) * ff]), 0.0)
        acc = acc + _dot((hmid * hmid).astype(BF16), w2_ref[kk * ff:(kk + 1) * ff, :])
    hmid = jnp.maximum(_dot(xn2, w1_ref[:, 3 * ff:]), 0.0)
    h2 = (hmid * hmid).astype(BF16)
    half = tm // 2
    for r in range(2):
        rows = slice(r * half, (r + 1) * half)
        x2 = x1[rows, :] + g2_ref[...] * (acc[rows, :] + _dot(h2[rows, :], w2_ref[3 * ff:, :]))
        o_ref[rows, :] = _rms(x2, nf_ref[...])


def _merge_mlp(x, pm, ht, mod3, norm2, norm_f, g_mh, g_sgu, w_a, w_b, w_out, w_s, b_st, w1, w2, tm):
    B, L, D = x.shape
    mod_spec = lambda k: pl.BlockSpec((None, 1, D), lambda b, i: (b, 0, k))
    return pl.pallas_call(
        _merge_mlp_kernel,
        grid=(B, L // tm),
        in_specs=[
            pl.BlockSpec((None, tm, D), lambda b, i: (b, i, 0)),
            pl.BlockSpec((None, tm, N_MERGE_SEG * D), lambda b, i: (b, i, 0)),
            pl.BlockSpec((None, tm // MCHUNK, D, MCHUNK), lambda b, i: (b, i, 0, 0)),
            mod_spec(2), mod_spec(3), mod_spec(4), mod_spec(5),
            _const_spec((1, D)), _const_spec((1, D)), _const_spec((1, D)), _const_spec((1, D)),
            _const_spec((D, D)), _const_spec((D, D)), _const_spec((D, D)),
            _const_spec(w_s.shape), _const_spec(b_st.shape),
            _const_spec((D, D_FF)), _const_spec((D_FF, D)),
        ],
        out_specs=pl.BlockSpec((None, tm, D), lambda b, i: (b, i, 0)),
        out_shape=jax.ShapeDtypeStruct((B, L, D), F32),
        scratch_shapes=[pltpu.VMEM((tm, D), BF16)],
        compiler_params=pltpu.CompilerParams(
            dimension_semantics=("parallel", "parallel"), vmem_limit_bytes=VMEM_LIMIT),
        name="merge_mlp",
    )(x, pm, ht, mod3, mod3, mod3, mod3, norm2, norm_f, g_mh, g_sgu, w_a, w_b, w_out, w_s, b_st, w1, w2)


def _gate_weights(w_gate, b_gate):
    D = w_gate.shape[0]

    def series(a, kind_of_dir):
        a4 = a.reshape(a.shape[:-1] + (4, N_HEADS))
        per_dir = jnp.stack([a4[..., kind_of_dir[0], :], a4[..., kind_of_dir[1], :]], axis=-1)
        dup = jnp.broadcast_to(per_dir[..., :, None, :], per_dir.shape[:-1] + (N_QUANT, 2))
        return dup.reshape(a.shape[:-1] + (SERIES,))

    wf, wi = series(w_gate, (1, 3)), series(w_gate, (0, 2))
    bf, bi = series(b_gate, (1, 3)), series(b_gate, (0, 2))
    eye = jnp.eye(CHUNKS_PER_TILE, dtype=w_gate.dtype)
    place = lambda w: jnp.einsum("ab,dk->adbk", eye, w).reshape(CHUNKS_PER_TILE, D, LANES)
    wgs = jnp.concatenate([place(wf), place(wi)], axis=2).astype(BF16)
    bgs = jnp.concatenate([jnp.tile(bf, CHUNKS_PER_TILE), jnp.tile(bi, CHUNKS_PER_TILE)])[None, :]
    return wgs, bgs


def kernel(x, c, ctx, c_ctx, norm1, norm2, w_mod, b_mod, w_in, conv_qk, b_gate, g_mh, w_a, w_s, b_s,
           g_sgu, w_b, w_out, w1, w2, norm_f):
    B, S, D = x.shape
    Lx = ctx.shape[1]
    assert D == D_MODEL and S % (MCHUNK * CHUNKS_PER_TILE) == 0 and S % MERGE_TM == 0
    assert Lx % MCHUNK == 0 and Lx // MCHUNK <= CHUNKS_PER_TILE
    assert w_mod.shape[0] == 1, "single-layer block"
    assert conv_qk.shape[1] == CONV_W
    W = D_MODEL
    off_g = 3 * W
    off_o = off_g + 4 * N_HEADS

    mod_rows = ((B + 1 + 7) // 8) * 8
    cc = jnp.zeros((mod_rows, D), F32).at[:B].set(c).at[B].set(c_ctx)
    mod = _modulation(cc, w_mod, b_mod)
    mod3 = mod.reshape(mod_rows, 1, 6 * D)

    wi = w_in[0]
    w_merge = wi[:, off_o:].astype(BF16)
    wq = wi[:, :W].astype(BF16)
    wk = wi[:, W:2 * W].astype(BF16)
    wvt = wi[:, 2 * W:off_g].T.astype(BF16)
    wgs, bgs = _gate_weights(wi[:, off_g:off_o], b_gate[0])

    n1 = norm1[0][None, :]
    conv = conv_qk[0]
    pm, q, k, vt, gf, gi = _inproj(x, mod3, lambda b: b, n1, wgs, bgs, conv, (w_merge, wq, wk, wvt), INPROJ_TM, True)
    kx, vtx, gfx, gix = _inproj(ctx, mod3, lambda b: B, n1, wgs, bgs, conv, (wk, wvt), Lx, False)

    ht = _mlstm(q, k, vt, kx, vtx, gf, gi, gfx, gix)

    return _merge_mlp(
        x, pm, ht, mod3, norm2[0][None, :], norm_f[None, :], g_mh[0][None, :], g_sgu[0][None, :],
        w_a[0].astype(BF16), w_b[0].astype(BF16), w_out[0].astype(BF16),
        w_s[0].astype(BF16), b_s[0].T, w1[0].astype(BF16), w2[0].astype(BF16), MERGE_TM)
```

```python
import jax
import jax.numpy as jnp
from jax import lax
from jax.experimental import pallas as pl
from jax.experimental.pallas import tpu as pltpu

D_MODEL = 1024
N_HEADS = 4
HEAD_DIM = D_MODEL // N_HEADS
CHUNK = 128
MCHUNK = 256
GROUP_DIM = 128
N_GROUPS = D_MODEL // GROUP_DIM
D_FF = 4 * D_MODEL
CONV_W = 3
NEG = -1e30
LOG2E = 1.4426950408889634
EPS = 1e-6
LANES = 128
HALO = 16
EXTRA = 16

N_QUANT = 4
ROWS_PER_HEAD = 2 * N_QUANT
SERIES = N_HEADS * ROWS_PER_HEAD
CHUNKS_PER_TILE = LANES // SERIES
Q_G, Q_E, Q_S, Q_FL = 0, 1, 2, 3

N_MERGE_SEG = 5

VMEM_LIMIT = 60000 * 1024
INPROJ_TM = 2 * MCHUNK
MERGE_TM = 512
MOD_TN = 1536

F32 = jnp.float32
BF16 = jnp.bfloat16


def _dot(a, b):
    return jnp.dot(a, b, preferred_element_type=F32)


def _dot_nt(a, b):
    return lax.dot_general(a, b, (((1,), (1,)), ((), ())), preferred_element_type=F32)


def _split_hi_lo(x):
    hi = x.astype(BF16)
    lo = (x - hi.astype(F32)).astype(BF16)
    return hi, lo


def _sigmoid(x):
    return 0.5 * jnp.tanh(0.5 * x) + 0.5


def _silu(x):
    half = 0.5 * x
    return half * jnp.tanh(half) + half


def _gelu(x):
    c = 0.7978845608028654
    half = 0.5 * x
    return half * jnp.tanh(x * (c + (c * 0.044715) * (x * x))) + half


MERGE_ACT = (_sigmoid, None, None, _sigmoid, _sigmoid)


def _const_spec(shape):
    nd = len(shape)
    return pl.BlockSpec(shape, lambda *_: (0,) * nd, pipeline_mode=pl.Buffered(1))


def _mod_kernel(c_ref, w_ref, b_ref, o_ref):
    s = _silu(c_ref[...])
    s_hi, s_lo = _split_hi_lo(s)
    w_hi, w_lo = _split_hi_lo(w_ref[...])
    acc = _dot(s_hi, w_hi) + _dot(s_hi, w_lo) + _dot(s_lo, w_hi)
    o_ref[...] = acc + b_ref[...]


def _modulation(cc, w_mod, b_mod):
    rows, d = cc.shape
    n = w_mod.shape[2]
    tn = MOD_TN
    return pl.pallas_call(
        _mod_kernel,
        grid=(n // tn,),
        in_specs=[
            pl.BlockSpec((rows, d), lambda j: (0, 0)),
            pl.BlockSpec((None, d, tn), lambda j: (0, 0, j)),
            pl.BlockSpec((1, tn), lambda j: (0, j)),
        ],
        out_specs=pl.BlockSpec((rows, tn), lambda j: (0, j)),
        out_shape=jax.ShapeDtypeStruct((rows, n), F32),
        compiler_params=pltpu.CompilerParams(
            dimension_semantics=("arbitrary",), vmem_limit_bytes=VMEM_LIMIT),
        name="modulation",
    )(cc, w_mod, b_mod)


def _log_sigmoid(x):
    return jnp.minimum(x, 0.0) - jnp.log1p(jnp.exp(-jnp.abs(x)))


def _make_inproj_kernel(tm, n_tiles, latent):
    n_cc = tm // CHUNK
    n_mc = tm // MCHUNK
    tiles_per_gate_tile = CHUNKS_PER_TILE // n_mc

    def kernel(*refs):
        if latent:
            (x_ref, xp_ref, xnx_ref, sh_ref, sc_ref, g_ref, wg_ref, bg_ref, cw_ref, wm_ref, wq_ref, wk_ref, wvt_ref,
             pm_ref, q_ref, k_ref, vt_ref, gf_ref, gi_ref, xe_ref, peq_ref, pek_ref) = refs
        else:
            (x_ref, xp_ref, xnx_ref, sh_ref, sc_ref, g_ref, wg_ref, bg_ref, cw_ref, wk_ref, wvt_ref,
             k_ref, vt_ref, gf_ref, gi_ref, xe_ref, pek_ref) = refs
        i = pl.program_id(1)

        gain = g_ref[...] * (1.0 + sc_ref[...])

        def normed(x):
            ms = jnp.mean(x * x, axis=-1, keepdims=True)
            return x * lax.rsqrt(ms + EPS) * gain + sh_ref[...]

        xn = normed(x_ref[...]).astype(BF16)
        xe_ref[HALO:HALO + tm, :] = xn
        xe_ref[0:HALO, :] = jnp.where(i > 0, normed(xp_ref[...]), 0.0).astype(BF16)
        xe_ref[HALO + tm:, :] = jnp.where(i < n_tiles - 1, normed(xnx_ref[...]), 0.0).astype(BF16)

        pad = 8
        n_ext = CHUNK + 2 * pad

        def conv_silu(pe_ref, w_ref, cw, scale, out_ref):
            pe_ref[...] = _dot(xe_ref[...], w_ref[...])
            for cc in range(n_cc):
                r = HALO + cc * CHUNK
                x_ext = pe_ref[r - pad:r + CHUNK + pad, :]
                prev = pltpu.roll(x_ext, 1, 0)[pad:pad + CHUNK, :]
                nxt = pltpu.roll(x_ext, n_ext - 1, 0)[pad:pad + CHUNK, :]
                y = prev * cw[0:1, :] + x_ext[pad:pad + CHUNK, :] * cw[1:2, :] + nxt * cw[2:3, :]
                y = _silu(y)
                if scale != 1.0:
                    y = y * scale
                out_ref[cc * CHUNK:(cc + 1) * CHUNK, :] = y.astype(BF16)

        cw = cw_ref[...]
        if latent:
            conv_silu(peq_ref, wq_ref, cw[:, :D_MODEL], 1.0, q_ref)
        conv_silu(pek_ref, wk_ref, cw[:, D_MODEL:], HEAD_DIM ** -0.5, k_ref)

        if latent:
            for s in range(N_MERGE_SEG):
                cols = slice(s * D_MODEL, (s + 1) * D_MODEL)
                z = _dot(xn, wm_ref[:, cols])
                if MERGE_ACT[s] is not None:
                    z = MERGE_ACT[s](z)
                pm_ref[:, cols] = z.astype(BF16)

        sub = i % tiles_per_gate_tile
        acc = _dot(xn[0:MCHUNK], wg_ref[sub * n_mc])
        for mc in range(1, n_mc):
            acc = acc + _dot(xn[mc * MCHUNK:(mc + 1) * MCHUNK], wg_ref[sub * n_mc + mc])
        acc = acc + bg_ref[...]
        group = lax.broadcasted_iota(jnp.int32, (1, LANES), 1) // SERIES
        own = (group >= sub * n_mc) & (group < (sub + 1) * n_mc)
        new_f = jnp.where(own, _log_sigmoid(acc[:, :LANES]), 0.0)
        new_i = jnp.where(own, acc[:, LANES:], 0.0)

        vt = _dot_nt(wvt_ref[...], xn)
        for mc in range(n_mc):
            vt_ref[mc] = vt[:, mc * MCHUNK:(mc + 1) * MCHUNK].astype(BF16)

        @pl.when(sub == 0)
        def _():
            gf_ref[...] = new_f
            gi_ref[...] = new_i

        @pl.when(sub != 0)
        def _():
            gf_ref[...] += new_f
            gi_ref[...] += new_i

    return kernel


def _inproj(x, mod3, mod_row_of_batch, norm_g, wgs, bgs, conv_qk, weights, tm, latent):
    B, L, D = x.shape
    n_tiles = L // tm
    n_mc = tm // MCHUNK
    tiles_per_gate_tile = CHUNKS_PER_TILE // n_mc
    n_gate_tiles = -(-n_tiles // tiles_per_gate_tile)
    hb = tm // HALO
    n_hblk = L // HALO
    row = mod_row_of_batch
    tok = lambda width: pl.BlockSpec((None, tm, width), lambda b, i: (b, i, 0))
    tok_shape = lambda width: jax.ShapeDtypeStruct((B, L, width), BF16)
    widths = [w.shape[1] for w in weights[:-1]]
    gate_spec = pl.BlockSpec((None, MCHUNK, LANES), lambda b, i: (b, 0, i // tiles_per_gate_tile))
    gate_shape = jax.ShapeDtypeStruct((B, MCHUNK, n_gate_tiles * LANES), F32)
    vt_spec = pl.BlockSpec((None, n_mc, D, MCHUNK), lambda b, i: (b, i, 0, 0))
    vt_shape = jax.ShapeDtypeStruct((B, L // MCHUNK, D, MCHUNK), BF16)
    return pl.pallas_call(
        _make_inproj_kernel(tm, n_tiles, latent),
        grid=(B, n_tiles),
        in_specs=[
            pl.BlockSpec((None, tm, D), lambda b, i: (b, i, 0)),
            pl.BlockSpec((None, HALO, D), lambda b, i: (b, jnp.maximum(i * hb - 1, 0), 0)),
            pl.BlockSpec((None, HALO, D), lambda b, i: (b, jnp.minimum((i + 1) * hb, n_hblk - 1), 0)),
            pl.BlockSpec((None, 1, D), lambda b, i: (row(b), 0, 0)),
            pl.BlockSpec((None, 1, D), lambda b, i: (row(b), 0, 1)),
            _const_spec((1, D)), _const_spec(wgs.shape), _const_spec(bgs.shape), _const_spec(conv_qk.shape),
        ] + [_const_spec(w.shape) for w in weights],
        out_specs=[tok(w) for w in widths] + [vt_spec, gate_spec, gate_spec],
        out_shape=[tok_shape(w) for w in widths] + [vt_shape, gate_shape, gate_shape],
        scratch_shapes=[pltpu.VMEM((tm + 2 * HALO, D), BF16)]
        + [pltpu.VMEM((tm + 2 * HALO, D), F32)] * (2 if latent else 1),
        compiler_params=pltpu.CompilerParams(
            dimension_semantics=("parallel", "arbitrary"), vmem_limit_bytes=VMEM_LIMIT,
            allow_input_fusion=[False] * 9 + [True] * len(weights)),
        name="inproj_latent" if latent else "inproj_context",
    )(x, x, x, mod3, mod3, norm_g, wgs, bgs, conv_qk, *weights)


def _cummax_rows(x, reverse):
    n = x.shape[0]
    rows = lax.broadcasted_iota(jnp.int32, x.shape, 0)
    k = 1
    while k < n:
        if reverse:
            shifted = jnp.where(rows < n - k, pltpu.roll(x, n - k, 0), NEG)
        else:
            shifted = jnp.where(rows >= k, pltpu.roll(x, k, 0), NEG)
        x = jnp.maximum(x, shifted)
        k *= 2
    return x


def _gate_prep(gf, gi, n_lat, n_ctx, r_ref, rt_ref, sp_ref):
    n_ext = n_lat + n_ctx
    width = gf.shape[1]
    n_tile = width // LANES
    ii = lax.broadcasted_iota(jnp.int32, (MCHUNK, MCHUNK), 0)
    jj = lax.broadcasted_iota(jnp.int32, (MCHUNK, MCHUNK), 1)
    tril = jnp.where(ii >= jj, 1.0, 0.0).astype(BF16)
    triu = jnp.where(ii <= jj, 1.0, 0.0).astype(BF16)
    lane = lax.broadcasted_iota(jnp.int32, (1, width), 1)
    lane_bwd = (lane % 2) == 1
    lane_q = (lane // 2) % N_QUANT

    hi, lo = _split_hi_lo(gf)
    b = jnp.where(lane_bwd, _dot(triu, hi) + _dot(triu, lo), _dot(tril, hi) + _dot(tril, lo))
    tot = jnp.where(lane_bwd, b[0:1, :], b[MCHUNK - 1:MCHUNK, :])
    r = gi - b
    cm = jnp.where(lane_bwd, _cummax_rows(r, True), _cummax_rows(r, False))
    rmax = jnp.where(lane_bwd, cm[0:1, :], cm[MCHUNK - 1:MCHUNK, :])

    def to_rows(v):
        return [jnp.broadcast_to(v[:, t * LANES:(t + 1) * LANES], (LANES, LANES)).T for t in range(n_tile)]

    tot_t, rmax_t = to_rows(tot), to_rows(rmax)

    def slab(tiles, e):
        t, k = divmod(e, CHUNKS_PER_TILE)
        return tiles[t][k * SERIES:(k + 1) * SERIES, :]

    ctx_ids = list(range(n_lat, n_ext))
    orders = (ctx_ids + list(range(n_lat)), ctx_ids[::-1] + list(range(n_lat - 1, -1, -1)))
    m_prev, m_new = [{}, {}], [{}, {}]
    for d, order in enumerate(orders):
        m = jnp.zeros((SERIES, LANES), F32)
        for e in order:
            m_prev[d][e] = m
            tot_e = slab(tot_t, e)
            m = jnp.maximum(tot_e + m, tot_e + slab(rmax_t, e))
            m_new[d][e] = m
    row_bwd = (lax.broadcasted_iota(jnp.int32, (SERIES, LANES), 0) % 2) == 1
    mp_slabs = [jnp.where(row_bwd, m_prev[1][e], m_prev[0][e]) for e in range(n_ext)]
    mn_slabs = [jnp.where(row_bwd, m_new[1][e], m_new[0][e]) for e in range(n_ext)]
    sp_slabs = [jnp.exp(slab(tot_t, e) + mp_slabs[e] - mn_slabs[e]) for e in range(n_ext)]

    def tiles_of(slabs):
        pad = [jnp.zeros((SERIES, LANES), F32)] * (n_tile * CHUNKS_PER_TILE - n_ext)
        full = slabs + pad
        return [jnp.concatenate(full[t * CHUNKS_PER_TILE:(t + 1) * CHUNKS_PER_TILE], axis=0) for t in range(n_tile)]

    def to_lanes(slabs):
        return jnp.concatenate([t.T[0:1, :] for t in tiles_of(slabs)], axis=1)

    mp = to_lanes(mp_slabs)
    mn = to_lanes(mn_slabs)

    g = jnp.maximum(mp, cm)
    e_w = jnp.exp(tot + r - mn)
    s_inter = jnp.exp(mp - g)
    floor = jnp.exp(-(b + g))
    packed = jnp.where(lane_q == Q_G, g * LOG2E,
                       jnp.where(lane_q == Q_E, e_w, jnp.where(lane_q == Q_S, s_inter, floor)))

    sp_tiles = tiles_of(sp_slabs)
    for t in range(n_tile):
        rt_ref[t * LANES:(t + 1) * LANES, :] = packed[:, t * LANES:(t + 1) * LANES].T
        sp_ref[t * LANES:(t + 1) * LANES, :] = sp_tiles[t]
    r_ref[...] = r * LOG2E


def _mlstm_kernel(q_ref, k_ref, vt_ref, kx_ref, vtx_ref, gf_ref, gi_ref, gfx_ref, gix_ref,
                  o_ref, r_ref, rt_ref, sp_ref, ct_ref, n_ref, cprev_ref, at_ref):
    n_lat = q_ref.shape[0] // MCHUNK
    n_ctx = kx_ref.shape[0] // MCHUNK

    gf = jnp.concatenate([gf_ref[...], gfx_ref[...]], axis=1)
    gi = jnp.concatenate([gi_ref[...], gix_ref[...]], axis=1)
    _gate_prep(gf, gi, n_lat, n_ctx, r_ref, rt_ref, sp_ref)

    jj = lax.broadcasted_iota(jnp.int32, (MCHUNK, MCHUNK), 0)
    ii = lax.broadcasted_iota(jnp.int32, (MCHUNK, MCHUNK), 1)
    visible = (jj <= ii, jj >= ii)
    ones_rows = jnp.ones((EXTRA, MCHUNK), BF16)

    def row(rt, quantity, d):
        i = 2 * quantity + d
        return rt[i:i + 1, :]

    for h in range(N_HEADS):
        hs = slice(h * HEAD_DIM, (h + 1) * HEAD_DIM)
        par = h % 2
        series0 = lambda e, h=h: e * SERIES + h * ROWS_PER_HEAD

        def state_step(d, e, k, vt, c_lat):
            rt_x = rt_ref[series0(e):series0(e) + EXTRA, :]
            s_row = sp_ref[series0(e) + d:series0(e) + d + 1, :]
            s2 = jnp.concatenate([s_row, s_row], axis=1)
            vet = (vt.astype(F32) * row(rt_x, Q_E, d)).astype(BF16)
            upd = _dot(jnp.concatenate([vet, rt_x.astype(BF16)], axis=0), k)
            i_e = HEAD_DIM + 2 * Q_E + d
            ek = upd[i_e:i_e + 1, :]
            ct_old = ct_ref[par, d]
            n_old = n_ref[par, d]
            if c_lat is not None:
                cprev_ref[par, d, c_lat, 0:HEAD_DIM, :] = ct_old.astype(BF16)
                cprev_ref[par, d, c_lat, HEAD_DIM:, :] = jnp.broadcast_to(n_old, (EXTRA, HEAD_DIM)).astype(BF16)
            ct_ref[par, d] = s2 * ct_old + upd[0:HEAD_DIM, :]
            n_ref[par, d] = s2 * n_old + ek

        ct_ref[par] = jnp.zeros(ct_ref.shape[1:], F32)
        n_ref[par] = jnp.zeros(n_ref.shape[1:], F32)
        for t in range(n_ctx):
            for d in range(2):
                cx = (n_ctx - 1 - t) if d else t
                rows = slice(cx * MCHUNK, (cx + 1) * MCHUNK)
                state_step(d, n_lat + cx, kx_ref[rows, hs], vtx_ref[cx, hs, :], None)
        for t in range(n_lat):
            for d in range(2):
                c = (n_lat - 1 - t) if d else t
                rows = slice(c * MCHUNK, (c + 1) * MCHUNK)
                state_step(d, c, k_ref[rows, hs], vt_ref[c, hs, :], c)

        def weights_stage(c, slot):
            rows = slice(c * MCHUNK, (c + 1) * MCHUNK)
            rt = rt_ref[series0(c):series0(c) + ROWS_PER_HEAD, :]
            st = _dot_nt(k_ref[rows, hs], q_ref[rows, hs])
            for d in range(2):
                r_col = r_ref[:, series0(c) + d:series0(c) + d + 1]
                w = jnp.exp2(jnp.where(visible[d], r_col - row(rt, Q_G, d), NEG))
                at_ref[par, slot, :, d * MCHUNK:(d + 1) * MCHUNK] = (w * st).astype(BF16)

        def readout_stage(c, slot):
            rows = slice(c * MCHUNK, (c + 1) * MCHUNK)
            q = q_ref[rows, hs]
            rt = rt_ref[series0(c):series0(c) + ROWS_PER_HEAD, :]
            at = at_ref[par, slot]
            num = _dot(jnp.concatenate([vt_ref[c, hs, :], ones_rows], axis=0), at)
            out = None
            for d in range(2):
                lanes = slice(d * MCHUNK, (d + 1) * MCHUNK)
                s_inter = row(rt, Q_S, d)
                inter = _dot_nt(cprev_ref[par, d, c], q)
                den = num[HEAD_DIM:HEAD_DIM + 1, lanes] + s_inter * inter[HEAD_DIM:HEAD_DIM + 1, :]
                inv = 1.0 / jnp.maximum(jnp.abs(den), row(rt, Q_FL, d))
                hd = (num[0:HEAD_DIM, lanes] + s_inter * inter[0:HEAD_DIM, :]) * inv
                out = hd if out is None else out + hd
            o_ref[c, hs, :] = out.astype(o_ref.dtype)

        weights_stage(0, 0)
        for c in range(n_lat - 1):
            readout_stage(c, c % 2)
            weights_stage(c + 1, (c + 1) % 2)
        readout_stage(n_lat - 1, (n_lat - 1) % 2)


def _mlstm(q, k, vt, kx, vtx, gf, gi, gfx, gix):
    B, L, _ = q.shape
    Lx = kx.shape[1]
    n_lat, n_ctx = L // MCHUNK, Lx // MCHUNK
    n_ext = n_lat + n_ctx
    n_tile = gf.shape[2] // LANES + gfx.shape[2] // LANES
    assert (n_ext - 1) * SERIES + (N_HEADS - 1) * ROWS_PER_HEAD + EXTRA <= n_tile * LANES
    tokens = lambda n: pl.BlockSpec((None, n, D_MODEL), lambda b: (b, 0, 0))
    chunks_t = lambda n: pl.BlockSpec((None, n, D_MODEL, MCHUNK), lambda b: (b, 0, 0, 0))
    gate_spec = lambda a: pl.BlockSpec((None,) + a.shape[1:], lambda b: (b, 0, 0))
    return pl.pallas_call(
        _mlstm_kernel,
        grid=(B,),
        in_specs=[
            tokens(L), tokens(L), chunks_t(n_lat), tokens(Lx), chunks_t(n_ctx),
            gate_spec(gf), gate_spec(gi), gate_spec(gfx), gate_spec(gix),
        ],
        out_specs=chunks_t(n_lat),
        out_shape=jax.ShapeDtypeStruct((B, n_lat, D_MODEL, MCHUNK), BF16),
        scratch_shapes=[
            pltpu.VMEM((MCHUNK, n_tile * LANES), F32),
            pltpu.VMEM((n_tile * LANES, MCHUNK), F32),
            pltpu.VMEM((n_tile * LANES, LANES), F32),
            pltpu.VMEM((2, 2, HEAD_DIM, HEAD_DIM), F32),
            pltpu.VMEM((2, 2, 1, HEAD_DIM), F32),
            pltpu.VMEM((2, 2, n_lat, HEAD_DIM + EXTRA, HEAD_DIM), BF16),
            pltpu.VMEM((2, 2, MCHUNK, 2 * MCHUNK), BF16),
        ],
        compiler_params=pltpu.CompilerParams(
            dimension_semantics=("parallel",), vmem_limit_bytes=VMEM_LIMIT),
        name="mlstm",
    )(q, k, vt, kx, vtx, gf, gi, gfx, gix)


def _rms(x, g):
    ms = jnp.mean(x * x, axis=-1, keepdims=True)
    return x * lax.rsqrt(ms + EPS) * g


def _merge_mlp_kernel(x_ref, p_ref, ht_ref, g1_ref, sh2_ref, sc2_ref, g2_ref,
                      n2_ref, nf_ref, gmh_ref, gsgu_ref, wa_ref, wb_ref, wo_ref, ws_ref, bst_ref,
                      w1_ref, w2_ref, o_ref, t_ref):
    tm = x_ref.shape[0]
    D = D_MODEL
    seg = lambda s: p_ref[:, s * D:(s + 1) * D].astype(F32)

    h_m = jnp.concatenate([ht_ref[mc].T for mc in range(tm // MCHUNK)], axis=0).astype(F32)
    hm = seg(0) * h_m
    gmh = gmh_ref[...]
    parts = []
    for h in range(N_HEADS):
        cols = slice(h * HEAD_DIM, (h + 1) * HEAD_DIM)
        parts.append(_rms(hm[:, cols], gmh[:, cols]).astype(BF16))
    ya = _dot(jnp.concatenate(parts, axis=1), wa_ref[...])

    u = _gelu(seg(1))
    vn = _rms(_gelu(seg(2)), gsgu_ref[...]).astype(BF16)
    bst = bst_ref[...]
    for cc in range(tm // CHUNK):
        rows = slice(cc * CHUNK, (cc + 1) * CHUNK)
        for g in range(N_GROUPS):
            cols = slice(g * GROUP_DIM, (g + 1) * GROUP_DIM)
            s = _dot(ws_ref[g], vn[rows, cols]) + bst[:, g:g + 1]
            t_ref[rows, cols] = (u[rows, cols] * s).astype(BF16)
    yb = _dot(t_ref[...], wb_ref[...])

    y = seg(3) * ya + seg(4) * yb
    mix = _dot(y.astype(BF16), wo_ref[...])
    x1 = x_ref[...] + g1_ref[...] * mix

    xn2 = (_rms(x1, n2_ref[...] * (1.0 + sc2_ref[...])) + sh2_ref[...]).astype(BF16)
    ff = D_FF // 4
    acc = jnp.zeros((tm, D), F32)
    for kk in range(3):
        hmid = jnp.maximum(_dot(xn2, w1_ref[:, kk * ff:(kk + 1) * ff]), 0.0)
        acc = acc + _dot((hmid * hmid).astype(BF16), w2_ref[kk * ff:(kk + 1) * ff, :])
    hmid = jnp.maximum(_dot(xn2, w1_ref[:, 3 * ff:]), 0.0)
    h2 = (hmid * hmid).astype(BF16)
    half = tm // 2
    for r in range(2):
        rows = slice(r * half, (r + 1) * half)
        x2 = x1[rows, :] + g2_ref[...] * (acc[rows, :] + _dot(h2[rows, :], w2_ref[3 * ff:, :]))
        o_ref[rows, :] = _rms(x2, nf_ref[...])


def _merge_mlp(x, pm, ht, mod3, norm2, norm_f, g_mh, g_sgu, w_a, w_b, w_out, w_s, b_st, w1, w2, tm):
    B, L, D = x.shape
    mod_spec = lambda k: pl.BlockSpec((None, 1, D), lambda b, i: (b, 0, k))
    return pl.pallas_call(
        _merge_mlp_kernel,
        grid=(B, L // tm),
        in_specs=[
            pl.BlockSpec((None, tm, D), lambda b, i: (b, i, 0)),
            pl.BlockSpec((None, tm, N_MERGE_SEG * D), lambda b, i: (b, i, 0)),
            pl.BlockSpec((None, tm // MCHUNK, D, MCHUNK), lambda b, i: (b, i, 0, 0)),
            mod_spec(2), mod_spec(3), mod_spec(4), mod_spec(5),
            _const_spec((1, D)), _const_spec((1, D)), _const_spec((1, D)), _const_spec((1, D)),
            _const_spec((D, D)), _const_spec((D, D)), _const_spec((D, D)),
            _const_spec(w_s.shape), _const_spec(b_st.shape),
            _const_spec((D, D_FF)), _const_spec((D_FF, D)),
        ],
        out_specs=pl.BlockSpec((None, tm, D), lambda b, i: (b, i, 0)),
        out_shape=jax.ShapeDtypeStruct((B, L, D), F32),
        scratch_shapes=[pltpu.VMEM((tm, D), BF16)],
        compiler_params=pltpu.CompilerParams(
            dimension_semantics=("parallel", "parallel"), vmem_limit_bytes=VMEM_LIMIT,
            allow_input_fusion=[False] * 11 + [True] * 7),
        name="merge_mlp",
    )(x, pm, ht, mod3, mod3, mod3, mod3, norm2, norm_f, g_mh, g_sgu, w_a, w_b, w_out, w_s, b_st, w1, w2)


def _gate_weights(w_gate, b_gate):
    D = w_gate.shape[0]

    def series(a, kind_of_dir):
        a4 = a.reshape(a.shape[:-1] + (4, N_HEADS))
        per_dir = jnp.stack([a4[..., kind_of_dir[0], :], a4[..., kind_of_dir[1], :]], axis=-1)
        dup = jnp.broadcast_to(per_dir[..., :, None, :], per_dir.shape[:-1] + (N_QUANT, 2))
        return dup.reshape(a.shape[:-1] + (SERIES,))

    wf, wi = series(w_gate, (1, 3)), series(w_gate, (0, 2))
    bf, bi = series(b_gate, (1, 3)), series(b_gate, (0, 2))
    eye = jnp.eye(CHUNKS_PER_TILE, dtype=w_gate.dtype)
    place = lambda w: jnp.einsum("ab,dk->adbk", eye, w).reshape(CHUNKS_PER_TILE, D, LANES)
    wgs = jnp.concatenate([place(wf), place(wi)], axis=2).astype(BF16)
    bgs = jnp.concatenate([jnp.tile(bf, CHUNKS_PER_TILE), jnp.tile(bi, CHUNKS_PER_TILE)])[None, :]
    return wgs, bgs


def kernel(x, c, ctx, c_ctx, norm1, norm2, w_mod, b_mod, w_in, conv_qk, b_gate, g_mh, w_a, w_s, b_s,
           g_sgu, w_b, w_out, w1, w2, norm_f):
    B, S, D = x.shape
    Lx = ctx.shape[1]
    assert D == D_MODEL and S % (MCHUNK * CHUNKS_PER_TILE) == 0 and S % MERGE_TM == 0
    assert Lx % MCHUNK == 0 and Lx // MCHUNK <= CHUNKS_PER_TILE
    assert w_mod.shape[0] == 1, "single-layer block"
    assert conv_qk.shape[1] == CONV_W
    W = D_MODEL
    off_g = 3 * W
    off_o = off_g + 4 * N_HEADS

    mod_rows = ((B + 1 + 7) // 8) * 8
    cc = jnp.zeros((mod_rows, D), F32).at[:B].set(c).at[B].set(c_ctx)
    mod = _modulation(cc, w_mod, b_mod)
    mod3 = mod.reshape(mod_rows, 1, 6 * D)

    wi = w_in[0]
    w_merge = wi[:, off_o:].astype(BF16)
    wq = wi[:, :W].astype(BF16)
    wk = wi[:, W:2 * W].astype(BF16)
    wvt = wi[:, 2 * W:off_g].T.astype(BF16)
    wgs, bgs = _gate_weights(wi[:, off_g:off_o], b_gate[0])

    n1 = norm1[0][None, :]
    conv = conv_qk[0]
    pm, q, k, vt, gf, gi = _inproj(x, mod3, lambda b: b, n1, wgs, bgs, conv, (w_merge, wq, wk, wvt), INPROJ_TM, True)
    kx, vtx, gfx, gix = _inproj(ctx, mod3, lambda b: B, n1, wgs, bgs, conv, (wk, wvt), Lx, False)

    ht = _mlstm(q, k, vt, kx, vtx, gf, gi, gfx, gix)

    return _merge_mlp(
        x, pm, ht, mod3, norm2[0][None, :], norm_f[None, :], g_mh[0][None, :], g_sgu[0][None, :],
        w_a[0].astype(BF16), w_b[0].astype(BF16), w_out[0].astype(BF16),
        w_s[0].astype(BF16), b_s[0].T, w1[0].astype(BF16), w2[0].astype(BF16), MERGE_TM)
```

```python
import jax
import jax.numpy as jnp
from jax import lax
from jax.experimental import pallas as pl
from jax.experimental.pallas import tpu as pltpu

D_MODEL = 1024
N_HEADS = 4
HEAD_DIM = D_MODEL // N_HEADS
CHUNK = 128
MCHUNK = 256
GROUP_DIM = 128
N_GROUPS = D_MODEL // GROUP_DIM
D_FF = 4 * D_MODEL
CONV_W = 3
NEG = -1e30
LOG2E = 1.4426950408889634
EPS = 1e-6
LANES = 128
HALO = 16
EXTRA = 16

N_QUANT = 4
ROWS_PER_HEAD = 2 * N_QUANT
SERIES = N_HEADS * ROWS_PER_HEAD
CHUNKS_PER_TILE = LANES // SERIES
Q_G, Q_E, Q_S, Q_FL = 0, 1, 2, 3

N_MERGE_SEG = 5

VMEM_LIMIT = 60000 * 1024
INPROJ_TM = 2 * MCHUNK
MERGE_TM = 512
MOD_TN = 1536

F32 = jnp.float32
BF16 = jnp.bfloat16


def _dot(a, b):
    return jnp.dot(a, b, preferred_element_type=F32)


def _dot_nt(a, b):
    return lax.dot_general(a, b, (((1,), (1,)), ((), ())), preferred_element_type=F32)


def _split_hi_lo(x):
    hi = x.astype(BF16)
    lo = (x - hi.astype(F32)).astype(BF16)
    return hi, lo


def _sigmoid(x):
    return 0.5 * jnp.tanh(0.5 * x) + 0.5


def _silu(x):
    half = 0.5 * x
    return half * jnp.tanh(half) + half


def _gelu(x):
    c = 0.7978845608028654
    half = 0.5 * x
    return half * jnp.tanh(x * (c + (c * 0.044715) * (x * x))) + half


MERGE_ACT = (_sigmoid, None, None, _sigmoid, _sigmoid)


def _const_spec(shape):
    nd = len(shape)
    return pl.BlockSpec(shape, lambda *_: (0,) * nd, pipeline_mode=pl.Buffered(1))


def _mod_kernel(c_ref, w_ref, b_ref, o_ref):
    s = _silu(c_ref[...])
    s_hi, s_lo = _split_hi_lo(s)
    w_hi, w_lo = _split_hi_lo(w_ref[...])
    acc = _dot(s_hi, w_hi) + _dot(s_hi, w_lo) + _dot(s_lo, w_hi)
    o_ref[...] = acc + b_ref[...]


def _modulation(cc, w_mod, b_mod):
    rows, d = cc.shape
    n = w_mod.shape[2]
    tn = MOD_TN
    return pl.pallas_call(
        _mod_kernel,
        grid=(n // tn,),
        in_specs=[
            pl.BlockSpec((rows, d), lambda j: (0, 0)),
            pl.BlockSpec((None, d, tn), lambda j: (0, 0, j)),
            pl.BlockSpec((1, tn), lambda j: (0, j)),
        ],
        out_specs=pl.BlockSpec((rows, tn), lambda j: (0, j)),
        out_shape=jax.ShapeDtypeStruct((rows, n), F32),
        compiler_params=pltpu.CompilerParams(
            dimension_semantics=("arbitrary",), vmem_limit_bytes=VMEM_LIMIT),
        name="modulation",
    )(cc, w_mod, b_mod)


def _log_sigmoid(x):
    return jnp.minimum(x, 0.0) - jnp.log1p(jnp.exp(-jnp.abs(x)))


def _make_inproj_kernel(tm, n_tiles, latent):
    n_cc = tm // CHUNK
    n_mc = tm // MCHUNK
    tiles_per_gate_tile = CHUNKS_PER_TILE // n_mc

    def kernel(*refs):
        if latent:
            (x_ref, xp_ref, xnx_ref, sh_ref, sc_ref, g_ref, wg_ref, bg_ref, cw_ref, wm_ref, wq_ref, wk_ref, wvt_ref,
             pm_ref, q_ref, k_ref, vt_ref, gf_ref, gi_ref, xe_ref, peq_ref, pek_ref) = refs
        else:
            (x_ref, xp_ref, xnx_ref, sh_ref, sc_ref, g_ref, wg_ref, bg_ref, cw_ref, wk_ref, wvt_ref,
             k_ref, vt_ref, gf_ref, gi_ref, xe_ref, pek_ref) = refs
        i = pl.program_id(1)

        gain = g_ref[...] * (1.0 + sc_ref[...])

        def normed(x):
            ms = jnp.mean(x * x, axis=-1, keepdims=True)
            return x * lax.rsqrt(ms + EPS) * gain + sh_ref[...]

        xn = normed(x_ref[...]).astype(BF16)
        xe_ref[HALO:HALO + tm, :] = xn
        xe_ref[0:HALO, :] = jnp.where(i > 0, normed(xp_ref[...]), 0.0).astype(BF16)
        xe_ref[HALO + tm:, :] = jnp.where(i < n_tiles - 1, normed(xnx_ref[...]), 0.0).astype(BF16)

        pad = 8
        n_ext = CHUNK + 2 * pad

        def conv_silu(pe_ref, w_ref, cw, scale, out_ref):
            pe_ref[...] = _dot(xe_ref[...], w_ref[...])
            for cc in range(n_cc):
                r = HALO + cc * CHUNK
                x_ext = pe_ref[r - pad:r + CHUNK + pad, :]
                prev = pltpu.roll(x_ext, 1, 0)[pad:pad + CHUNK, :]
                nxt = pltpu.roll(x_ext, n_ext - 1, 0)[pad:pad + CHUNK, :]
                y = prev * cw[0:1, :] + x_ext[pad:pad + CHUNK, :] * cw[1:2, :] + nxt * cw[2:3, :]
                y = _silu(y)
                if scale != 1.0:
                    y = y * scale
                out_ref[cc * CHUNK:(cc + 1) * CHUNK, :] = y.astype(BF16)

        vt = _dot_nt(wvt_ref[...], xn)
        for mc in range(n_mc):
            vt_ref[mc] = vt[:, mc * MCHUNK:(mc + 1) * MCHUNK].astype(BF16)

        cw = cw_ref[...]
        if latent:
            conv_silu(peq_ref, wq_ref, cw[:, :D_MODEL], 1.0, q_ref)
        conv_silu(pek_ref, wk_ref, cw[:, D_MODEL:], HEAD_DIM ** -0.5, k_ref)

        if latent:
            for s in range(N_MERGE_SEG):
                cols = slice(s * D_MODEL, (s + 1) * D_MODEL)
                z = _dot(xn, wm_ref[:, cols])
                if MERGE_ACT[s] is not None:
                    z = MERGE_ACT[s](z)
                pm_ref[:, cols] = z.astype(BF16)

        sub = i % tiles_per_gate_tile
        acc = _dot(xn[0:MCHUNK], wg_ref[sub * n_mc])
        for mc in range(1, n_mc):
            acc = acc + _dot(xn[mc * MCHUNK:(mc + 1) * MCHUNK], wg_ref[sub * n_mc + mc])
        acc = acc + bg_ref[...]
        group = lax.broadcasted_iota(jnp.int32, (1, LANES), 1) // SERIES
        own = (group >= sub * n_mc) & (group < (sub + 1) * n_mc)
        new_f = jnp.where(own, _log_sigmoid(acc[:, :LANES]), 0.0)
        new_i = jnp.where(own, acc[:, LANES:], 0.0)

        @pl.when(sub == 0)
        def _():
            gf_ref[...] = new_f
            gi_ref[...] = new_i

        @pl.when(sub != 0)
        def _():
            gf_ref[...] += new_f
            gi_ref[...] += new_i

    return kernel


def _inproj(x, mod3, mod_row_of_batch, norm_g, wgs, bgs, conv_qk, weights, tm, latent):
    B, L, D = x.shape
    n_tiles = L // tm
    n_mc = tm // MCHUNK
    tiles_per_gate_tile = CHUNKS_PER_TILE // n_mc
    n_gate_tiles = -(-n_tiles // tiles_per_gate_tile)
    hb = tm // HALO
    n_hblk = L // HALO
    row = mod_row_of_batch
    tok = lambda width: pl.BlockSpec((None, tm, width), lambda b, i: (b, i, 0))
    tok_shape = lambda width: jax.ShapeDtypeStruct((B, L, width), BF16)
    widths = [w.shape[1] for w in weights[:-1]]
    gate_spec = pl.BlockSpec((None, MCHUNK, LANES), lambda b, i: (b, 0, i // tiles_per_gate_tile))
    gate_shape = jax.ShapeDtypeStruct((B, MCHUNK, n_gate_tiles * LANES), F32)
    vt_spec = pl.BlockSpec((None, n_mc, D, MCHUNK), lambda b, i: (b, i, 0, 0))
    vt_shape = jax.ShapeDtypeStruct((B, L // MCHUNK, D, MCHUNK), BF16)
    return pl.pallas_call(
        _make_inproj_kernel(tm, n_tiles, latent),
        grid=(B, n_tiles),
        in_specs=[
            pl.BlockSpec((None, tm, D), lambda b, i: (b, i, 0)),
            pl.BlockSpec((None, HALO, D), lambda b, i: (b, jnp.maximum(i * hb - 1, 0), 0)),
            pl.BlockSpec((None, HALO, D), lambda b, i: (b, jnp.minimum((i + 1) * hb, n_hblk - 1), 0)),
            pl.BlockSpec((None, 1, D), lambda b, i: (row(b), 0, 0)),
            pl.BlockSpec((None, 1, D), lambda b, i: (row(b), 0, 1)),
            _const_spec((1, D)), _const_spec(wgs.shape), _const_spec(bgs.shape), _const_spec(conv_qk.shape),
        ] + [_const_spec(w.shape) for w in weights],
        out_specs=[tok(w) for w in widths] + [vt_spec, gate_spec, gate_spec],
        out_shape=[tok_shape(w) for w in widths] + [vt_shape, gate_shape, gate_shape],
        scratch_shapes=[pltpu.VMEM((tm + 2 * HALO, D), BF16)]
        + [pltpu.VMEM((tm + 2 * HALO, D), F32)] * (2 if latent else 1),
        compiler_params=pltpu.CompilerParams(
            dimension_semantics=("parallel", "arbitrary"), vmem_limit_bytes=VMEM_LIMIT),
        name="inproj_latent" if latent else "inproj_context",
    )(x, x, x, mod3, mod3, norm_g, wgs, bgs, conv_qk, *weights)


def _cummax_rows(x, reverse):
    n = x.shape[0]
    rows = lax.broadcasted_iota(jnp.int32, x.shape, 0)
    k = 1
    while k < n:
        if reverse:
            shifted = jnp.where(rows < n - k, pltpu.roll(x, n - k, 0), NEG)
        else:
            shifted = jnp.where(rows >= k, pltpu.roll(x, k, 0), NEG)
        x = jnp.maximum(x, shifted)
        k *= 2
    return x


def _gate_prep(gf, gi, n_lat, n_ctx, r_ref, rt_ref, sp_ref):
    n_ext = n_lat + n_ctx
    width = gf.shape[1]
    n_tile = width // LANES
    ii = lax.broadcasted_iota(jnp.int32, (MCHUNK, MCHUNK), 0)
    jj = lax.broadcasted_iota(jnp.int32, (MCHUNK, MCHUNK), 1)
    tril = jnp.where(ii >= jj, 1.0, 0.0).astype(BF16)
    triu = jnp.where(ii <= jj, 1.0, 0.0).astype(BF16)
    lane = lax.broadcasted_iota(jnp.int32, (1, width), 1)
    lane_bwd = (lane % 2) == 1
    lane_q = (lane // 2) % N_QUANT

    hi, lo = _split_hi_lo(gf)
    b = jnp.where(lane_bwd, _dot(triu, hi) + _dot(triu, lo), _dot(tril, hi) + _dot(tril, lo))
    tot = jnp.where(lane_bwd, b[0:1, :], b[MCHUNK - 1:MCHUNK, :])
    r = gi - b
    cm = jnp.where(lane_bwd, _cummax_rows(r, True), _cummax_rows(r, False))
    rmax = jnp.where(lane_bwd, cm[0:1, :], cm[MCHUNK - 1:MCHUNK, :])

    def to_rows(v):
        return [jnp.broadcast_to(v[:, t * LANES:(t + 1) * LANES], (LANES, LANES)).T for t in range(n_tile)]

    tot_t, rmax_t = to_rows(tot), to_rows(rmax)

    def slab(tiles, e):
        t, k = divmod(e, CHUNKS_PER_TILE)
        return tiles[t][k * SERIES:(k + 1) * SERIES, :]

    ctx_ids = list(range(n_lat, n_ext))
    orders = (ctx_ids + list(range(n_lat)), ctx_ids[::-1] + list(range(n_lat - 1, -1, -1)))
    m_prev, m_new = [{}, {}], [{}, {}]
    for d, order in enumerate(orders):
        m = jnp.zeros((SERIES, LANES), F32)
        for e in order:
            m_prev[d][e] = m
            tot_e = slab(tot_t, e)
            m = jnp.maximum(tot_e + m, tot_e + slab(rmax_t, e))
            m_new[d][e] = m
    row_bwd = (lax.broadcasted_iota(jnp.int32, (SERIES, LANES), 0) % 2) == 1
    mp_slabs = [jnp.where(row_bwd, m_prev[1][e], m_prev[0][e]) for e in range(n_ext)]
    mn_slabs = [jnp.where(row_bwd, m_new[1][e], m_new[0][e]) for e in range(n_ext)]
    sp_slabs = [jnp.exp(slab(tot_t, e) + mp_slabs[e] - mn_slabs[e]) for e in range(n_ext)]

    def tiles_of(slabs):
        pad = [jnp.zeros((SERIES, LANES), F32)] * (n_tile * CHUNKS_PER_TILE - n_ext)
        full = slabs + pad
        return [jnp.concatenate(full[t * CHUNKS_PER_TILE:(t + 1) * CHUNKS_PER_TILE], axis=0) for t in range(n_tile)]

    def to_lanes(slabs):
        return jnp.concatenate([t.T[0:1, :] for t in tiles_of(slabs)], axis=1)

    mp = to_lanes(mp_slabs)
    mn = to_lanes(mn_slabs)

    g = jnp.maximum(mp, cm)
    e_w = jnp.exp(tot + r - mn)
    s_inter = jnp.exp(mp - g)
    floor = jnp.exp(-(b + g))
    packed = jnp.where(lane_q == Q_G, g * LOG2E,
                       jnp.where(lane_q == Q_E, e_w, jnp.where(lane_q == Q_S, s_inter, floor)))

    sp_tiles = tiles_of(sp_slabs)
    for t in range(n_tile):
        rt_ref[t * LANES:(t + 1) * LANES, :] = packed[:, t * LANES:(t + 1) * LANES].T
        sp_ref[t * LANES:(t + 1) * LANES, :] = sp_tiles[t]
    r_ref[...] = r * LOG2E


def _mlstm_kernel(q_ref, k_ref, vt_ref, kx_ref, vtx_ref, gf_ref, gi_ref, gfx_ref, gix_ref,
                  o_ref, r_ref, rt_ref, sp_ref, ct_ref, n_ref, cprev_ref, at_ref):
    n_lat = q_ref.shape[0] // MCHUNK
    n_ctx = kx_ref.shape[0] // MCHUNK

    gf = jnp.concatenate([gf_ref[...], gfx_ref[...]], axis=1)
    gi = jnp.concatenate([gi_ref[...], gix_ref[...]], axis=1)
    _gate_prep(gf, gi, n_lat, n_ctx, r_ref, rt_ref, sp_ref)

    jj = lax.broadcasted_iota(jnp.int32, (MCHUNK, MCHUNK), 0)
    ii = lax.broadcasted_iota(jnp.int32, (MCHUNK, MCHUNK), 1)
    visible = (jj <= ii, jj >= ii)
    ones_rows = jnp.ones((EXTRA, MCHUNK), BF16)

    def row(rt, quantity, d):
        i = 2 * quantity + d
        return rt[i:i + 1, :]

    for h in range(N_HEADS):
        hs = slice(h * HEAD_DIM, (h + 1) * HEAD_DIM)
        par = h % 2
        series0 = lambda e, h=h: e * SERIES + h * ROWS_PER_HEAD

        def state_step(d, e, k, vt, c_lat):
            rt_x = rt_ref[series0(e):series0(e) + EXTRA, :]
            s_row = sp_ref[series0(e) + d:series0(e) + d + 1, :]
            s2 = jnp.concatenate([s_row, s_row], axis=1)
            vet = (vt.astype(F32) * row(rt_x, Q_E, d)).astype(BF16)
            upd = _dot(jnp.concatenate([vet, rt_x.astype(BF16)], axis=0), k)
            i_e = HEAD_DIM + 2 * Q_E + d
            ek = upd[i_e:i_e + 1, :]
            ct_old = ct_ref[par, d]
            n_old = n_ref[par, d]
            if c_lat is not None:
                cprev_ref[par, d, c_lat, 0:HEAD_DIM, :] = ct_old.astype(BF16)
                cprev_ref[par, d, c_lat, HEAD_DIM:, :] = jnp.broadcast_to(n_old, (EXTRA, HEAD_DIM)).astype(BF16)
            ct_ref[par, d] = s2 * ct_old + upd[0:HEAD_DIM, :]
            n_ref[par, d] = s2 * n_old + ek

        ct_ref[par] = jnp.zeros(ct_ref.shape[1:], F32)
        n_ref[par] = jnp.zeros(n_ref.shape[1:], F32)
        for t in range(n_ctx):
            for d in range(2):
                cx = (n_ctx - 1 - t) if d else t
                rows = slice(cx * MCHUNK, (cx + 1) * MCHUNK)
                state_step(d, n_lat + cx, kx_ref[rows, hs], vtx_ref[cx, hs, :], None)
        for t in range(n_lat):
            for d in range(2):
                c = (n_lat - 1 - t) if d else t
                rows = slice(c * MCHUNK, (c + 1) * MCHUNK)
                state_step(d, c, k_ref[rows, hs], vt_ref[c, hs, :], c)

        def weights_stage(c, slot):
            rows = slice(c * MCHUNK, (c + 1) * MCHUNK)
            rt = rt_ref[series0(c):series0(c) + ROWS_PER_HEAD, :]
            st = _dot_nt(k_ref[rows, hs], q_ref[rows, hs])
            for d in range(2):
                r_col = r_ref[:, series0(c) + d:series0(c) + d + 1]
                w = jnp.exp2(jnp.where(visible[d], r_col - row(rt, Q_G, d), NEG))
                at_ref[par, slot, :, d * MCHUNK:(d + 1) * MCHUNK] = (w * st).astype(BF16)

        def readout_stage(c, slot):
            rows = slice(c * MCHUNK, (c + 1) * MCHUNK)
            q = q_ref[rows, hs]
            rt = rt_ref[series0(c):series0(c) + ROWS_PER_HEAD, :]
            at = at_ref[par, slot]
            num = _dot(jnp.concatenate([vt_ref[c, hs, :], ones_rows], axis=0), at)
            out = None
            for d in range(2):
                lanes = slice(d * MCHUNK, (d + 1) * MCHUNK)
                s_inter = row(rt, Q_S, d)
                inter = _dot_nt(cprev_ref[par, d, c], q)
                den = num[HEAD_DIM:HEAD_DIM + 1, lanes] + s_inter * inter[HEAD_DIM:HEAD_DIM + 1, :]
                inv = 1.0 / jnp.maximum(jnp.abs(den), row(rt, Q_FL, d))
                hd = (num[0:HEAD_DIM, lanes] + s_inter * inter[0:HEAD_DIM, :]) * inv
                out = hd if out is None else out + hd
            o_ref[c, hs, :] = out.astype(o_ref.dtype)

        weights_stage(0, 0)
        for c in range(n_lat - 1):
            readout_stage(c, c % 2)
            weights_stage(c + 1, (c + 1) % 2)
        readout_stage(n_lat - 1, (n_lat - 1) % 2)


def _mlstm(q, k, vt, kx, vtx, gf, gi, gfx, gix):
    B, L, _ = q.shape
    Lx = kx.shape[1]
    n_lat, n_ctx = L // MCHUNK, Lx // MCHUNK
    n_ext = n_lat + n_ctx
    n_tile = gf.shape[2] // LANES + gfx.shape[2] // LANES
    assert (n_ext - 1) * SERIES + (N_HEADS - 1) * ROWS_PER_HEAD + EXTRA <= n_tile * LANES
    tokens = lambda n: pl.BlockSpec((None, n, D_MODEL), lambda b: (b, 0, 0))
    chunks_t = lambda n: pl.BlockSpec((None, n, D_MODEL, MCHUNK), lambda b: (b, 0, 0, 0))
    gate_spec = lambda a: pl.BlockSpec((None,) + a.shape[1:], lambda b: (b, 0, 0))
    return pl.pallas_call(
        _mlstm_kernel,
        grid=(B,),
        in_specs=[
            tokens(L), tokens(L), chunks_t(n_lat), tokens(Lx), chunks_t(n_ctx),
            gate_spec(gf), gate_spec(gi), gate_spec(gfx), gate_spec(gix),
        ],
        out_specs=chunks_t(n_lat),
        out_shape=jax.ShapeDtypeStruct((B, n_lat, D_MODEL, MCHUNK), BF16),
        scratch_shapes=[
            pltpu.VMEM((MCHUNK, n_tile * LANES), F32),
            pltpu.VMEM((n_tile * LANES, MCHUNK), F32),
            pltpu.VMEM((n_tile * LANES, LANES), F32),
            pltpu.VMEM((2, 2, HEAD_DIM, HEAD_DIM), F32),
            pltpu.VMEM((2, 2, 1, HEAD_DIM), F32),
            pltpu.VMEM((2, 2, n_lat, HEAD_DIM + EXTRA, HEAD_DIM), BF16),
            pltpu.VMEM((2, 2, MCHUNK, 2 * MCHUNK), BF16),
        ],
        compiler_params=pltpu.CompilerParams(
            dimension_semantics=("parallel",), vmem_limit_bytes=VMEM_LIMIT),
        name="mlstm",
    )(q, k, vt, kx, vtx, gf, gi, gfx, gix)


def _rms(x, g):
    ms = jnp.mean(x * x, axis=-1, keepdims=True)
    return x * lax.rsqrt(ms + EPS) * g


def _merge_mlp_kernel(x_ref, p_ref, ht_ref, g1_ref, sh2_ref, sc2_ref, g2_ref,
                      n2_ref, nf_ref, gmh_ref, gsgu_ref, wa_ref, wb_ref, wo_ref, ws_ref, bst_ref,
                      w1_ref, w2_ref, o_ref, t_ref):
    tm = x_ref.shape[0]
    D = D_MODEL
    seg = lambda s: p_ref[:, s * D:(s + 1) * D].astype(F32)

    h_m = jnp.concatenate([ht_ref[mc].T for mc in range(tm // MCHUNK)], axis=0).astype(F32)
    hm = seg(0) * h_m
    gmh = gmh_ref[...]
    parts = []
    for h in range(N_HEADS):
        cols = slice(h * HEAD_DIM, (h + 1) * HEAD_DIM)
        parts.append(_rms(hm[:, cols], gmh[:, cols]).astype(BF16))
    ya = _dot(jnp.concatenate(parts, axis=1), wa_ref[...])

    u = _gelu(seg(1))
    vn = _rms(_gelu(seg(2)), gsgu_ref[...]).astype(BF16)
    bst = bst_ref[...]
    for cc in range(tm // CHUNK):
        rows = slice(cc * CHUNK, (cc + 1) * CHUNK)
        for g in range(N_GROUPS):
            cols = slice(g * GROUP_DIM, (g + 1) * GROUP_DIM)
            s = _dot(ws_ref[g], vn[rows, cols]) + bst[:, g:g + 1]
            t_ref[rows, cols] = (u[rows, cols] * s).astype(BF16)
    yb = _dot(t_ref[...], wb_ref[...])

    y = seg(3) * ya + seg(4) * yb
    mix = _dot(y.astype(BF16), wo_ref[...])
    x1 = x_ref[...] + g1_ref[...] * mix

    xn2 = (_rms(x1, n2_ref[...] * (1.0 + sc2_ref[...])) + sh2_ref[...]).astype(BF16)
    ff = D_FF // 4
    acc = jnp.zeros((tm, D), F32)
    for kk in range(3):
        hmid = jnp.maximum(_dot(xn2, w1_ref[:, kk * ff:(kk + 1) * ff]), 0.0)
        acc = acc + _dot((hmid * hmid).astype(BF16), w2_ref[kk * ff:(kk + 1) * ff, :])
    hmid = jnp.maximum(_dot(xn2, w1_ref[:, 3 * ff:]), 0.0)
    h2 = (hmid * hmid).astype(BF16)
    half = tm // 2
    for r in range(2):
        rows = slice(r * half, (r + 1) * half)
        x2 = x1[rows, :] + g2_ref[...] * (acc[rows, :] + _dot(h2[rows, :], w2_ref[3 * ff:, :]))
        o_ref[rows, :] = _rms(x2, nf_ref[...])


def _merge_mlp(x, pm, ht, mod3, norm2, norm_f, g_mh, g_sgu, w_a, w_b, w_out, w_s, b_st, w1, w2, tm):
    B, L, D = x.shape
    mod_spec = lambda k: pl.BlockSpec((None, 1, D), lambda b, i: (b, 0, k))
    return pl.pallas_call(
        _merge_mlp_kernel,
        grid=(B, L // tm),
        in_specs=[
            pl.BlockSpec((None, tm, D), lambda b, i: (b, i, 0)),
            pl.BlockSpec((None, tm, N_MERGE_SEG * D), lambda b, i: (b, i, 0)),
            pl.BlockSpec((None, tm // MCHUNK, D, MCHUNK), lambda b, i: (b, i, 0, 0)),
            mod_spec(2), mod_spec(3), mod_spec(4), mod_spec(5),
            _const_spec((1, D)), _const_spec((1, D)), _const_spec((1, D)), _const_spec((1, D)),
            _const_spec((D, D)), _const_spec((D, D)), _const_spec((D, D)),
            _const_spec(w_s.shape), _const_spec(b_st.shape),
            _const_spec((D, D_FF)), _const_spec((D_FF, D)),
        ],
        out_specs=pl.BlockSpec((None, tm, D), lambda b, i: (b, i, 0)),
        out_shape=jax.ShapeDtypeStruct((B, L, D), F32),
        scratch_shapes=[pltpu.VMEM((tm, D), BF16)],
        compiler_params=pltpu.CompilerParams(
            dimension_semantics=("parallel", "parallel"), vmem_limit_bytes=VMEM_LIMIT),
        name="merge_mlp",
    )(x, pm, ht, mod3, mod3, mod3, mod3, norm2, norm_f, g_mh, g_sgu, w_a, w_b, w_out, w_s, b_st, w1, w2)


def _gate_weights(w_gate, b_gate):
    D = w_gate.shape[0]

    def series(a, kind_of_dir):
        a4 = a.reshape(a.shape[:-1] + (4, N_HEADS))
        per_dir = jnp.stack([a4[..., kind_of_dir[0], :], a4[..., kind_of_dir[1], :]], axis=-1)
        dup = jnp.broadcast_to(per_dir[..., :, None, :], per_dir.shape[:-1] + (N_QUANT, 2))
        return dup.reshape(a.shape[:-1] + (SERIES,))

    wf, wi = series(w_gate, (1, 3)), series(w_gate, (0, 2))
    bf, bi = series(b_gate, (1, 3)), series(b_gate, (0, 2))
    eye = jnp.eye(CHUNKS_PER_TILE, dtype=w_gate.dtype)
    place = lambda w: jnp.einsum("ab,dk->adbk", eye, w).reshape(CHUNKS_PER_TILE, D, LANES)
    wgs = jnp.concatenate([place(wf), place(wi)], axis=2).astype(BF16)
    bgs = jnp.concatenate([jnp.tile(bf, CHUNKS_PER_TILE), jnp.tile(bi, CHUNKS_PER_TILE)])[None, :]
    return wgs, bgs


def kernel(x, c, ctx, c_ctx, norm1, norm2, w_mod, b_mod, w_in, conv_qk, b_gate, g_mh, w_a, w_s, b_s,
           g_sgu, w_b, w_out, w1, w2, norm_f):
    B, S, D = x.shape
    Lx = ctx.shape[1]
    assert D == D_MODEL and S % (MCHUNK * CHUNKS_PER_TILE) == 0 and S % MERGE_TM == 0
    assert Lx % MCHUNK == 0 and Lx // MCHUNK <= CHUNKS_PER_TILE
    assert w_mod.shape[0] == 1, "single-layer block"
    assert conv_qk.shape[1] == CONV_W
    W = D_MODEL
    off_g = 3 * W
    off_o = off_g + 4 * N_HEADS

    mod_rows = ((B + 1 + 7) // 8) * 8
    cc = jnp.zeros((mod_rows, D), F32).at[:B].set(c).at[B].set(c_ctx)
    mod = _modulation(cc, w_mod, b_mod)
    mod3 = mod.reshape(mod_rows, 1, 6 * D)

    wi = w_in[0]
    w_merge = wi[:, off_o:].astype(BF16)
    wq = wi[:, :W].astype(BF16)
    wk = wi[:, W:2 * W].astype(BF16)
    wvt = wi[:, 2 * W:off_g].T.astype(BF16)
    wgs, bgs = _gate_weights(wi[:, off_g:off_o], b_gate[0])

    n1 = norm1[0][None, :]
    conv = conv_qk[0]
    pm, q, k, vt, gf, gi = _inproj(x, mod3, lambda b: b, n1, wgs, bgs, conv, (w_merge, wq, wk, wvt), INPROJ_TM, True)
    kx, vtx, gfx, gix = _inproj(ctx, mod3, lambda b: B, n1, wgs, bgs, conv, (wk, wvt), Lx, False)

    ht = _mlstm(q, k, vt, kx, vtx, gf, gi, gfx, gix)

    return _merge_mlp(
        x, pm, ht, mod3, norm2[0][None, :], norm_f[None, :], g_mh[0][None, :], g_sgu[0][None, :],
        w_a[0].astype(BF16), w_b[0].astype(BF16), w_out[0].astype(BF16),
        w_s[0].astype(BF16), b_s[0].T, w1[0].astype(BF16), w2[0].astype(BF16), MERGE_TM)
```
